```python
import math
import jax
import jax.numpy as jnp
from jax import lax
import numpy as np

D_MODEL = 2048
BATCH = 4
SEQ = 2048
DEPTH = 2


CHUNK = 64
Q_BLOCK = 128
N_MIXERS = 2
RMS_EPS = 1e-6
NEG_INF = -1e30

MLA_HEADS = 16
MLA_Q_RANK = 512
MLA_KV_RANK = 512
MLA_NOPE = 128
MLA_ROPE = 64
MLA_V = 128
MLA_QK = MLA_NOPE + MLA_ROPE
MLA_IN = MLA_Q_RANK + MLA_KV_RANK + MLA_ROPE
ROPE_THETA = 10000.0

FOX_HEADS = 16
FOX_HEAD_DIM = D_MODEL // FOX_HEADS
FOX_IN = 4 * D_MODEL + FOX_HEADS

PEER_HEADS = 8
PEER_N_KEYS = 128
PEER_N_EXPERTS = PEER_N_KEYS * PEER_N_KEYS
PEER_D_KEY = 256
PEER_D_HALF = PEER_D_KEY // 2
PEER_TOPK = 16
PEER_TOKEN_BLOCK = 128

kernel_name = 'hybrid_mla_fox_peer_streaming_encoder'


def rms_norm(x, g):
    xf = x.astype(jnp.float32)
    y = xf * lax.rsqrt(jnp.mean(xf * xf, axis=-1, keepdims=True) + RMS_EPS)
    return (y * g.astype(jnp.float32)).astype(x.dtype)


def rotate(x, cos, sin):
    x1, x2 = jnp.split(x, 2, axis=-1)
    return jnp.concatenate([x1 * cos - x2 * sin, x2 * cos + x1 * sin], axis=-1)


def blocked_attention(q, k, v, scale, chunk_causal, log_decay=None):
    seq = q.shape[1]
    outs = []
    for qb in range(seq // Q_BLOCK):
        q0 = qb * Q_BLOCK
        q1 = q0 + Q_BLOCK
        logits = jnp.einsum('bqhd,bkhd->bhqk', q[:, q0:q1], k[:, :q1]).astype(jnp.float32) * scale
        t = jnp.arange(q0, q1)[:, None]
        s = jnp.arange(q1)[None, :]
        if chunk_causal:
            allowed = (s // CHUNK) <= (t // CHUNK)
        else:
            allowed = s <= t
        if log_decay is not None:
            logits = logits + (log_decay[:, :, q0:q1, None] - log_decay[:, :, None, :q1])
        logits = jnp.where(allowed, logits, NEG_INF)
        p = jax.nn.softmax(logits, axis=-1).astype(v.dtype)
        outs.append(jnp.einsum('bhqk,bkhd->bqhd', p, v[:, :q1]))
    return jnp.concatenate(outs, axis=1)


def mla_mixer(h, cos, sin, w_in, g_qa, w_qb, g_kva, w_kvb, g_q, g_k, w_o):
    b, s, _ = h.shape
    z = h @ w_in
    c_q, c_kv, k_rope = jnp.split(z, [MLA_Q_RANK, MLA_Q_RANK + MLA_KV_RANK], axis=-1)
    q = (rms_norm(c_q, g_qa) @ w_qb).reshape(b, s, MLA_HEADS, MLA_QK)
    kv = (rms_norm(c_kv, g_kva) @ w_kvb).reshape(b, s, MLA_HEADS, MLA_NOPE + MLA_V)
    k_nope, v = jnp.split(kv, [MLA_NOPE], axis=-1)
    k_rope = jnp.broadcast_to(k_rope[:, :, None, :], (b, s, MLA_HEADS, MLA_ROPE))
    k = jnp.concatenate([k_nope, k_rope], axis=-1)
    q = rms_norm(q, g_q)
    k = rms_norm(k, g_k)
    q = jnp.concatenate([q[..., :MLA_NOPE], rotate(q[..., MLA_NOPE:], cos, sin)], axis=-1)
    k = jnp.concatenate([k[..., :MLA_NOPE], rotate(k[..., MLA_NOPE:], cos, sin)], axis=-1)
    o = blocked_attention(q, k, v, 1.0 / math.sqrt(MLA_QK), chunk_causal=True)
    return o.reshape(b, s, MLA_HEADS * MLA_V) @ w_o


def fox_mixer(h, w_in, b_f, g_q, g_k, w_o):
    b, s, _ = h.shape
    z = h @ w_in
    q, k, v, gate, f_logit = jnp.split(z, [D_MODEL, 2 * D_MODEL, 3 * D_MODEL, 4 * D_MODEL], axis=-1)
    q = rms_norm(q.reshape(b, s, FOX_HEADS, FOX_HEAD_DIM), g_q)
    k = rms_norm(k.reshape(b, s, FOX_HEADS, FOX_HEAD_DIM), g_k)
    v = v.reshape(b, s, FOX_HEADS, FOX_HEAD_DIM)
    log_f = jax.nn.log_sigmoid((f_logit + b_f).astype(jnp.float32))
    c = jnp.cumsum(log_f, axis=1).transpose(0, 2, 1)
    o = blocked_attention(q, k, v, 1.0 / math.sqrt(FOX_HEAD_DIM), chunk_causal=False, log_decay=c)
    o = o.reshape(b, s, D_MODEL) * jax.nn.sigmoid(gate)
    return o @ w_o


def peer_ffn(h, w_q, sub_keys, u, v):
    b, s, d = h.shape
    t = b * s
    x = h.reshape(t, d)
    q = (x @ w_q).reshape(t, PEER_HEADS, 2, PEER_D_HALF)
    scores = jnp.einsum('thcd,cnd->thcn', q, sub_keys).astype(jnp.float32)
    top_s, top_i = lax.top_k(scores, PEER_TOPK)
    cand_s = top_s[..., 0, :, None] + top_s[..., 1, None, :]
    cand_i = top_i[..., 0, :, None] * PEER_N_KEYS + top_i[..., 1, None, :]
    cand_s = cand_s.reshape(t, PEER_HEADS, PEER_TOPK * PEER_TOPK)
    cand_i = cand_i.reshape(t, PEER_HEADS, PEER_TOPK * PEER_TOPK)
    best_s, pos = lax.top_k(cand_s, PEER_TOPK)
    idx = jnp.take_along_axis(cand_i, pos, axis=-1)
    g = jax.nn.softmax(best_s, axis=-1).astype(h.dtype)
    n_sel = PEER_HEADS * PEER_TOPK
    n_blk = t // PEER_TOKEN_BLOCK
    xb = x.reshape(n_blk, PEER_TOKEN_BLOCK, d)
    ib = idx.reshape(n_blk, PEER_TOKEN_BLOCK, n_sel)
    gb = g.reshape(n_blk, PEER_TOKEN_BLOCK, n_sel)

    def one_block(args):
        x_blk, i_blk, g_blk = args
        u_sel = u[i_blk]
        a = jax.nn.gelu(jnp.einsum('tkd,td->tk', u_sel, x_blk), approximate=False) * g_blk
        return jnp.einsum('tk,tkd->td', a, v[i_blk])

    out = lax.map(one_block, (xb, ib, gb))
    return out.reshape(b, s, d)


def setup_inputs(seed: int = 0) -> dict:
    key = jax.random.key(seed)
    ks = jax.random.split(key, 24)
    f32 = jnp.float32
    n_a = (DEPTH + N_MIXERS - 1) // N_MIXERS
    n_b = DEPTH // N_MIXERS

    def nrm(k, shape, scale):
        return jax.random.normal(k, shape, f32) * scale

    def gain(k, shape):
        return 1.0 + 0.02 * jax.random.normal(k, shape, f32)

    x = jax.random.normal(ks[0], (BATCH, SEQ, D_MODEL), f32)
    start = jax.random.randint(ks[1], (BATCH, 1), 0, 8 * SEQ, dtype=jnp.int32)
    positions = (start + jnp.arange(SEQ, dtype=jnp.int32)[None, :]).astype(jnp.int32)
    fox_b_f = jnp.linspace(1.0, 6.0, FOX_HEADS, dtype=f32)[None, :] + 0.1 * jax.random.normal(ks[15], (n_b, FOX_HEADS), f32)
    return {
        'x': x,
        'positions': positions,
        'norm_mix_g': gain(ks[2], (DEPTH, D_MODEL)),
        'norm_ffn_g': gain(ks[3], (DEPTH, D_MODEL)),
        'mla_w_in': nrm(ks[4], (n_a, D_MODEL, MLA_IN), D_MODEL ** -0.5),
        'mla_g_qa': gain(ks[5], (n_a, MLA_Q_RANK)),
        'mla_w_qb': nrm(ks[6], (n_a, MLA_Q_RANK, MLA_HEADS * MLA_QK), MLA_Q_RANK ** -0.5),
        'mla_g_kva': gain(ks[7], (n_a, MLA_KV_RANK)),
        'mla_w_kvb': nrm(ks[8], (n_a, MLA_KV_RANK, MLA_HEADS * (MLA_NOPE + MLA_V)), MLA_KV_RANK ** -0.5),
        'mla_g_q': gain(ks[9], (n_a, MLA_QK)),
        'mla_g_k': gain(ks[10], (n_a, MLA_QK)),
        'mla_w_o': nrm(ks[11], (n_a, MLA_HEADS * MLA_V, D_MODEL), (MLA_HEADS * MLA_V) ** -0.5),
        'fox_w_in': nrm(ks[12], (n_b, D_MODEL, FOX_IN), D_MODEL ** -0.5),
        'fox_b_f': fox_b_f,
        'fox_g_q': gain(ks[13], (n_b, FOX_HEAD_DIM)),
        'fox_g_k': gain(ks[14], (n_b, FOX_HEAD_DIM)),
        'fox_w_o': nrm(ks[16], (n_b, D_MODEL, D_MODEL), D_MODEL ** -0.5),
        'peer_w_q': nrm(ks[17], (DEPTH, D_MODEL, PEER_HEADS * PEER_D_KEY), D_MODEL ** -0.5),
        'peer_sub_keys': nrm(ks[18], (DEPTH, 2, PEER_N_KEYS, PEER_D_HALF), PEER_D_HALF ** -0.5),
        'peer_u': nrm(ks[19], (DEPTH, PEER_N_EXPERTS, D_MODEL), D_MODEL ** -0.5),
        'peer_v': nrm(ks[20], (DEPTH, PEER_N_EXPERTS, D_MODEL), PEER_HEADS ** -0.5),
    }


def reference(x, positions, norm_mix_g, norm_ffn_g, mla_w_in, mla_g_qa, mla_w_qb, mla_g_kva, mla_w_kvb, mla_g_q, mla_g_k, mla_w_o, fox_w_in, fox_b_f, fox_g_q, fox_g_k, fox_w_o, peer_w_q, peer_sub_keys, peer_u, peer_v):
    inv_freq = ROPE_THETA ** (-jnp.arange(0, MLA_ROPE, 2, dtype=jnp.float32) / MLA_ROPE)
    ang = positions.astype(jnp.float32)[..., None] * inv_freq
    cos = jnp.cos(ang)[:, :, None, :].astype(x.dtype)
    sin = jnp.sin(ang)[:, :, None, :].astype(x.dtype)
    for i in range(DEPTH):
        j = i // N_MIXERS
        h = rms_norm(x, norm_mix_g[i])
        if i % N_MIXERS == 0:
            y = mla_mixer(h, cos, sin, mla_w_in[j], mla_g_qa[j], mla_w_qb[j], mla_g_kva[j], mla_w_kvb[j], mla_g_q[j], mla_g_k[j], mla_w_o[j])
        else:
            y = fox_mixer(h, fox_w_in[j], fox_b_f[j], fox_g_q[j], fox_g_k[j], fox_w_o[j])
        x = x + y
        h = rms_norm(x, norm_ffn_g[i])
        x = x + peer_ffn(h, peer_w_q[i], peer_sub_keys[i], peer_u[i], peer_v[i])
    return x
```

```python
import functools
import math

import jax
import jax.numpy as jnp
from jax import lax
from jax.experimental import pallas as pl
from jax.experimental.pallas import tpu as pltpu

F32 = jnp.float32
BF16 = jnp.bfloat16

RMS_EPS = 1e-6
NEG_INF = -1e30
CHUNK_SHIFT = 6
ROPE_THETA = 10000.0

MLA_HEADS = 16
MLA_Q_RANK = 512
MLA_KV_RANK = 512
MLA_NOPE = 128
MLA_ROPE = 64
MLA_V = 128
MLA_QK = MLA_NOPE + MLA_ROPE

FOX_HEADS = 16
FOX_HEAD_DIM = 128

PEER_HEADS = 8
PEER_N_KEYS = 128
PEER_D_HALF = 128
PEER_TOPK = 16

LANES = 128
VMEM_LIMIT = 56 * 1024 * 1024


def _cparams(sem):
    return pltpu.CompilerParams(dimension_semantics=sem, vmem_limit_bytes=VMEM_LIMIT)


def _nm_body(*refs, norm, has_res):
    it = iter(refs)
    a_ref = next(it)
    g_ref = next(it) if norm else None
    w_ref = next(it)
    r_ref = next(it) if has_res else None
    o_ref = next(it)
    an_ref = next(it) if norm else None

    if norm:
        @pl.when(pl.program_id(1) == 0)
        def _():
            a = a_ref[...].astype(F32)
            y = a * lax.rsqrt(jnp.mean(a * a, axis=-1, keepdims=True) + RMS_EPS)
            an_ref[...] = (y * g_ref[...]).astype(BF16)

        a_bf = an_ref[...]
    else:
        a_bf = a_ref[...]
    acc = jnp.dot(a_bf, w_ref[...], preferred_element_type=F32)
    if has_res:
        acc = acc + r_ref[...]
    o_ref[...] = acc.astype(o_ref.dtype)


def norm_matmul(a, w, *, gain=None, residual=None, a_col_block=0, out_dtype=F32, tm=512, tn=512, name):
    m = a.shape[0]
    k, n = w.shape
    tn = min(tn, n)
    assert m % tm == 0 and n % tn == 0
    norm = gain is not None
    in_specs = [pl.BlockSpec((tm, k), lambda i, j: (i, a_col_block))]
    args = [a]
    if norm:
        in_specs.append(pl.BlockSpec((1, k), lambda i, j: (0, 0)))
        args.append(gain.reshape(1, k).astype(F32))
    in_specs.append(pl.BlockSpec((k, tn), lambda i, j: (0, j)))
    args.append(w)
    if residual is not None:
        in_specs.append(pl.BlockSpec((tm, tn), lambda i, j: (i, j)))
        args.append(residual)
    return pl.pallas_call(
        functools.partial(_nm_body, norm=norm, has_res=residual is not None),
        grid=(m // tm, n // tn),
        in_specs=in_specs,
        out_specs=pl.BlockSpec((tm, tn), lambda i, j: (i, j)),
        out_shape=jax.ShapeDtypeStruct((m, n), out_dtype),
        scratch_shapes=[pltpu.VMEM((tm, k), BF16)] if norm else [],
        compiler_params=_cparams(("parallel", "arbitrary")),
        name=name,
    )(*args)


def _swap_halves(x, first_half):
    return jnp.where(first_half, pltpu.roll(x, 96, 1), pltpu.roll(x, 32, 1))


def _mla_prep_body(q_ref, kv_ref, kr_ref, pos_ref, invf_ref, sign_ref, gqn_ref, gqr_ref, gkn_ref, gkr_ref,
                   qf_ref, kf_ref, vf_ref):
    ts = q_ref.shape[0]
    lane = lax.broadcasted_iota(jnp.int32, (ts, LANES), 1)
    first_half = (lane & 63) < 32
    low64 = lane < 64
    ang = pos_ref[...] * invf_ref[...]
    cosv = jnp.cos(ang)
    sinv = jnp.sin(ang) * sign_ref[...]

    def rot(x):
        return x * cosv + _swap_halves(x, first_half) * sinv

    inv_d = 1.0 / MLA_QK
    zeros = jnp.zeros((ts, LANES), F32)

    kr = kr_ref[...]
    ss_kr = jnp.sum(kr * kr, axis=-1, keepdims=True)
    kr_rot = rot(kr * gkr_ref[...])
    for h in range(MLA_HEADS):
        kn = kv_ref[:, h * 256:h * 256 + 128]
        r = lax.rsqrt((jnp.sum(kn * kn, axis=-1, keepdims=True) + ss_kr) * inv_d + RMS_EPS)
        kf_ref[0, h, :, 0:128] = (kn * r * gkn_ref[...]).astype(BF16)
        kf_ref[0, h, :, 128:256] = (kr_rot * r).astype(BF16)
        vf_ref[0, h] = kv_ref[:, h * 256 + 128:(h + 1) * 256].astype(BF16)

    for p in range(MLA_HEADS // 2):
        xr = q_ref[:, 2048 + p * 128:2048 + (p + 1) * 128]
        xr2 = xr * xr
        ss_e = jnp.sum(jnp.where(low64, xr2, 0.0), axis=-1, keepdims=True)
        ss_o = jnp.sum(jnp.where(low64, 0.0, xr2), axis=-1, keepdims=True)
        qn_e = q_ref[:, (2 * p) * 128:(2 * p + 1) * 128]
        qn_o = q_ref[:, (2 * p + 1) * 128:(2 * p + 2) * 128]
        r_e = lax.rsqrt((jnp.sum(qn_e * qn_e, axis=-1, keepdims=True) + ss_e) * inv_d + RMS_EPS)
        r_o = lax.rsqrt((jnp.sum(qn_o * qn_o, axis=-1, keepdims=True) + ss_o) * inv_d + RMS_EPS)
        xr_rot = rot(xr * jnp.where(low64, r_e, r_o) * gqr_ref[...])
        qf_ref[0, 2 * p, :, 0:128] = (qn_e * r_e * gqn_ref[...]).astype(BF16)
        qf_ref[0, 2 * p, :, 128:256] = jnp.where(low64, xr_rot, zeros).astype(BF16)
        qf_ref[0, 2 * p + 1, :, 0:128] = (qn_o * r_o * gqn_ref[...]).astype(BF16)
        qf_ref[0, 2 * p + 1, :, 128:256] = jnp.where(low64, pltpu.roll(xr_rot, 64, 1), zeros).astype(BF16)


def mla_prep(qraw, kvraw, z, pos_col, rows, b, s, ts=256):
    ns = s // ts
    h = MLA_HEADS
    row = lambda bi, si: (bi * ns + si, 0)
    vec = pl.BlockSpec((1, LANES), lambda bi, si: (0, 0))
    head_out = lambda d: pl.BlockSpec((1, h, ts, d), lambda bi, si: (bi, 0, si, 0))
    return pl.pallas_call(
        _mla_prep_body,
        grid=(b, ns),
        in_specs=[
            pl.BlockSpec((ts, qraw.shape[1]), row),
            pl.BlockSpec((ts, kvraw.shape[1]), row),
            pl.BlockSpec((ts, LANES), lambda bi, si: (bi * ns + si, (MLA_Q_RANK + MLA_KV_RANK) // LANES)),
            pl.BlockSpec((ts, 1), row),
        ] + [vec] * 6,
        out_specs=[head_out(256), head_out(256), head_out(128)],
        out_shape=[
            jax.ShapeDtypeStruct((b, h, s, 256), BF16),
            jax.ShapeDtypeStruct((b, h, s, 256), BF16),
            jax.ShapeDtypeStruct((b, h, s, 128), BF16),
        ],
        compiler_params=_cparams(("parallel", "parallel")),
        name="mla_prep",
    )(qraw, kvraw, z, pos_col, *rows)


def _split3(x):
    hi = x.astype(BF16)
    r1 = x - hi.astype(F32)
    mid = r1.astype(BF16)
    lo = (r1 - mid.astype(F32)).astype(BF16)
    return hi, mid, lo


def _fox_prep_body(z_ref, bf_ref, gq_ref, gk_ref, qf_ref, kf_ref, vf_ref, ct_ref, carry_ref):
    ts = z_ref.shape[0]
    d = FOX_HEAD_DIM
    nh = FOX_HEADS

    @pl.when(pl.program_id(1) == 0)
    def _():
        carry_ref[...] = jnp.zeros_like(carry_ref)

    x = z_ref[:, 4 * nh * d:4 * nh * d + LANES] + bf_ref[...]
    logf = jnp.minimum(x, 0.0) - jnp.log1p(jnp.exp(-jnp.abs(x)))
    tri = (lax.broadcasted_iota(jnp.int32, (ts, ts), 0) >= lax.broadcasted_iota(jnp.int32, (ts, ts), 1)).astype(BF16)
    hi, mid, lo = _split3(logf)
    local = (jnp.dot(tri, hi, preferred_element_type=F32) + jnp.dot(tri, mid, preferred_element_type=F32)
             + jnp.dot(tri, lo, preferred_element_type=F32))
    c = carry_ref[0:1, :] + local
    carry_ref[0:1, :] = c[ts - 1:ts, :]
    ct_ref[0] = c.T

    inv_d = 1.0 / d
    for h in range(nh):
        q = z_ref[:, h * d:(h + 1) * d]
        k = z_ref[:, (nh + h) * d:(nh + h + 1) * d]
        rq = lax.rsqrt(jnp.sum(q * q, axis=-1, keepdims=True) * inv_d + RMS_EPS)
        rk = lax.rsqrt(jnp.sum(k * k, axis=-1, keepdims=True) * inv_d + RMS_EPS)
        qf_ref[0, h] = (q * rq * gq_ref[...]).astype(BF16)
        kf_ref[0, h] = (k * rk * gk_ref[...]).astype(BF16)
        vf_ref[0, h] = z_ref[:, (2 * nh + h) * d:(2 * nh + h + 1) * d].astype(BF16)


def fox_prep(zmain, rows, b, s, ts=256):
    ns = s // ts
    h = FOX_HEADS
    row = lambda bi, si: (bi * ns + si, 0)
    vec = pl.BlockSpec((1, LANES), lambda bi, si: (0, 0))
    head_out = pl.BlockSpec((1, h, ts, 128), lambda bi, si: (bi, 0, si, 0))
    return pl.pallas_call(
        _fox_prep_body,
        grid=(b, ns),
        in_specs=[pl.BlockSpec((ts, zmain.shape[1]), row), vec, vec, vec],
        out_specs=[head_out, head_out, head_out, pl.BlockSpec((1, LANES, ts), lambda bi, si: (bi, 0, si))],
        out_shape=[jax.ShapeDtypeStruct((b, h, s, 128), BF16)] * 3 + [jax.ShapeDtypeStruct((b, LANES, s), F32)],
        scratch_shapes=[pltpu.VMEM((8, LANES), F32)],
        compiler_params=_cparams(("parallel", "arbitrary")),
        name="fox_prep",
    )(zmain, *rows)


def _attn_body(*refs, fox, tq):
    if fox:
        q_ref, k_ref, v_ref, ccol_ref, crow_ref, gate_ref, o_ref = refs
    else:
        q_ref, k_ref, v_ref, o_ref = refs
    qi = pl.program_id(2)
    q = q_ref[0, 0]
    dv = v_ref.shape[-1]

    def step(kj, carry, diagonal):
        m, l, acc = carry
        k = k_ref[0, 0, kj]
        v = v_ref[0, 0, kj]
        s = lax.dot_general(q, k, (((1,), (1,)), ((), ())), preferred_element_type=F32)
        if fox:
            s = s + (ccol_ref[0, 0] - crow_ref[0, 0, kj])
        if diagonal:
            r = lax.broadcasted_iota(jnp.int32, (tq, tq), 0)
            c = lax.broadcasted_iota(jnp.int32, (tq, tq), 1)
            allowed = (c <= r) if fox else ((c >> CHUNK_SHIFT) <= (r >> CHUNK_SHIFT))
            s = jnp.where(allowed, s, NEG_INF)
        m_new = jnp.maximum(m, jnp.max(s, axis=-1, keepdims=True))
        alpha = jnp.exp(m - m_new)
        p = jnp.exp(s - m_new)
        l = alpha * l + jnp.sum(p, axis=-1, keepdims=True)
        acc = alpha * acc + jnp.dot(p.astype(BF16), v, preferred_element_type=F32)
        return m_new, l, acc

    init = (jnp.full((tq, 1), NEG_INF, F32), jnp.zeros((tq, 1), F32), jnp.zeros((tq, dv), F32))
    carry = lax.fori_loop(0, qi, lambda kj, cr: step(kj, cr, False), init)
    _, l, acc = step(qi, carry, True)
    o = acc / l
    if fox:
        o = o * (1.0 / (1.0 + jnp.exp(-gate_ref[...])))
    o_ref[...] = o.astype(o_ref.dtype)


def attention(qf, kf, vf, *, fox, ccol=None, crow=None, gate=None, gate_col0=0, tq=256):
    b, h, s, dk = qf.shape
    dv = vf.shape[-1]
    nq = s // tq
    kf5 = kf.reshape(b, h, nq, tq, dk)
    vf5 = vf.reshape(b, h, nq, tq, dv)
    in_specs = [
        pl.BlockSpec((1, 1, tq, dk), lambda bi, hi, qi: (bi, hi, qi, 0)),
        pl.BlockSpec((1, 1, nq, tq, dk), lambda bi, hi, qi: (bi, hi, 0, 0, 0)),
        pl.BlockSpec((1, 1, nq, tq, dv), lambda bi, hi, qi: (bi, hi, 0, 0, 0)),
    ]
    args = [qf, kf5, vf5]
    if fox:
        in_specs += [
            pl.BlockSpec((1, 1, tq, 1), lambda bi, hi, qi: (bi, hi, qi, 0)),
            pl.BlockSpec((1, 1, nq, 1, tq), lambda bi, hi, qi: (bi, hi, 0, 0, 0)),
            pl.BlockSpec((tq, dv), lambda bi, hi, qi: (bi * nq + qi, gate_col0 + hi)),
        ]
        args += [ccol, crow.reshape(b, h, nq, 1, tq), gate]
    return pl.pallas_call(
        functools.partial(_attn_body, fox=fox, tq=tq),
        grid=(b, h, nq),
        in_specs=in_specs,
        out_specs=pl.BlockSpec((tq, dv), lambda bi, hi, qi: (bi * nq + qi, hi)),
        out_shape=jax.ShapeDtypeStruct((b * s, h * dv), BF16),
        compiler_params=_cparams(("parallel", "parallel", "arbitrary")),
        name="fox_attn" if fox else "mla_attn",
    )(*args)


_CAND_PAIRS = [(p, q) for p in range(PEER_TOPK) for q in range(PEER_TOPK) if (p + 1) * (q + 1) <= PEER_TOPK]
_CAND_ROWS = 56


def _top_rows(v, n):
    nrows = v.shape[0]
    rows = lax.broadcasted_iota(jnp.int32, v.shape, 0)
    out = []
    for r in range(n):
        m = jnp.max(v, axis=0, keepdims=True)
        out.append(m)
        if r + 1 < n:
            first = jnp.min(jnp.where(v == m, rows, nrows), axis=0, keepdims=True)
            v = jnp.where(rows == first, -jnp.inf, v)
    return out


def _peer_prep_body(x_ref, g_ref, wq_ref, sk_ref, xn_ref, s1_ref, s2_ref, e2_ref, c1_ref, tau_ref, cand_ref):
    @pl.when(pl.program_id(1) == 0)
    def _():
        a = x_ref[...]
        y = a * lax.rsqrt(jnp.mean(a * a, axis=-1, keepdims=True) + RMS_EPS)
        xn_ref[...] = (y * g_ref[...]).astype(BF16)

    nt = (((1,), (1,)), ((), ()))
    qt = lax.dot_general(wq_ref[...], xn_ref[...], nt, preferred_element_type=F32)
    s1 = jnp.dot(sk_ref[0], qt[0:128].astype(BF16), preferred_element_type=F32)
    s2 = jnp.dot(sk_ref[1], qt[128:256].astype(BF16), preferred_element_type=F32)
    a1 = _top_rows(s1, PEER_TOPK)
    a2 = _top_rows(s2, PEER_TOPK)
    cand_ref[...] = jnp.full(cand_ref.shape, -jnp.inf, F32)
    for r, (p, q) in enumerate(_CAND_PAIRS):
        cand_ref[r:r + 1, :] = a1[p] + a2[q]
    best = _top_rows(cand_ref[...], PEER_TOPK)
    z = jnp.ones_like(best[0])
    for r in range(1, PEER_TOPK):
        z = z + jnp.exp(best[r] - best[0])
    s1_ref[0] = s1
    s2_ref[0] = s2
    e2_ref[0] = jnp.exp(s2 - a2[0])
    c1_ref[0] = jnp.exp(s1 - a1[0]) / z
    tau_ref[0] = best[PEER_TOPK - 1]


def peer_prep(x, gain, wq_t, sk, tm=512):
    t, d = x.shape
    nh = PEER_HEADS
    stat = pl.BlockSpec((1, PEER_N_KEYS, tm), lambda i, h: (h, 0, i))
    stat_shape = jax.ShapeDtypeStruct((nh, PEER_N_KEYS, t), F32)
    return pl.pallas_call(
        _peer_prep_body,
        grid=(t // tm, nh),
        in_specs=[
            pl.BlockSpec((tm, d), lambda i, h: (i, 0)),
            pl.BlockSpec((1, d), lambda i, h: (0, 0)),
            pl.BlockSpec((2 * PEER_D_HALF, d), lambda i, h: (h, 0)),
            pl.BlockSpec((2, PEER_N_KEYS, PEER_D_HALF), lambda i, h: (0, 0, 0)),
        ],
        out_specs=[pl.BlockSpec((tm, d), lambda i, h: (i, 0)), stat, stat, stat, stat,
                   pl.BlockSpec((1, 1, tm), lambda i, h: (h, 0, i))],
        out_shape=[jax.ShapeDtypeStruct((t, d), BF16), stat_shape, stat_shape, stat_shape, stat_shape,
                   jax.ShapeDtypeStruct((nh, 1, t), F32)],
        scratch_shapes=[pltpu.VMEM((_CAND_ROWS, tm), F32)],
        compiler_params=_cparams(("parallel", "arbitrary")),
        name="peer_prep",
    )(x, gain.reshape(1, d).astype(F32), wq_t, sk)


def _peer_main_body(xn_ref, u_ref, vt_ref, s1_ref, s2_ref, e2_ref, c1_ref, tau_ref, res_ref, o_ref, acc_ref, a_ref):
    e = pl.program_id(1)
    te = u_ref.shape[0]
    n_i = te // PEER_N_KEYS

    @pl.when(e == 0)
    def _():
        acc_ref[...] = jnp.zeros_like(acc_ref)

    nt = (((1,), (1,)), ((), ()))
    ht = lax.dot_general(u_ref[...], xn_ref[...], nt, preferred_element_type=F32)
    for ii in range(n_i):
        i = e * n_i + ii
        g = None
        for h in range(PEER_HEADS):
            sums = s2_ref[h] + s1_ref[h, pl.ds(i, 1), :]
            w = jnp.where(sums >= tau_ref[h], e2_ref[h] * c1_ref[h, pl.ds(i, 1), :], 0.0)
            g = w if g is None else g + w
        hh = ht[ii * PEER_N_KEYS:(ii + 1) * PEER_N_KEYS]
        act = 0.5 * hh * (1.0 + lax.erf(hh * (1.0 / math.sqrt(2.0))))
        a_ref[ii * PEER_N_KEYS:(ii + 1) * PEER_N_KEYS, :] = (act * g).astype(BF16)
    acc_ref[...] += jnp.dot(vt_ref[...], a_ref[...], preferred_element_type=F32)

    @pl.when(e == pl.num_programs(1) - 1)
    def _():
        o_ref[...] = res_ref[...] + acc_ref[...].T


def peer_main(xn, u, vt, s1, s2, e2, c1, tau, res, tm=512, te=512):
    t, d = xn.shape
    n_e = u.shape[0]
    nh = PEER_HEADS
    stat = pl.BlockSpec((nh, PEER_N_KEYS, tm), lambda i, e: (0, 0, i))
    return pl.pallas_call(
        _peer_main_body,
        grid=(t // tm, n_e // te),
        in_specs=[
            pl.BlockSpec((tm, d), lambda i, e: (i, 0)),
            pl.BlockSpec((te, d), lambda i, e: (e, 0)),
            pl.BlockSpec((d, te), lambda i, e: (0, e)),
            stat, stat, stat, stat,
            pl.BlockSpec((nh, 1, tm), lambda i, e: (0, 0, i)),
            pl.BlockSpec((tm, d), lambda i, e: (i, 0)),
        ],
        out_specs=pl.BlockSpec((tm, d), lambda i, e: (i, 0)),
        out_shape=jax.ShapeDtypeStruct((t, d), F32),
        scratch_shapes=[pltpu.VMEM((d, tm), F32), pltpu.VMEM((te, tm), BF16)],
        compiler_params=_cparams(("parallel", "arbitrary")),
        name="peer_main",
    )(xn, u, vt, s1, s2, e2, c1, tau, res)


def peer_ffn(x, gain, w_q, sub_keys, u, v):
    xn, s1, s2, e2, c1, tau = peer_prep(x, gain, w_q.T.astype(BF16), sub_keys.astype(BF16))
    return peer_main(xn, u.astype(BF16), v.T.astype(BF16), s1, s2, e2, c1, tau, x)


def _row128(v):
    return v.reshape(1, LANES).astype(F32)


def mla_mixer(x, gain, positions, w_in, g_qa, w_qb, g_kva, w_kvb, g_q, g_k, w_o, b, s):
    t = b * s
    nh = MLA_HEADS
    w_in_p = jnp.pad(w_in, ((0, 0), (0, 64))).astype(BF16)
    z = norm_matmul(x, w_in_p, gain=gain, tn=384, name="mla_in")
    wq = w_qb.reshape(MLA_Q_RANK, nh, MLA_QK)
    wq = jnp.concatenate([wq[:, :, :MLA_NOPE].reshape(MLA_Q_RANK, -1), wq[:, :, MLA_NOPE:].reshape(MLA_Q_RANK, -1)], 1)
    qraw = norm_matmul(z, wq.astype(BF16), gain=g_qa, a_col_block=0, name="mla_qb")
    kvraw = norm_matmul(z, w_kvb.astype(BF16), gain=g_kva, a_col_block=1, name="mla_kvb")

    scale = 1.0 / math.sqrt(MLA_QK)
    inv_freq = ROPE_THETA ** (-jnp.arange(0, MLA_ROPE, 2, dtype=F32) / MLA_ROPE)
    sign = jnp.where((jnp.arange(LANES) % 64) < 32, -1.0, 1.0)
    rows = [
        _row128(jnp.tile(inv_freq, 4)),
        _row128(sign),
        _row128(g_q[:MLA_NOPE] * scale),
        _row128(jnp.tile(g_q[MLA_NOPE:], 2) * scale),
        _row128(g_k[:MLA_NOPE]),
        _row128(jnp.pad(g_k[MLA_NOPE:], (0, 64))),
    ]
    pos_col = positions.astype(F32).reshape(t, 1)
    qf, kf, vf = mla_prep(qraw, kvraw, z, pos_col, rows, b, s)
    o = attention(qf, kf, vf, fox=False)
    return norm_matmul(o, w_o.astype(BF16), residual=x, name="mla_out")


def fox_mixer(x, gain, w_in, b_f, g_q, g_k, w_o, b, s):
    nh = FOX_HEADS
    d = x.shape[1]
    w_in_p = jnp.pad(w_in, ((0, 0), (0, LANES - nh))).astype(BF16)
    zmain = norm_matmul(x, w_in_p, gain=gain, tn=640, name="fox_in")
    scale = 1.0 / math.sqrt(FOX_HEAD_DIM)
    rows = [_row128(jnp.pad(b_f, (0, LANES - nh))), _row128(g_q * scale), _row128(g_k)]
    qf, kf, vf, ct = fox_prep(zmain, rows, b, s)
    crow = ct[:, :nh, :]
    o = attention(qf, kf, vf, fox=True, ccol=crow[..., None], crow=crow, gate=zmain, gate_col0=3 * nh)
    return norm_matmul(o, w_o.astype(BF16), residual=x, name="fox_out")


def kernel(x, positions, norm_mix_g, norm_ffn_g, mla_w_in, mla_g_qa, mla_w_qb, mla_g_kva, mla_w_kvb, mla_g_q, mla_g_k, mla_w_o, fox_w_in, fox_b_f, fox_g_q, fox_g_k, fox_w_o, peer_w_q, peer_sub_keys, peer_u, peer_v):
    b, s, d = x.shape
    depth = norm_mix_g.shape[0]
    xt = x.reshape(b * s, d)
    for i in range(depth):
        j = i // 2
        if i % 2 == 0:
            xt = mla_mixer(xt, norm_mix_g[i], positions, mla_w_in[j], mla_g_qa[j], mla_w_qb[j], mla_g_kva[j],
                           mla_w_kvb[j], mla_g_q[j], mla_g_k[j], mla_w_o[j], b, s)
        else:
            xt = fox_mixer(xt, norm_mix_g[i], fox_w_in[j], fox_b_f[j], fox_g_q[j], fox_g_k[j], fox_w_o[j], b, s)
        xt = peer_ffn(xt, norm_ffn_g[i], peer_w_q[i], peer_sub_keys[i], peer_u[i], peer_v[i])
    return xt.reshape(b, s, d)
```

```python
import functools
import math

import jax
import jax.numpy as jnp
from jax import lax
from jax.experimental import pallas as pl
from jax.experimental.pallas import tpu as pltpu

F32 = jnp.float32
BF16 = jnp.bfloat16

RMS_EPS = 1e-6
NEG_INF = -1e30
CHUNK_SHIFT = 6
ROPE_THETA = 10000.0

MLA_HEADS = 16
MLA_Q_RANK = 512
MLA_KV_RANK = 512
MLA_NOPE = 128
MLA_ROPE = 64
MLA_V = 128
MLA_QK = MLA_NOPE + MLA_ROPE

FOX_HEADS = 16
FOX_HEAD_DIM = 128

PEER_HEADS = 8
PEER_N_KEYS = 128
PEER_D_HALF = 128
PEER_TOPK = 16

LANES = 128
VMEM_LIMIT = 56 * 1024 * 1024


def _cparams(sem):
    return pltpu.CompilerParams(dimension_semantics=sem, vmem_limit_bytes=VMEM_LIMIT)


def _nm_body(*refs, norm, has_res):
    it = iter(refs)
    a_ref = next(it)
    g_ref = next(it) if norm else None
    w_ref = next(it)
    r_ref = next(it) if has_res else None
    o_ref = next(it)
    an_ref = next(it) if norm else None

    if norm:
        @pl.when(pl.program_id(1) == 0)
        def _():
            a = a_ref[...].astype(F32)
            y = a * lax.rsqrt(jnp.mean(a * a, axis=-1, keepdims=True) + RMS_EPS)
            an_ref[...] = (y * g_ref[...]).astype(BF16)

        a_bf = an_ref[...]
    else:
        a_bf = a_ref[...]
    acc = jnp.dot(a_bf, w_ref[...], preferred_element_type=F32)
    if has_res:
        acc = acc + r_ref[...]
    o_ref[...] = acc.astype(o_ref.dtype)


def norm_matmul(a, w, *, gain=None, residual=None, a_col_block=0, out_dtype=F32, tm=512, tn=512, name):
    m = a.shape[0]
    k, n = w.shape
    tn = min(tn, n)
    assert m % tm == 0 and n % tn == 0
    norm = gain is not None
    in_specs = [pl.BlockSpec((tm, k), lambda i, j: (i, a_col_block))]
    args = [a]
    if norm:
        in_specs.append(pl.BlockSpec((1, k), lambda i, j: (0, 0)))
        args.append(gain.reshape(1, k).astype(F32))
    in_specs.append(pl.BlockSpec((k, tn), lambda i, j: (0, j)))
    args.append(w)
    if residual is not None:
        in_specs.append(pl.BlockSpec((tm, tn), lambda i, j: (i, j)))
        args.append(residual)
    return pl.pallas_call(
        functools.partial(_nm_body, norm=norm, has_res=residual is not None),
        grid=(m // tm, n // tn),
        in_specs=in_specs,
        out_specs=pl.BlockSpec((tm, tn), lambda i, j: (i, j)),
        out_shape=jax.ShapeDtypeStruct((m, n), out_dtype),
        scratch_shapes=[pltpu.VMEM((tm, k), BF16)] if norm else [],
        compiler_params=_cparams(("parallel", "arbitrary")),
        name=name,
    )(*args)


def _swap_halves(x, first_half):
    return jnp.where(first_half, pltpu.roll(x, 96, 1), pltpu.roll(x, 32, 1))


def _mla_prep_body(q_ref, kv_ref, kr_ref, pos_ref, invf_ref, sign_ref, gqn_ref, gqr_ref, gkn_ref, gkr_ref,
                   qf_ref, kf_ref, vf_ref):
    ts = q_ref.shape[0]
    lane = lax.broadcasted_iota(jnp.int32, (ts, LANES), 1)
    first_half = (lane & 63) < 32
    low64 = lane < 64
    ang = pos_ref[...] * invf_ref[...]
    cosv = jnp.cos(ang)
    sinv = jnp.sin(ang) * sign_ref[...]

    def rot(x):
        return x * cosv + _swap_halves(x, first_half) * sinv

    inv_d = 1.0 / MLA_QK
    zeros = jnp.zeros((ts, LANES), F32)

    kr = kr_ref[...]
    ss_kr = jnp.sum(kr * kr, axis=-1, keepdims=True)
    kr_rot = rot(kr * gkr_ref[...])
    for h in range(MLA_HEADS):
        kn = kv_ref[:, h * 256:h * 256 + 128]
        r = lax.rsqrt((jnp.sum(kn * kn, axis=-1, keepdims=True) + ss_kr) * inv_d + RMS_EPS)
        kf_ref[0, h, :, 0:128] = (kn * r * gkn_ref[...]).astype(BF16)
        kf_ref[0, h, :, 128:256] = (kr_rot * r).astype(BF16)
        vf_ref[0, h] = kv_ref[:, h * 256 + 128:(h + 1) * 256].astype(BF16)

    for p in range(MLA_HEADS // 2):
        xr = q_ref[:, 2048 + p * 128:2048 + (p + 1) * 128]
        xr2 = xr * xr
        ss_e = jnp.sum(jnp.where(low64, xr2, 0.0), axis=-1, keepdims=True)
        ss_o = jnp.sum(jnp.where(low64, 0.0, xr2), axis=-1, keepdims=True)
        qn_e = q_ref[:, (2 * p) * 128:(2 * p + 1) * 128]
        qn_o = q_ref[:, (2 * p + 1) * 128:(2 * p + 2) * 128]
        r_e = lax.rsqrt((jnp.sum(qn_e * qn_e, axis=-1, keepdims=True) + ss_e) * inv_d + RMS_EPS)
        r_o = lax.rsqrt((jnp.sum(qn_o * qn_o, axis=-1, keepdims=True) + ss_o) * inv_d + RMS_EPS)
        xr_rot = rot(xr * jnp.where(low64, r_e, r_o) * gqr_ref[...])
        qf_ref[0, 2 * p, :, 0:128] = (qn_e * r_e * gqn_ref[...]).astype(BF16)
        qf_ref[0, 2 * p, :, 128:256] = jnp.where(low64, xr_rot, zeros).astype(BF16)
        qf_ref[0, 2 * p + 1, :, 0:128] = (qn_o * r_o * gqn_ref[...]).astype(BF16)
        qf_ref[0, 2 * p + 1, :, 128:256] = jnp.where(low64, pltpu.roll(xr_rot, 64, 1), zeros).astype(BF16)


def mla_prep(qraw, kvraw, z, pos_col, rows, b, s, ts=256):
    ns = s // ts
    h = MLA_HEADS
    row = lambda bi, si: (bi * ns + si, 0)
    vec = pl.BlockSpec((1, LANES), lambda bi, si: (0, 0))
    head_out = lambda d: pl.BlockSpec((1, h, ts, d), lambda bi, si: (bi, 0, si, 0))
    return pl.pallas_call(
        _mla_prep_body,
        grid=(b, ns),
        in_specs=[
            pl.BlockSpec((ts, qraw.shape[1]), row),
            pl.BlockSpec((ts, kvraw.shape[1]), row),
            pl.BlockSpec((ts, LANES), lambda bi, si: (bi * ns + si, (MLA_Q_RANK + MLA_KV_RANK) // LANES)),
            pl.BlockSpec((ts, 1), row),
        ] + [vec] * 6,
        out_specs=[head_out(256), head_out(256), head_out(128)],
        out_shape=[
            jax.ShapeDtypeStruct((b, h, s, 256), BF16),
            jax.ShapeDtypeStruct((b, h, s, 256), BF16),
            jax.ShapeDtypeStruct((b, h, s, 128), BF16),
        ],
        compiler_params=_cparams(("parallel", "parallel")),
        name="mla_prep",
    )(qraw, kvraw, z, pos_col, *rows)


def _split3(x):
    hi = x.astype(BF16)
    r1 = x - hi.astype(F32)
    mid = r1.astype(BF16)
    lo = (r1 - mid.astype(F32)).astype(BF16)
    return hi, mid, lo


def _fox_prep_body(z_ref, bf_ref, gq_ref, gk_ref, qf_ref, kf_ref, vf_ref, ct_ref, carry_ref):
    ts = z_ref.shape[0]
    d = FOX_HEAD_DIM
    nh = FOX_HEADS

    @pl.when(pl.program_id(1) == 0)
    def _():
        carry_ref[...] = jnp.zeros_like(carry_ref)

    x = z_ref[:, 4 * nh * d:4 * nh * d + LANES] + bf_ref[...]
    logf = jnp.minimum(x, 0.0) - jnp.log1p(jnp.exp(-jnp.abs(x)))
    tri = (lax.broadcasted_iota(jnp.int32, (ts, ts), 0) >= lax.broadcasted_iota(jnp.int32, (ts, ts), 1)).astype(BF16)
    hi, mid, lo = _split3(logf)
    local = (jnp.dot(tri, hi, preferred_element_type=F32) + jnp.dot(tri, mid, preferred_element_type=F32)
             + jnp.dot(tri, lo, preferred_element_type=F32))
    c = carry_ref[0:1, :] + local
    carry_ref[0:1, :] = c[ts - 1:ts, :]
    ct_ref[0] = c.T

    inv_d = 1.0 / d
    for h in range(nh):
        q = z_ref[:, h * d:(h + 1) * d]
        k = z_ref[:, (nh + h) * d:(nh + h + 1) * d]
        rq = lax.rsqrt(jnp.sum(q * q, axis=-1, keepdims=True) * inv_d + RMS_EPS)
        rk = lax.rsqrt(jnp.sum(k * k, axis=-1, keepdims=True) * inv_d + RMS_EPS)
        qf_ref[0, h] = (q * rq * gq_ref[...]).astype(BF16)
        kf_ref[0, h] = (k * rk * gk_ref[...]).astype(BF16)
        vf_ref[0, h] = z_ref[:, (2 * nh + h) * d:(2 * nh + h + 1) * d].astype(BF16)


def fox_prep(zmain, rows, b, s, ts=256):
    ns = s // ts
    h = FOX_HEADS
    row = lambda bi, si: (bi * ns + si, 0)
    vec = pl.BlockSpec((1, LANES), lambda bi, si: (0, 0))
    head_out = pl.BlockSpec((1, h, ts, 128), lambda bi, si: (bi, 0, si, 0))
    return pl.pallas_call(
        _fox_prep_body,
        grid=(b, ns),
        in_specs=[pl.BlockSpec((ts, zmain.shape[1]), row), vec, vec, vec],
        out_specs=[head_out, head_out, head_out, pl.BlockSpec((1, LANES, ts), lambda bi, si: (bi, 0, si))],
        out_shape=[jax.ShapeDtypeStruct((b, h, s, 128), BF16)] * 3 + [jax.ShapeDtypeStruct((b, LANES, s), F32)],
        scratch_shapes=[pltpu.VMEM((8, LANES), F32)],
        compiler_params=_cparams(("parallel", "arbitrary")),
        name="fox_prep",
    )(zmain, *rows)


def _attn_body(*refs, fox, tq, nq):
    if fox:
        q_ref, k_ref, v_ref, ccol_ref, crow_ref, gate_ref, o_ref, s_ref, p_ref = refs
    else:
        q_ref, k_ref, v_ref, o_ref, s_ref, p_ref = refs
    nt = (((1,), (1,)), ((), ()))
    half = tq // 2
    r = lax.broadcasted_iota(jnp.int32, (tq, tq), 0)
    c = lax.broadcasted_iota(jnp.int32, (tq, tq), 1)
    allowed = (c <= r) if fox else ((c >> CHUNK_SHIFT) <= (r >> CHUNK_SHIFT))
    for i in range(nq):
        buf = i % 2
        rows = slice(i * tq, (i + 1) * tq)
        q = q_ref[0, 0, rows, :]
        mx = jnp.full((tq, half), NEG_INF, F32)
        for j in range(i + 1):
            cols = slice(j * tq, (j + 1) * tq)
            s = lax.dot_general(q, k_ref[0, 0, cols, :], nt, preferred_element_type=F32)
            if fox:
                s = s + (ccol_ref[0, 0, rows, :] - crow_ref[0, 0, :, cols])
            if j == i:
                s = jnp.where(allowed, s, NEG_INF)
            s_ref[buf, :, cols] = s
            mx = jnp.maximum(mx, jnp.maximum(s[:, :half], s[:, half:]))
        m = jnp.max(mx, axis=-1, keepdims=True)
        ps = jnp.zeros((tq, half), F32)
        for j in range(i + 1):
            cols = slice(j * tq, (j + 1) * tq)
            p = jnp.exp(s_ref[buf, :, cols] - m)
            ps = ps + (p[:, :half] + p[:, half:])
            p_ref[buf, :, cols] = p.astype(BF16)
        l = jnp.sum(ps, axis=-1, keepdims=True)
        n_keys = (i + 1) * tq
        o = jnp.dot(p_ref[buf, :, :n_keys], v_ref[0, 0, :n_keys, :], preferred_element_type=F32) / l
        if fox:
            o = o * (1.0 / (1.0 + jnp.exp(-gate_ref[rows, :])))
        o_ref[rows, :] = o.astype(o_ref.dtype)


def attention(qf, kf, vf, *, fox, ccol=None, crow=None, gate=None, gate_col0=0, tq=256):
    b, h, s, dk = qf.shape
    dv = vf.shape[-1]
    nq = s // tq
    head = lambda d: pl.BlockSpec((1, 1, s, d), lambda bi, hi: (bi, hi, 0, 0))
    in_specs = [head(dk), head(dk), head(dv)]
    args = [qf, kf, vf]
    if fox:
        in_specs += [
            head(1),
            pl.BlockSpec((1, 1, 1, s), lambda bi, hi: (bi, hi, 0, 0)),
            pl.BlockSpec((s, dv), lambda bi, hi: (bi, gate_col0 + hi)),
        ]
        args += [ccol, crow.reshape(b, h, 1, s), gate]
    return pl.pallas_call(
        functools.partial(_attn_body, fox=fox, tq=tq, nq=nq),
        grid=(b, h),
        in_specs=in_specs,
        out_specs=pl.BlockSpec((s, dv), lambda bi, hi: (bi, hi)),
        out_shape=jax.ShapeDtypeStruct((b * s, h * dv), BF16),
        scratch_shapes=[pltpu.VMEM((2, tq, s), F32), pltpu.VMEM((2, tq, s), BF16)],
        compiler_params=_cparams(("parallel", "parallel")),
        name="fox_attn" if fox else "mla_attn",
    )(*args)


_CAND_PAIRS = [(p, q) for p in range(PEER_TOPK) for q in range(PEER_TOPK) if (p + 1) * (q + 1) <= PEER_TOPK]
_CAND_ROWS = 56


def _top_rows(v, n, want_rank=False):
    nrows = v.shape[0]
    rows = lax.broadcasted_iota(jnp.int32, v.shape, 0)
    rank = jnp.full(v.shape, float(n), F32)
    out = []
    for r in range(n):
        m = jnp.max(v, axis=0, keepdims=True)
        out.append(m)
        if r + 1 < n or want_rank:
            first = rows == jnp.min(jnp.where(v == m, rows, nrows), axis=0, keepdims=True)
            if want_rank:
                rank = jnp.where(first, float(r), rank)
            if r + 1 < n:
                v = jnp.where(first, -jnp.inf, v)
    return (out, rank) if want_rank else out


def _peer_prep_body(x_ref, g_ref, wq_ref, sk_ref, xn_ref, rk2_ref, e2_ref, n1_ref, c1_ref, cand_ref):
    @pl.when(pl.program_id(1) == 0)
    def _():
        a = x_ref[...]
        y = a * lax.rsqrt(jnp.mean(a * a, axis=-1, keepdims=True) + RMS_EPS)
        xn_ref[...] = (y * g_ref[...]).astype(BF16)

    nt = (((1,), (1,)), ((), ()))
    qt = lax.dot_general(wq_ref[...], xn_ref[...], nt, preferred_element_type=F32)
    s1 = jnp.dot(sk_ref[0], qt[0:128].astype(BF16), preferred_element_type=F32)
    s2 = jnp.dot(sk_ref[1], qt[128:256].astype(BF16), preferred_element_type=F32)
    a1, rank1 = _top_rows(s1, PEER_TOPK, want_rank=True)
    a2, rank2 = _top_rows(s2, PEER_TOPK, want_rank=True)
    cand_ref[...] = jnp.full(cand_ref.shape, -jnp.inf, F32)
    for r, (p, q) in enumerate(_CAND_PAIRS):
        cand_ref[r:r + 1, :] = a1[p] + a2[q]
    best = _top_rows(cand_ref[...], PEER_TOPK)
    tau = best[PEER_TOPK - 1]
    z = jnp.ones_like(tau)
    for r in range(1, PEER_TOPK):
        z = z + jnp.exp(best[r] - best[0])
    n1 = jnp.zeros(s1.shape, F32)
    for p in range(PEER_TOPK):
        n_p = jnp.zeros_like(tau)
        for q in range(PEER_TOPK // (p + 1)):
            n_p = n_p + jnp.where(a1[p] + a2[q] >= tau, 1.0, 0.0)
        n1 = jnp.where(rank1 == float(p), n_p, n1)
    rk2_ref[0] = rank2.astype(BF16)
    e2_ref[0] = jnp.exp(s2 - a2[0]).astype(BF16)
    n1_ref[0] = n1
    c1_ref[0] = jnp.exp(s1 - a1[0]) / z


def peer_prep(x, gain, wq_t, sk, tm=512):
    t, d = x.shape
    nh = PEER_HEADS
    stat = pl.BlockSpec((1, PEER_N_KEYS, tm), lambda i, h: (h, 0, i))
    stat_f32 = jax.ShapeDtypeStruct((nh, PEER_N_KEYS, t), F32)
    stat_bf16 = jax.ShapeDtypeStruct((nh, PEER_N_KEYS, t), BF16)
    return pl.pallas_call(
        _peer_prep_body,
        grid=(t // tm, nh),
        in_specs=[
            pl.BlockSpec((tm, d), lambda i, h: (i, 0)),
            pl.BlockSpec((1, d), lambda i, h: (0, 0)),
            pl.BlockSpec((2 * PEER_D_HALF, d), lambda i, h: (h, 0)),
            pl.BlockSpec((2, PEER_N_KEYS, PEER_D_HALF), lambda i, h: (0, 0, 0)),
        ],
        out_specs=[pl.BlockSpec((tm, d), lambda i, h: (i, 0)), stat, stat, stat, stat],
        out_shape=[jax.ShapeDtypeStruct((t, d), BF16), stat_bf16, stat_bf16, stat_f32, stat_f32],
        scratch_shapes=[pltpu.VMEM((_CAND_ROWS, tm), F32)],
        compiler_params=_cparams(("parallel", "arbitrary")),
        name="peer_prep",
    )(x, gain.reshape(1, d).astype(F32), wq_t, sk)


_SLAB = 16


def _peer_main_body(xn_ref, u_ref, vt_ref, rk2_ref, e2_ref, n1_ref, c1_ref, res_ref, o_ref, acc_ref, h_ref, a_ref):
    e = pl.program_id(1)
    te, tm = h_ref.shape
    n_i = te // PEER_N_KEYS
    n_slab = PEER_N_KEYS // _SLAB

    @pl.when(e == 0)
    def _():
        acc_ref[...] = jnp.zeros_like(acc_ref)

    nt = (((1,), (1,)), ((), ()))
    h_ref[...] = lax.dot_general(u_ref[...], xn_ref[...], nt, preferred_element_type=F32)
    zero = jnp.zeros((_SLAB, tm), BF16)
    for ii in range(n_i):
        i = e * n_i + ii
        g = [None] * n_slab
        for h in range(PEER_HEADS):
            nb = jnp.broadcast_to(n1_ref[h, pl.ds(i, 1), :], (_SLAB, tm)).astype(BF16)
            cb = jnp.broadcast_to(c1_ref[h, pl.ds(i, 1), :], (_SLAB, tm)).astype(BF16)
            for sl in range(n_slab):
                rows = slice(sl * _SLAB, (sl + 1) * _SLAB)
                w = jnp.where(rk2_ref[h, rows, :] < nb, e2_ref[h, rows, :] * cb, zero)
                g[sl] = w if g[sl] is None else g[sl] + w
        for sl in range(n_slab):
            rows = slice(ii * PEER_N_KEYS + sl * _SLAB, ii * PEER_N_KEYS + (sl + 1) * _SLAB)
            hh = h_ref[rows, :]
            act = 0.5 * hh * (1.0 + lax.erf(hh * (1.0 / math.sqrt(2.0))))
            a_ref[rows, :] = act.astype(BF16) * g[sl]
    acc_ref[...] += jnp.dot(vt_ref[...], a_ref[...], preferred_element_type=F32)

    @pl.when(e == pl.num_programs(1) - 1)
    def _():
        o_ref[...] = res_ref[...] + acc_ref[...].T


def peer_main(xn, u, vt, rk2, e2, n1, c1, res, tm=512, te=1024):
    t, d = xn.shape
    n_e = u.shape[0]
    nh = PEER_HEADS
    once = pl.Buffered(1)
    stat = pl.BlockSpec((nh, PEER_N_KEYS, tm), lambda i, e: (0, 0, i), pipeline_mode=once)
    return pl.pallas_call(
        _peer_main_body,
        grid=(t // tm, n_e // te),
        in_specs=[
            pl.BlockSpec((tm, d), lambda i, e: (i, 0), pipeline_mode=once),
            pl.BlockSpec((te, d), lambda i, e: (e, 0)),
            pl.BlockSpec((d, te), lambda i, e: (0, e)),
            stat, stat, stat, stat,
            pl.BlockSpec((tm, d), lambda i, e: (i, 0), pipeline_mode=once),
        ],
        out_specs=pl.BlockSpec((tm, d), lambda i, e: (i, 0)),
        out_shape=jax.ShapeDtypeStruct((t, d), F32),
        scratch_shapes=[pltpu.VMEM((d, tm), F32), pltpu.VMEM((te, tm), F32), pltpu.VMEM((te, tm), BF16)],
        compiler_params=_cparams(("parallel", "arbitrary")),
        name="peer_main",
    )(xn, u, vt, rk2, e2, n1, c1, res)


def peer_ffn(x, gain, w_q, sub_keys, u, v):
    xn, rk2, e2, n1, c1 = peer_prep(x, gain, w_q.T.astype(BF16), sub_keys.astype(BF16))
    return peer_main(xn, u.astype(BF16), v.T.astype(BF16), rk2, e2, n1, c1, x)


def _row128(v):
    return v.reshape(1, LANES).astype(F32)


def mla_mixer(x, gain, positions, w_in, g_qa, w_qb, g_kva, w_kvb, g_q, g_k, w_o, b, s):
    t = b * s
    nh = MLA_HEADS
    w_in_p = jnp.pad(w_in, ((0, 0), (0, 64))).astype(BF16)
    z = norm_matmul(x, w_in_p, gain=gain, tn=384, name="mla_in")
    wq = w_qb.reshape(MLA_Q_RANK, nh, MLA_QK)
    wq = jnp.concatenate([wq[:, :, :MLA_NOPE].reshape(MLA_Q_RANK, -1), wq[:, :, MLA_NOPE:].reshape(MLA_Q_RANK, -1)], 1)
    qraw = norm_matmul(z, wq.astype(BF16), gain=g_qa, a_col_block=0, name="mla_qb")
    kvraw = norm_matmul(z, w_kvb.astype(BF16), gain=g_kva, a_col_block=1, name="mla_kvb")

    scale = 1.0 / math.sqrt(MLA_QK)
    inv_freq = ROPE_THETA ** (-jnp.arange(0, MLA_ROPE, 2, dtype=F32) / MLA_ROPE)
    sign = jnp.where((jnp.arange(LANES) % 64) < 32, -1.0, 1.0)
    rows = [
        _row128(jnp.tile(inv_freq, 4)),
        _row128(sign),
        _row128(g_q[:MLA_NOPE] * scale),
        _row128(jnp.tile(g_q[MLA_NOPE:], 2) * scale),
        _row128(g_k[:MLA_NOPE]),
        _row128(jnp.pad(g_k[MLA_NOPE:], (0, 64))),
    ]
    pos_col = positions.astype(F32).reshape(t, 1)
    qf, kf, vf = mla_prep(qraw, kvraw, z, pos_col, rows, b, s)
    o = attention(qf, kf, vf, fox=False)
    return norm_matmul(o, w_o.astype(BF16), residual=x, name="mla_out")


def fox_mixer(x, gain, w_in, b_f, g_q, g_k, w_o, b, s):
    nh = FOX_HEADS
    d = x.shape[1]
    w_in_p = jnp.pad(w_in, ((0, 0), (0, LANES - nh))).astype(BF16)
    zmain = norm_matmul(x, w_in_p, gain=gain, tn=640, name="fox_in")
    scale = 1.0 / math.sqrt(FOX_HEAD_DIM)
    rows = [_row128(jnp.pad(b_f, (0, LANES - nh))), _row128(g_q * scale), _row128(g_k)]
    qf, kf, vf, ct = fox_prep(zmain, rows, b, s)
    crow = ct[:, :nh, :]
    o = attention(qf, kf, vf, fox=True, ccol=crow[..., None], crow=crow, gate=zmain, gate_col0=3 * nh)
    return norm_matmul(o, w_o.astype(BF16), residual=x, name="fox_out")


def kernel(x, positions, norm_mix_g, norm_ffn_g, mla_w_in, mla_g_qa, mla_w_qb, mla_g_kva, mla_w_kvb, mla_g_q, mla_g_k, mla_w_o, fox_w_in, fox_b_f, fox_g_q, fox_g_k, fox_w_o, peer_w_q, peer_sub_keys, peer_u, peer_v):
    b, s, d = x.shape
    depth = norm_mix_g.shape[0]
    xt = x.reshape(b * s, d)
    for i in range(depth):
        j = i // 2
        if i % 2 == 0:
            xt = mla_mixer(xt, norm_mix_g[i], positions, mla_w_in[j], mla_g_qa[j], mla_w_qb[j], mla_g_kva[j],
                           mla_w_kvb[j], mla_g_q[j], mla_g_k[j], mla_w_o[j], b, s)
        else:
            xt = fox_mixer(xt, norm_mix_g[i], fox_w_in[j], fox_b_f[j], fox_g_q[j], fox_g_k[j], fox_w_o[j], b, s)
        xt = peer_ffn(xt, norm_ffn_g[i], peer_w_q[i], peer_sub_keys[i], peer_u[i], peer_v[i])
    return xt.reshape(b, s, d)
```

```python
import functools
import math

import jax
import jax.numpy as jnp
from jax import lax
from jax.experimental import pallas as pl
from jax.experimental.pallas import tpu as pltpu

F32 = jnp.float32
BF16 = jnp.bfloat16

RMS_EPS = 1e-6
NEG_INF = -1e30
CHUNK_SHIFT = 6
ROPE_THETA = 10000.0

MLA_HEADS = 16
MLA_Q_RANK = 512
MLA_KV_RANK = 512
MLA_NOPE = 128
MLA_ROPE = 64
MLA_V = 128
MLA_QK = MLA_NOPE + MLA_ROPE

FOX_HEADS = 16
FOX_HEAD_DIM = 128

PEER_HEADS = 8
PEER_N_KEYS = 128
PEER_D_HALF = 128
PEER_TOPK = 16

LANES = 128
VMEM_LIMIT = 56 * 1024 * 1024


def _cparams(sem, flags=None):
    return pltpu.CompilerParams(dimension_semantics=sem, vmem_limit_bytes=VMEM_LIMIT, flags=flags)


def _nm_body(*refs, norm, has_res):
    it = iter(refs)
    a_ref = next(it)
    g_ref = next(it) if norm else None
    w_ref = next(it)
    r_ref = next(it) if has_res else None
    o_ref = next(it)
    an_ref = next(it) if norm else None

    if norm:
        @pl.when(pl.program_id(1) == 0)
        def _():
            a = a_ref[...].astype(F32)
            y = a * lax.rsqrt(jnp.mean(a * a, axis=-1, keepdims=True) + RMS_EPS)
            an_ref[...] = (y * g_ref[...]).astype(BF16)

        a_bf = an_ref[...]
    else:
        a_bf = a_ref[...]
    acc = jnp.dot(a_bf, w_ref[...], preferred_element_type=F32)
    if has_res:
        acc = acc + r_ref[...]
    o_ref[...] = acc.astype(o_ref.dtype)


def norm_matmul(a, w, *, gain=None, residual=None, a_col_block=0, out_dtype=F32, tm=512, tn=512, name):
    m = a.shape[0]
    k, n = w.shape
    tn = min(tn, n)
    assert m % tm == 0 and n % tn == 0
    norm = gain is not None
    in_specs = [pl.BlockSpec((tm, k), lambda i, j: (i, a_col_block))]
    args = [a]
    if norm:
        in_specs.append(pl.BlockSpec((1, k), lambda i, j: (0, 0)))
        args.append(gain.reshape(1, k).astype(F32))
    in_specs.append(pl.BlockSpec((k, tn), lambda i, j: (0, j)))
    args.append(w)
    if residual is not None:
        in_specs.append(pl.BlockSpec((tm, tn), lambda i, j: (i, j)))
        args.append(residual)
    return pl.pallas_call(
        functools.partial(_nm_body, norm=norm, has_res=residual is not None),
        grid=(m // tm, n // tn),
        in_specs=in_specs,
        out_specs=pl.BlockSpec((tm, tn), lambda i, j: (i, j)),
        out_shape=jax.ShapeDtypeStruct((m, n), out_dtype),
        scratch_shapes=[pltpu.VMEM((tm, k), BF16)] if norm else [],
        compiler_params=_cparams(("parallel", "arbitrary")),
        name=name,
    )(*args)


def _swap_halves(x, first_half):
    return jnp.where(first_half, pltpu.roll(x, 96, 1), pltpu.roll(x, 32, 1))


def _mla_prep_body(q_ref, kv_ref, kr_ref, pos_ref, invf_ref, sign_ref, gqn_ref, gqr_ref, gkn_ref, gkr_ref,
                   qf_ref, kf_ref, vf_ref):
    ts = q_ref.shape[0]
    lane = lax.broadcasted_iota(jnp.int32, (ts, LANES), 1)
    first_half = (lane & 63) < 32
    low64 = lane < 64
    ang = pos_ref[...] * invf_ref[...]
    cosv = jnp.cos(ang)
    sinv = jnp.sin(ang) * sign_ref[...]

    def rot(x):
        return x * cosv + _swap_halves(x, first_half) * sinv

    inv_d = 1.0 / MLA_QK
    zeros = jnp.zeros((ts, LANES), F32)

    kr = kr_ref[...]
    ss_kr = jnp.sum(kr * kr, axis=-1, keepdims=True)
    kr_rot = rot(kr * gkr_ref[...])
    for h in range(MLA_HEADS):
        kn = kv_ref[:, h * 256:h * 256 + 128]
        r = lax.rsqrt((jnp.sum(kn * kn, axis=-1, keepdims=True) + ss_kr) * inv_d + RMS_EPS)
        kf_ref[0, h, :, 0:128] = (kn * r * gkn_ref[...]).astype(BF16)
        kf_ref[0, h, :, 128:256] = (kr_rot * r).astype(BF16)
        vf_ref[0, h] = kv_ref[:, h * 256 + 128:(h + 1) * 256].astype(BF16)

    for p in range(MLA_HEADS // 2):
        xr = q_ref[:, 2048 + p * 128:2048 + (p + 1) * 128]
        xr2 = xr * xr
        ss_e = jnp.sum(jnp.where(low64, xr2, 0.0), axis=-1, keepdims=True)
        ss_o = jnp.sum(jnp.where(low64, 0.0, xr2), axis=-1, keepdims=True)
        qn_e = q_ref[:, (2 * p) * 128:(2 * p + 1) * 128]
        qn_o = q_ref[:, (2 * p + 1) * 128:(2 * p + 2) * 128]
        r_e = lax.rsqrt((jnp.sum(qn_e * qn_e, axis=-1, keepdims=True) + ss_e) * inv_d + RMS_EPS)
        r_o = lax.rsqrt((jnp.sum(qn_o * qn_o, axis=-1, keepdims=True) + ss_o) * inv_d + RMS_EPS)
        xr_rot = rot(xr * jnp.where(low64, r_e, r_o) * gqr_ref[...])
        qf_ref[0, 2 * p, :, 0:128] = (qn_e * r_e * gqn_ref[...]).astype(BF16)
        qf_ref[0, 2 * p, :, 128:256] = jnp.where(low64, xr_rot, zeros).astype(BF16)
        qf_ref[0, 2 * p + 1, :, 0:128] = (qn_o * r_o * gqn_ref[...]).astype(BF16)
        qf_ref[0, 2 * p + 1, :, 128:256] = jnp.where(low64, pltpu.roll(xr_rot, 64, 1), zeros).astype(BF16)


def mla_prep(qraw, kvraw, z, pos_col, rows, b, s, ts=256):
    ns = s // ts
    h = MLA_HEADS
    row = lambda bi, si: (bi * ns + si, 0)
    vec = pl.BlockSpec((1, LANES), lambda bi, si: (0, 0))
    head_out = lambda d: pl.BlockSpec((1, h, ts, d), lambda bi, si: (bi, 0, si, 0))
    return pl.pallas_call(
        _mla_prep_body,
        grid=(b, ns),
        in_specs=[
            pl.BlockSpec((ts, qraw.shape[1]), row),
            pl.BlockSpec((ts, kvraw.shape[1]), row),
            pl.BlockSpec((ts, LANES), lambda bi, si: (bi * ns + si, (MLA_Q_RANK + MLA_KV_RANK) // LANES)),
            pl.BlockSpec((ts, 1), row),
        ] + [vec] * 6,
        out_specs=[head_out(256), head_out(256), head_out(128)],
        out_shape=[
            jax.ShapeDtypeStruct((b, h, s, 256), BF16),
            jax.ShapeDtypeStruct((b, h, s, 256), BF16),
            jax.ShapeDtypeStruct((b, h, s, 128), BF16),
        ],
        compiler_params=_cparams(("parallel", "parallel")),
        name="mla_prep",
    )(qraw, kvraw, z, pos_col, *rows)


def _split3(x):
    hi = x.astype(BF16)
    r1 = x - hi.astype(F32)
    mid = r1.astype(BF16)
    lo = (r1 - mid.astype(F32)).astype(BF16)
    return hi, mid, lo


def _fox_prep_body(z_ref, bf_ref, gq_ref, gk_ref, qf_ref, kf_ref, vf_ref, ct_ref, carry_ref):
    ts = z_ref.shape[0]
    d = FOX_HEAD_DIM
    nh = FOX_HEADS

    @pl.when(pl.program_id(1) == 0)
    def _():
        carry_ref[...] = jnp.zeros_like(carry_ref)

    x = z_ref[:, 4 * nh * d:4 * nh * d + LANES] + bf_ref[...]
    logf = jnp.minimum(x, 0.0) - jnp.log1p(jnp.exp(-jnp.abs(x)))
    tri = (lax.broadcasted_iota(jnp.int32, (ts, ts), 0) >= lax.broadcasted_iota(jnp.int32, (ts, ts), 1)).astype(BF16)
    hi, mid, lo = _split3(logf)
    local = (jnp.dot(tri, hi, preferred_element_type=F32) + jnp.dot(tri, mid, preferred_element_type=F32)
             + jnp.dot(tri, lo, preferred_element_type=F32))
    c = carry_ref[0:1, :] + local
    carry_ref[0:1, :] = c[ts - 1:ts, :]
    ct_ref[0] = c.T

    inv_d = 1.0 / d
    for h in range(nh):
        q = z_ref[:, h * d:(h + 1) * d]
        k = z_ref[:, (nh + h) * d:(nh + h + 1) * d]
        rq = lax.rsqrt(jnp.sum(q * q, axis=-1, keepdims=True) * inv_d + RMS_EPS)
        rk = lax.rsqrt(jnp.sum(k * k, axis=-1, keepdims=True) * inv_d + RMS_EPS)
        qf_ref[0, h] = (q * rq * gq_ref[...]).astype(BF16)
        kf_ref[0, h] = (k * rk * gk_ref[...]).astype(BF16)
        vf_ref[0, h] = z_ref[:, (2 * nh + h) * d:(2 * nh + h + 1) * d].astype(BF16)


def fox_prep(zmain, rows, b, s, ts=256):
    ns = s // ts
    h = FOX_HEADS
    row = lambda bi, si: (bi * ns + si, 0)
    vec = pl.BlockSpec((1, LANES), lambda bi, si: (0, 0))
    head_out = pl.BlockSpec((1, h, ts, 128), lambda bi, si: (bi, 0, si, 0))
    return pl.pallas_call(
        _fox_prep_body,
        grid=(b, ns),
        in_specs=[pl.BlockSpec((ts, zmain.shape[1]), row), vec, vec, vec],
        out_specs=[head_out, head_out, head_out, pl.BlockSpec((1, LANES, ts), lambda bi, si: (bi, 0, si))],
        out_shape=[jax.ShapeDtypeStruct((b, h, s, 128), BF16)] * 3 + [jax.ShapeDtypeStruct((b, LANES, s), F32)],
        scratch_shapes=[pltpu.VMEM((8, LANES), F32)],
        compiler_params=_cparams(("parallel", "arbitrary")),
        name="fox_prep",
    )(zmain, *rows)


def _attn_body(*refs, fox, tq, nq):
    if fox:
        q_ref, k_ref, v_ref, ccol_ref, crow_ref, gate_ref, o_ref, s_ref, p_ref = refs
    else:
        q_ref, k_ref, v_ref, o_ref, s_ref, p_ref = refs
    nt = (((1,), (1,)), ((), ()))
    half = tq // 2
    r = lax.broadcasted_iota(jnp.int32, (tq, tq), 0)
    c = lax.broadcasted_iota(jnp.int32, (tq, tq), 1)
    allowed = (c <= r) if fox else ((c >> CHUNK_SHIFT) <= (r >> CHUNK_SHIFT))
    for i in range(nq):
        buf = i % 2
        rows = slice(i * tq, (i + 1) * tq)
        q = q_ref[0, 0, rows, :]
        mx = jnp.full((tq, half), NEG_INF, F32)
        for j in range(i + 1):
            cols = slice(j * tq, (j + 1) * tq)
            s = lax.dot_general(q, k_ref[0, 0, cols, :], nt, preferred_element_type=F32)
            if fox:
                s = s + (ccol_ref[0, 0, rows, :] - crow_ref[0, 0, :, cols])
            if j == i:
                s = jnp.where(allowed, s, NEG_INF)
            s_ref[buf, :, cols] = s
            mx = jnp.maximum(mx, jnp.maximum(s[:, :half], s[:, half:]))
        m = jnp.max(mx, axis=-1, keepdims=True)
        ps = jnp.zeros((tq, half), F32)
        for j in range(i + 1):
            cols = slice(j * tq, (j + 1) * tq)
            p = jnp.exp(s_ref[buf, :, cols] - m)
            ps = ps + (p[:, :half] + p[:, half:])
            p_ref[buf, :, cols] = p.astype(BF16)
        l = jnp.sum(ps, axis=-1, keepdims=True)
        n_keys = (i + 1) * tq
        o = jnp.dot(p_ref[buf, :, :n_keys], v_ref[0, 0, :n_keys, :], preferred_element_type=F32) / l
        if fox:
            o = o * (1.0 / (1.0 + jnp.exp(-gate_ref[rows, :])))
        o_ref[rows, :] = o.astype(o_ref.dtype)


def attention(qf, kf, vf, *, fox, ccol=None, crow=None, gate=None, gate_col0=0, tq=256):
    b, h, s, dk = qf.shape
    dv = vf.shape[-1]
    nq = s // tq
    head = lambda d: pl.BlockSpec((1, 1, s, d), lambda bi, hi: (bi, hi, 0, 0))
    in_specs = [head(dk), head(dk), head(dv)]
    args = [qf, kf, vf]
    if fox:
        in_specs += [
            head(1),
            pl.BlockSpec((1, 1, 1, s), lambda bi, hi: (bi, hi, 0, 0)),
            pl.BlockSpec((s, dv), lambda bi, hi: (bi, gate_col0 + hi)),
        ]
        args += [ccol, crow.reshape(b, h, 1, s), gate]
    return pl.pallas_call(
        functools.partial(_attn_body, fox=fox, tq=tq, nq=nq),
        grid=(b, h),
        in_specs=in_specs,
        out_specs=pl.BlockSpec((s, dv), lambda bi, hi: (bi, hi)),
        out_shape=jax.ShapeDtypeStruct((b * s, h * dv), BF16),
        scratch_shapes=[pltpu.VMEM((2, tq, s), F32), pltpu.VMEM((2, tq, s), BF16)],
        compiler_params=_cparams(("parallel", "parallel")),
        name="fox_attn" if fox else "mla_attn",
    )(*args)


_CAND_PAIRS = [(p, q) for p in range(PEER_TOPK) for q in range(PEER_TOPK) if (p + 1) * (q + 1) <= PEER_TOPK]
_CAND_ROWS = 56


def _top_rows(v, n, want_rank=False):
    rank = jnp.full(v.shape, float(n), F32)
    out = []
    for r in range(n):
        m = jnp.max(v, axis=0, keepdims=True)
        out.append(m)
        hit = v == m
        if want_rank:
            rank = jnp.where(hit, float(r), rank)
        if r + 1 < n:
            v = jnp.where(hit, -jnp.inf, v)
    return (out, rank) if want_rank else out


def _peer_prep_body(x_ref, g_ref, wq_ref, sk_ref, xn_ref, rk2_ref, e2_ref, n1_ref, c1_ref, s_ref, cand_ref):
    @pl.when(pl.program_id(1) == 0)
    def _():
        a = x_ref[...]
        y = a * lax.rsqrt(jnp.mean(a * a, axis=-1, keepdims=True) + RMS_EPS)
        xn_ref[...] = (y * g_ref[...]).T.astype(BF16)

    qt = jnp.dot(wq_ref[...], xn_ref[...], preferred_element_type=F32)
    s_ref[0] = jnp.dot(sk_ref[0], qt[0:128].astype(BF16), preferred_element_type=F32)
    s_ref[1] = jnp.dot(sk_ref[1], qt[128:256].astype(BF16), preferred_element_type=F32)
    cand_ref[...] = jnp.full(cand_ref.shape, -jnp.inf, F32)

    def lane_group(c, carry):
        lanes = pl.ds(pl.multiple_of(c * LANES, LANES), LANES)
        s1 = s_ref[0, :, lanes]
        s2 = s_ref[1, :, lanes]
        a1, rank1 = _top_rows(s1, PEER_TOPK, want_rank=True)
        a2, rank2 = _top_rows(s2, PEER_TOPK, want_rank=True)
        for r, (p, q) in enumerate(_CAND_PAIRS):
            cand_ref[r:r + 1, :] = a1[p] + a2[q]
        best = _top_rows(cand_ref[...], PEER_TOPK)
        tau = best[PEER_TOPK - 1]
        z = jnp.ones_like(tau)
        for r in range(1, PEER_TOPK):
            z = z + jnp.exp(best[r] - best[0])
        n1 = jnp.zeros(s1.shape, F32)
        for p in range(PEER_TOPK):
            n_p = jnp.zeros_like(tau)
            for q in range(PEER_TOPK // (p + 1)):
                n_p = n_p + jnp.where(a1[p] + a2[q] >= tau, 1.0, 0.0)
            n1 = jnp.where(rank1 == float(p), n_p, n1)
        rk2_ref[0, :, lanes] = rank2.astype(BF16)
        e2_ref[0, :, lanes] = jnp.exp(s2 - a2[0]).astype(BF16)
        n1_ref[0, :, lanes] = n1
        c1_ref[0, :, lanes] = jnp.exp(s1 - a1[0]) / z
        return carry

    lax.fori_loop(0, s_ref.shape[-1] // LANES, lane_group, 0, unroll=2)


def peer_prep(x, gain, wq_t, sk, tm=512):
    t, d = x.shape
    nh = PEER_HEADS
    stat = pl.BlockSpec((1, PEER_N_KEYS, tm), lambda i, h: (h, 0, i))
    stat_f32 = jax.ShapeDtypeStruct((nh, PEER_N_KEYS, t), F32)
    stat_bf16 = jax.ShapeDtypeStruct((nh, PEER_N_KEYS, t), BF16)
    return pl.pallas_call(
        _peer_prep_body,
        grid=(t // tm, nh),
        in_specs=[
            pl.BlockSpec((tm, d), lambda i, h: (i, 0)),
            pl.BlockSpec((1, d), lambda i, h: (0, 0)),
            pl.BlockSpec((2 * PEER_D_HALF, d), lambda i, h: (h, 0)),
            pl.BlockSpec((2, PEER_N_KEYS, PEER_D_HALF), lambda i, h: (0, 0, 0)),
        ],
        out_specs=[pl.BlockSpec((d, tm), lambda i, h: (0, i)), stat, stat, stat, stat],
        out_shape=[jax.ShapeDtypeStruct((d, t), BF16), stat_bf16, stat_bf16, stat_f32, stat_f32],
        scratch_shapes=[pltpu.VMEM((2, PEER_N_KEYS, tm), F32), pltpu.VMEM((_CAND_ROWS, LANES), F32)],
        compiler_params=_cparams(("parallel", "arbitrary")),
        name="peer_prep",
    )(x, gain.reshape(1, d).astype(F32), wq_t, sk)


_SLAB = 16
_GATE_LANES = 256


def _peer_main_body(xn_ref, u_ref, vta_ref, vtb_ref, rk2_ref, e2_ref, n1_ref, c1_ref, res_ref, o_ref,
                    acc_ref, h_ref, a_ref):
    k = pl.program_id(1)
    nk = pl.num_programs(1) - 1
    te, tm = h_ref.shape[1:]
    n_i = te // PEER_N_KEYS
    n_slab = PEER_N_KEYS // _SLAB
    zero = jnp.zeros((_SLAB, _GATE_LANES), BF16)

    @pl.when((pl.program_id(0) == 0) & (k == 0))
    def _():
        h_ref[...] = jnp.zeros_like(h_ref)

    @pl.when(k == 0)
    def _():
        acc_ref[...] = jnp.zeros_like(acc_ref)

    def gate_stage(slot, blk):
        valid = (blk >= 0) & (blk < 2 * nk)
        for lc in range(tm // _GATE_LANES):
            lanes = slice(lc * _GATE_LANES, (lc + 1) * _GATE_LANES)
            for ii in range(n_i):
                i = jnp.clip(blk * n_i + ii, 0, PEER_N_KEYS - 1)
                g = [None] * n_slab
                for h in range(PEER_HEADS):
                    n_row = jnp.where(valid, n1_ref[h, pl.ds(i, 1), lanes], 0.0)
                    nb = jnp.broadcast_to(n_row, (_SLAB, _GATE_LANES)).astype(BF16)
                    cb = jnp.broadcast_to(c1_ref[h, pl.ds(i, 1), lanes], (_SLAB, _GATE_LANES)).astype(BF16)
                    for sl in range(n_slab):
                        rows = slice(sl * _SLAB, (sl + 1) * _SLAB)
                        w = jnp.where(rk2_ref[h, rows, lanes] < nb, e2_ref[h, rows, lanes] * cb, zero)
                        g[sl] = w if g[sl] is None else g[sl] + w
                for sl in range(n_slab):
                    rows = slice(ii * PEER_N_KEYS + sl * _SLAB, ii * PEER_N_KEYS + (sl + 1) * _SLAB)
                    hh = h_ref[slot, rows, lanes]
                    act = 0.5 * hh * (1.0 + lax.erf(hh * (1.0 / math.sqrt(2.0))))
                    a_ref[slot, rows, lanes] = act.astype(BF16) * g[sl]

    gate_stage(1, 2 * k - 1)
    h_ref[0] = jnp.dot(u_ref[0:te, :], xn_ref[...], preferred_element_type=F32)
    acc_ref[...] += jnp.dot(vta_ref[...], a_ref[1], preferred_element_type=F32)
    gate_stage(0, 2 * k)
    h_ref[1] = jnp.dot(u_ref[te:2 * te, :], xn_ref[...], preferred_element_type=F32)
    acc_ref[...] += jnp.dot(vtb_ref[...], a_ref[0], preferred_element_type=F32)

    @pl.when(k == nk)
    def _():
        o_ref[...] = res_ref[...] + acc_ref[...].T


def peer_main(xn_t, u, vt, rk2, e2, n1, c1, res, tm=512, te=512):
    d, t = xn_t.shape
    nk = u.shape[0] // (2 * te)
    nh = PEER_HEADS
    once = pl.Buffered(1)
    stat = pl.BlockSpec((nh, PEER_N_KEYS, tm), lambda i, k: (0, 0, i), pipeline_mode=once)
    return pl.pallas_call(
        _peer_main_body,
        grid=(t // tm, nk + 1),
        in_specs=[
            pl.BlockSpec((d, tm), lambda i, k: (0, i), pipeline_mode=once),
            pl.BlockSpec((2 * te, d), lambda i, k: (jnp.minimum(k, nk - 1), 0)),
            pl.BlockSpec((d, te), lambda i, k: (0, jnp.clip(2 * k - 1, 0, 2 * nk - 1))),
            pl.BlockSpec((d, te), lambda i, k: (0, jnp.minimum(2 * k, 2 * nk - 1))),
            stat, stat, stat, stat,
            pl.BlockSpec((tm, d), lambda i, k: (i, 0), pipeline_mode=once),
        ],
        out_specs=pl.BlockSpec((tm, d), lambda i, k: (i, 0)),
        out_shape=jax.ShapeDtypeStruct((t, d), F32),
        scratch_shapes=[pltpu.VMEM((d, tm), F32), pltpu.VMEM((2, te, tm), F32), pltpu.VMEM((2, te, tm), BF16)],
        compiler_params=_cparams(("parallel", "arbitrary")),
        name="peer_main",
    )(xn_t, u, vt, vt, rk2, e2, n1, c1, res)


def peer_ffn(x, gain, w_q, sub_keys, u, v):
    xn_t, rk2, e2, n1, c1 = peer_prep(x, gain, w_q.T.astype(BF16), sub_keys.astype(BF16))
    return peer_main(xn_t, u.astype(BF16), v.T.astype(BF16), rk2, e2, n1, c1, x)


def _row128(v):
    return v.reshape(1, LANES).astype(F32)


def mla_mixer(x, gain, positions, w_in, g_qa, w_qb, g_kva, w_kvb, g_q, g_k, w_o, b, s):
    t = b * s
    nh = MLA_HEADS
    w_in_p = jnp.pad(w_in, ((0, 0), (0, 64))).astype(BF16)
    z = norm_matmul(x, w_in_p, gain=gain, tn=384, name="mla_in")
    wq = w_qb.reshape(MLA_Q_RANK, nh, MLA_QK)
    wq = jnp.concatenate([wq[:, :, :MLA_NOPE].reshape(MLA_Q_RANK, -1), wq[:, :, MLA_NOPE:].reshape(MLA_Q_RANK, -1)], 1)
    qraw = norm_matmul(z, wq.astype(BF16), gain=g_qa, a_col_block=0, name="mla_qb")
    kvraw = norm_matmul(z, w_kvb.astype(BF16), gain=g_kva, a_col_block=1, name="mla_kvb")

    scale = 1.0 / math.sqrt(MLA_QK)
    inv_freq = ROPE_THETA ** (-jnp.arange(0, MLA_ROPE, 2, dtype=F32) / MLA_ROPE)
    sign = jnp.where((jnp.arange(LANES) % 64) < 32, -1.0, 1.0)
    rows = [
        _row128(jnp.tile(inv_freq, 4)),
        _row128(sign),
        _row128(g_q[:MLA_NOPE] * scale),
        _row128(jnp.tile(g_q[MLA_NOPE:], 2) * scale),
        _row128(g_k[:MLA_NOPE]),
        _row128(jnp.pad(g_k[MLA_NOPE:], (0, 64))),
    ]
    pos_col = positions.astype(F32).reshape(t, 1)
    qf, kf, vf = mla_prep(qraw, kvraw, z, pos_col, rows, b, s)
    o = attention(qf, kf, vf, fox=False)
    return norm_matmul(o, w_o.astype(BF16), residual=x, name="mla_out")


def fox_mixer(x, gain, w_in, b_f, g_q, g_k, w_o, b, s):
    nh = FOX_HEADS
    d = x.shape[1]
    w_in_p = jnp.pad(w_in, ((0, 0), (0, LANES - nh))).astype(BF16)
    zmain = norm_matmul(x, w_in_p, gain=gain, tn=640, name="fox_in")
    scale = 1.0 / math.sqrt(FOX_HEAD_DIM)
    rows = [_row128(jnp.pad(b_f, (0, LANES - nh))), _row128(g_q * scale), _row128(g_k)]
    qf, kf, vf, ct = fox_prep(zmain, rows, b, s)
    crow = ct[:, :nh, :]
    o = attention(qf, kf, vf, fox=True, ccol=crow[..., None], crow=crow, gate=zmain, gate_col0=3 * nh)
    return norm_matmul(o, w_o.astype(BF16), residual=x, name="fox_out")


def kernel(x, positions, norm_mix_g, norm_ffn_g, mla_w_in, mla_g_qa, mla_w_qb, mla_g_kva, mla_w_kvb, mla_g_q, mla_g_k, mla_w_o, fox_w_in, fox_b_f, fox_g_q, fox_g_k, fox_w_o, peer_w_q, peer_sub_keys, peer_u, peer_v):
    b, s, d = x.shape
    depth = norm_mix_g.shape[0]
    xt = x.reshape(b * s, d)
    for i in range(depth):
        j = i // 2
        if i % 2 == 0:
            xt = mla_mixer(xt, norm_mix_g[i], positions, mla_w_in[j], mla_g_qa[j], mla_w_qb[j], mla_g_kva[j],
                           mla_w_kvb[j], mla_g_q[j], mla_g_k[j], mla_w_o[j], b, s)
        else:
            xt = fox_mixer(xt, norm_mix_g[i], fox_w_in[j], fox_b_f[j], fox_g_q[j], fox_g_k[j], fox_w_o[j], b, s)
        xt = peer_ffn(xt, norm_ffn_g[i], peer_w_q[i], peer_sub_keys[i], peer_u[i], peer_v[i])
    return xt.reshape(b, s, d)
```

```python
import functools
import math

import jax
import jax.numpy as jnp
from jax import lax
from jax.experimental import pallas as pl
from jax.experimental.pallas import tpu as pltpu

F32 = jnp.float32
BF16 = jnp.bfloat16

RMS_EPS = 1e-6
NEG_INF = -1e30
CHUNK_SHIFT = 6
ROPE_THETA = 10000.0

MLA_HEADS = 16
MLA_Q_RANK = 512
MLA_KV_RANK = 512
MLA_NOPE = 128
MLA_ROPE = 64
MLA_V = 128
MLA_QK = MLA_NOPE + MLA_ROPE

FOX_HEADS = 16
FOX_HEAD_DIM = 128

PEER_HEADS = 8
PEER_N_KEYS = 128
PEER_D_HALF = 128
PEER_TOPK = 16

LANES = 128
VMEM_LIMIT = 56 * 1024 * 1024


def _cparams(sem, flags=None):
    return pltpu.CompilerParams(dimension_semantics=sem, vmem_limit_bytes=VMEM_LIMIT, flags=flags)


def _nm_body(*refs, norm, has_res):
    it = iter(refs)
    a_ref = next(it)
    g_ref = next(it) if norm else None
    w_ref = next(it)
    r_ref = next(it) if has_res else None
    o_ref = next(it)
    an_ref = next(it) if norm else None

    if norm:
        @pl.when(pl.program_id(1) == 0)
        def _():
            a = a_ref[...].astype(F32)
            y = a * lax.rsqrt(jnp.mean(a * a, axis=-1, keepdims=True) + RMS_EPS)
            an_ref[...] = (y * g_ref[...]).astype(BF16)

        a_bf = an_ref[...]
    else:
        a_bf = a_ref[...]
    acc = jnp.dot(a_bf, w_ref[...], preferred_element_type=F32)
    if has_res:
        acc = acc + r_ref[...]
    o_ref[...] = acc.astype(o_ref.dtype)


def norm_matmul(a, w, *, gain=None, residual=None, a_col_block=0, out_dtype=F32, tm=512, tn=512, name):
    m = a.shape[0]
    k, n = w.shape
    tn = min(tn, n)
    assert m % tm == 0 and n % tn == 0
    norm = gain is not None
    in_specs = [pl.BlockSpec((tm, k), lambda i, j: (i, a_col_block))]
    args = [a]
    if norm:
        in_specs.append(pl.BlockSpec((1, k), lambda i, j: (0, 0)))
        args.append(gain.reshape(1, k).astype(F32))
    in_specs.append(pl.BlockSpec((k, tn), lambda i, j: (0, j)))
    args.append(w)
    if residual is not None:
        in_specs.append(pl.BlockSpec((tm, tn), lambda i, j: (i, j)))
        args.append(residual)
    return pl.pallas_call(
        functools.partial(_nm_body, norm=norm, has_res=residual is not None),
        grid=(m // tm, n // tn),
        in_specs=in_specs,
        out_specs=pl.BlockSpec((tm, tn), lambda i, j: (i, j)),
        out_shape=jax.ShapeDtypeStruct((m, n), out_dtype),
        scratch_shapes=[pltpu.VMEM((tm, k), BF16)] if norm else [],
        compiler_params=_cparams(("parallel", "arbitrary")),
        name=name,
    )(*args)


def _swap_halves(x, first_half):
    return jnp.where(first_half, pltpu.roll(x, 96, 1), pltpu.roll(x, 32, 1))


def _mla_prep_body(q_ref, kv_ref, kr_ref, pos_ref, invf_ref, sign_ref, gqn_ref, gqr_ref, gkn_ref, gkr_ref,
                   qf_ref, kf_ref, vf_ref):
    ts = q_ref.shape[0]
    lane = lax.broadcasted_iota(jnp.int32, (ts, LANES), 1)
    first_half = (lane & 63) < 32
    low64 = lane < 64
    ang = pos_ref[...] * invf_ref[...]
    cosv = jnp.cos(ang)
    sinv = jnp.sin(ang) * sign_ref[...]

    def rot(x):
        return x * cosv + _swap_halves(x, first_half) * sinv

    inv_d = 1.0 / MLA_QK
    zeros = jnp.zeros((ts, LANES), F32)

    kr = kr_ref[...]
    ss_kr = jnp.sum(kr * kr, axis=-1, keepdims=True)
    kr_rot = rot(kr * gkr_ref[...])
    for h in range(MLA_HEADS):
        kn = kv_ref[:, h * 256:h * 256 + 128]
        r = lax.rsqrt((jnp.sum(kn * kn, axis=-1, keepdims=True) + ss_kr) * inv_d + RMS_EPS)
        kf_ref[0, h, :, 0:128] = (kn * r * gkn_ref[...]).astype(BF16)
        kf_ref[0, h, :, 128:256] = (kr_rot * r).astype(BF16)
        vf_ref[0, h] = kv_ref[:, h * 256 + 128:(h + 1) * 256].astype(BF16)

    for p in range(MLA_HEADS // 2):
        xr = q_ref[:, 2048 + p * 128:2048 + (p + 1) * 128]
        xr2 = xr * xr
        ss_e = jnp.sum(jnp.where(low64, xr2, 0.0), axis=-1, keepdims=True)
        ss_o = jnp.sum(jnp.where(low64, 0.0, xr2), axis=-1, keepdims=True)
        qn_e = q_ref[:, (2 * p) * 128:(2 * p + 1) * 128]
        qn_o = q_ref[:, (2 * p + 1) * 128:(2 * p + 2) * 128]
        r_e = lax.rsqrt((jnp.sum(qn_e * qn_e, axis=-1, keepdims=True) + ss_e) * inv_d + RMS_EPS)
        r_o = lax.rsqrt((jnp.sum(qn_o * qn_o, axis=-1, keepdims=True) + ss_o) * inv_d + RMS_EPS)
        xr_rot = rot(xr * jnp.where(low64, r_e, r_o) * gqr_ref[...])
        qf_ref[0, 2 * p, :, 0:128] = (qn_e * r_e * gqn_ref[...]).astype(BF16)
        qf_ref[0, 2 * p, :, 128:256] = jnp.where(low64, xr_rot, zeros).astype(BF16)
        qf_ref[0, 2 * p + 1, :, 0:128] = (qn_o * r_o * gqn_ref[...]).astype(BF16)
        qf_ref[0, 2 * p + 1, :, 128:256] = jnp.where(low64, pltpu.roll(xr_rot, 64, 1), zeros).astype(BF16)


def mla_prep(qraw, kvraw, z, pos_col, rows, b, s, ts=256):
    ns = s // ts
    h = MLA_HEADS
    row = lambda bi, si: (bi * ns + si, 0)
    vec = pl.BlockSpec((1, LANES), lambda bi, si: (0, 0))
    head_out = lambda d: pl.BlockSpec((1, h, ts, d), lambda bi, si: (bi, 0, si, 0))
    return pl.pallas_call(
        _mla_prep_body,
        grid=(b, ns),
        in_specs=[
            pl.BlockSpec((ts, qraw.shape[1]), row),
            pl.BlockSpec((ts, kvraw.shape[1]), row),
            pl.BlockSpec((ts, LANES), lambda bi, si: (bi * ns + si, (MLA_Q_RANK + MLA_KV_RANK) // LANES)),
            pl.BlockSpec((ts, 1), row),
        ] + [vec] * 6,
        out_specs=[head_out(256), head_out(256), head_out(128)],
        out_shape=[
            jax.ShapeDtypeStruct((b, h, s, 256), BF16),
            jax.ShapeDtypeStruct((b, h, s, 256), BF16),
            jax.ShapeDtypeStruct((b, h, s, 128), BF16),
        ],
        compiler_params=_cparams(("parallel", "parallel")),
        name="mla_prep",
    )(qraw, kvraw, z, pos_col, *rows)


def _split3(x):
    hi = x.astype(BF16)
    r1 = x - hi.astype(F32)
    mid = r1.astype(BF16)
    lo = (r1 - mid.astype(F32)).astype(BF16)
    return hi, mid, lo


def _fox_prep_body(z_ref, bf_ref, gq_ref, gk_ref, qf_ref, kf_ref, vf_ref, ct_ref, carry_ref):
    ts = z_ref.shape[0]
    d = FOX_HEAD_DIM
    nh = FOX_HEADS

    @pl.when(pl.program_id(1) == 0)
    def _():
        carry_ref[...] = jnp.zeros_like(carry_ref)

    x = z_ref[:, 4 * nh * d:4 * nh * d + LANES] + bf_ref[...]
    logf = jnp.minimum(x, 0.0) - jnp.log1p(jnp.exp(-jnp.abs(x)))
    tri = (lax.broadcasted_iota(jnp.int32, (ts, ts), 0) >= lax.broadcasted_iota(jnp.int32, (ts, ts), 1)).astype(BF16)
    hi, mid, lo = _split3(logf)
    local = (jnp.dot(tri, hi, preferred_element_type=F32) + jnp.dot(tri, mid, preferred_element_type=F32)
             + jnp.dot(tri, lo, preferred_element_type=F32))
    c = carry_ref[0:1, :] + local
    carry_ref[0:1, :] = c[ts - 1:ts, :]
    ct_ref[0] = c.T

    inv_d = 1.0 / d
    for h in range(nh):
        q = z_ref[:, h * d:(h + 1) * d]
        k = z_ref[:, (nh + h) * d:(nh + h + 1) * d]
        rq = lax.rsqrt(jnp.sum(q * q, axis=-1, keepdims=True) * inv_d + RMS_EPS)
        rk = lax.rsqrt(jnp.sum(k * k, axis=-1, keepdims=True) * inv_d + RMS_EPS)
        qf_ref[0, h] = (q * rq * gq_ref[...]).astype(BF16)
        kf_ref[0, h] = (k * rk * gk_ref[...]).astype(BF16)
        vf_ref[0, h] = z_ref[:, (2 * nh + h) * d:(2 * nh + h + 1) * d].astype(BF16)


def fox_prep(zmain, rows, b, s, ts=256):
    ns = s // ts
    h = FOX_HEADS
    row = lambda bi, si: (bi * ns + si, 0)
    vec = pl.BlockSpec((1, LANES), lambda bi, si: (0, 0))
    head_out = pl.BlockSpec((1, h, ts, 128), lambda bi, si: (bi, 0, si, 0))
    return pl.pallas_call(
        _fox_prep_body,
        grid=(b, ns),
        in_specs=[pl.BlockSpec((ts, zmain.shape[1]), row), vec, vec, vec],
        out_specs=[head_out, head_out, head_out, pl.BlockSpec((1, LANES, ts), lambda bi, si: (bi, 0, si))],
        out_shape=[jax.ShapeDtypeStruct((b, h, s, 128), BF16)] * 3 + [jax.ShapeDtypeStruct((b, LANES, s), F32)],
        scratch_shapes=[pltpu.VMEM((8, LANES), F32)],
        compiler_params=_cparams(("parallel", "arbitrary")),
        name="fox_prep",
    )(zmain, *rows)


def _attn_body(*refs, fox, tq, nq):
    if fox:
        q_ref, k_ref, v_ref, ccol_ref, crow_ref, gate_ref, o_ref, s_ref, p_ref = refs
    else:
        q_ref, k_ref, v_ref, o_ref, s_ref, p_ref = refs
    nt = (((1,), (1,)), ((), ()))
    half = tq // 2
    r = lax.broadcasted_iota(jnp.int32, (tq, tq), 0)
    c = lax.broadcasted_iota(jnp.int32, (tq, tq), 1)
    allowed = (c <= r) if fox else ((c >> CHUNK_SHIFT) <= (r >> CHUNK_SHIFT))
    for i in range(nq):
        buf = i % 2
        rows = slice(i * tq, (i + 1) * tq)
        q = q_ref[0, 0, rows, :]
        mx = jnp.full((tq, half), NEG_INF, F32)
        for j in range(i + 1):
            cols = slice(j * tq, (j + 1) * tq)
            s = lax.dot_general(q, k_ref[0, 0, cols, :], nt, preferred_element_type=F32)
            if fox:
                s = s + (ccol_ref[0, 0, rows, :] - crow_ref[0, 0, :, cols])
            if j == i:
                s = jnp.where(allowed, s, NEG_INF)
            s_ref[buf, :, cols] = s
            mx = jnp.maximum(mx, jnp.maximum(s[:, :half], s[:, half:]))
        m = jnp.max(mx, axis=-1, keepdims=True)
        ps = jnp.zeros((tq, half), F32)
        for j in range(i + 1):
            cols = slice(j * tq, (j + 1) * tq)
            p = jnp.exp(s_ref[buf, :, cols] - m)
            ps = ps + (p[:, :half] + p[:, half:])
            p_ref[buf, :, cols] = p.astype(BF16)
        l = jnp.sum(ps, axis=-1, keepdims=True)
        n_keys = (i + 1) * tq
        o = jnp.dot(p_ref[buf, :, :n_keys], v_ref[0, 0, :n_keys, :], preferred_element_type=F32) / l
        if fox:
            o = o * (1.0 / (1.0 + jnp.exp(-gate_ref[rows, :])))
        o_ref[rows, :] = o.astype(o_ref.dtype)


def attention(qf, kf, vf, *, fox, ccol=None, crow=None, gate=None, gate_col0=0, tq=256):
    b, h, s, dk = qf.shape
    dv = vf.shape[-1]
    nq = s // tq
    head = lambda d: pl.BlockSpec((1, 1, s, d), lambda bi, hi: (bi, hi, 0, 0))
    in_specs = [head(dk), head(dk), head(dv)]
    args = [qf, kf, vf]
    if fox:
        in_specs += [
            head(1),
            pl.BlockSpec((1, 1, 1, s), lambda bi, hi: (bi, hi, 0, 0)),
            pl.BlockSpec((s, dv), lambda bi, hi: (bi, gate_col0 + hi)),
        ]
        args += [ccol, crow.reshape(b, h, 1, s), gate]
    return pl.pallas_call(
        functools.partial(_attn_body, fox=fox, tq=tq, nq=nq),
        grid=(b, h),
        in_specs=in_specs,
        out_specs=pl.BlockSpec((s, dv), lambda bi, hi: (bi, hi)),
        out_shape=jax.ShapeDtypeStruct((b * s, h * dv), BF16),
        scratch_shapes=[pltpu.VMEM((2, tq, s), F32), pltpu.VMEM((2, tq, s), BF16)],
        compiler_params=_cparams(("parallel", "parallel")),
        name="fox_attn" if fox else "mla_attn",
    )(*args)


_CAND_PAIRS = [(p, q) for p in range(PEER_TOPK) for q in range(PEER_TOPK) if (p + 1) * (q + 1) <= PEER_TOPK]
_CAND_ROWS = 56


def _top_rows(v, n, want_rank=False):
    rank = jnp.full(v.shape, float(n), F32)
    out = []
    for r in range(n):
        m = jnp.max(v, axis=0, keepdims=True)
        out.append(m)
        hit = v == m
        if want_rank:
            rank = jnp.where(hit, float(r), rank)
        if r + 1 < n:
            v = jnp.where(hit, -jnp.inf, v)
    return (out, rank) if want_rank else out


def _peer_prep_body(x_ref, g_ref, wq_ref, sk_ref, xn_ref, rk2_ref, e2_ref, n1_ref, c1_ref, s_ref, cand_ref):
    @pl.when(pl.program_id(1) == 0)
    def _():
        a = x_ref[...]
        y = a * lax.rsqrt(jnp.mean(a * a, axis=-1, keepdims=True) + RMS_EPS)
        xn_ref[...] = (y * g_ref[...]).T.astype(BF16)

    qt = jnp.dot(wq_ref[...], xn_ref[...], preferred_element_type=F32)
    s_ref[0] = jnp.dot(sk_ref[0], qt[0:128].astype(BF16), preferred_element_type=F32)
    s_ref[1] = jnp.dot(sk_ref[1], qt[128:256].astype(BF16), preferred_element_type=F32)
    cand_ref[...] = jnp.full(cand_ref.shape, -jnp.inf, F32)

    def lane_group(c, carry):
        lanes = pl.ds(pl.multiple_of(c * LANES, LANES), LANES)
        s1 = s_ref[0, :, lanes]
        s2 = s_ref[1, :, lanes]
        a1, rank1 = _top_rows(s1, PEER_TOPK, want_rank=True)
        a2, rank2 = _top_rows(s2, PEER_TOPK, want_rank=True)
        for r, (p, q) in enumerate(_CAND_PAIRS):
            cand_ref[r:r + 1, :] = a1[p] + a2[q]
        best = _top_rows(cand_ref[...], PEER_TOPK)
        tau = best[PEER_TOPK - 1]
        z = jnp.ones_like(tau)
        for r in range(1, PEER_TOPK):
            z = z + jnp.exp(best[r] - best[0])
        n1 = jnp.zeros(s1.shape, F32)
        for p in range(PEER_TOPK):
            n_p = jnp.zeros_like(tau)
            for q in range(PEER_TOPK // (p + 1)):
                n_p = n_p + jnp.where(a1[p] + a2[q] >= tau, 1.0, 0.0)
            n1 = jnp.where(rank1 == float(p), n_p, n1)
        rk2_ref[0, :, lanes] = rank2.astype(BF16)
        e2_ref[0, :, lanes] = jnp.exp(s2 - a2[0]).astype(BF16)
        n1_ref[0, :, lanes] = n1
        c1_ref[0, :, lanes] = jnp.exp(s1 - a1[0]) / z
        return carry

    lax.fori_loop(0, s_ref.shape[-1] // LANES, lane_group, 0, unroll=2)


def peer_prep(x, gain, wq_t, sk, layer, tm=512):
    t, d = x.shape
    nh = PEER_HEADS
    stat = pl.BlockSpec((1, PEER_N_KEYS, tm), lambda i, h: (h, 0, i))
    stat_f32 = jax.ShapeDtypeStruct((nh, PEER_N_KEYS, t), F32)
    stat_bf16 = jax.ShapeDtypeStruct((nh, PEER_N_KEYS, t), BF16)
    return pl.pallas_call(
        _peer_prep_body,
        grid=(t // tm, nh),
        in_specs=[
            pl.BlockSpec((tm, d), lambda i, h: (i, 0)),
            pl.BlockSpec((1, d), lambda i, h: (0, 0)),
            pl.BlockSpec((None, 2 * PEER_D_HALF, d), lambda i, h: (layer, h, 0)),
            pl.BlockSpec((None, 2, PEER_N_KEYS, PEER_D_HALF), lambda i, h: (layer, 0, 0, 0)),
        ],
        out_specs=[pl.BlockSpec((d, tm), lambda i, h: (0, i)), stat, stat, stat, stat],
        out_shape=[jax.ShapeDtypeStruct((d, t), BF16), stat_bf16, stat_bf16, stat_f32, stat_f32],
        scratch_shapes=[pltpu.VMEM((2, PEER_N_KEYS, tm), F32), pltpu.VMEM((_CAND_ROWS, LANES), F32)],
        compiler_params=_cparams(("parallel", "arbitrary")),
        name="peer_prep",
    )(x, gain.reshape(1, d).astype(F32), wq_t, sk)


_SLAB = 16
_GATE_LANES = 256


def _peer_main_body(xn_ref, u_ref, vta_ref, vtb_ref, rk2_ref, e2_ref, n1_ref, c1_ref, res_ref, o_ref,
                    acc_ref, h_ref, a_ref):
    k = pl.program_id(1)
    nk = pl.num_programs(1) - 1
    te, tm = h_ref.shape[1:]
    n_i = te // PEER_N_KEYS
    n_slab = PEER_N_KEYS // _SLAB
    zero = jnp.zeros((_SLAB, _GATE_LANES), BF16)

    @pl.when((pl.program_id(0) == 0) & (k == 0))
    def _():
        h_ref[...] = jnp.zeros_like(h_ref)

    @pl.when(k == 0)
    def _():
        acc_ref[...] = jnp.zeros_like(acc_ref)

    def gate_stage(slot, blk):
        valid = (blk >= 0) & (blk < 2 * nk)
        for lc in range(tm // _GATE_LANES):
            lanes = slice(lc * _GATE_LANES, (lc + 1) * _GATE_LANES)
            for ii in range(n_i):
                i = jnp.clip(blk * n_i + ii, 0, PEER_N_KEYS - 1)
                g = [None] * n_slab
                for h in range(PEER_HEADS):
                    n_row = jnp.where(valid, n1_ref[h, pl.ds(i, 1), lanes], 0.0)
                    nb = jnp.broadcast_to(n_row, (_SLAB, _GATE_LANES)).astype(BF16)
                    cb = jnp.broadcast_to(c1_ref[h, pl.ds(i, 1), lanes], (_SLAB, _GATE_LANES)).astype(BF16)
                    for sl in range(n_slab):
                        rows = slice(sl * _SLAB, (sl + 1) * _SLAB)
                        w = jnp.where(rk2_ref[h, rows, lanes] < nb, e2_ref[h, rows, lanes] * cb, zero)
                        g[sl] = w if g[sl] is None else g[sl] + w
                for sl in range(n_slab):
                    rows = slice(ii * PEER_N_KEYS + sl * _SLAB, ii * PEER_N_KEYS + (sl + 1) * _SLAB)
                    hh = h_ref[slot, rows, lanes]
                    act = 0.5 * hh * (1.0 + lax.erf(hh * (1.0 / math.sqrt(2.0))))
                    a_ref[slot, rows, lanes] = act.astype(BF16) * g[sl]

    gate_stage(1, 2 * k - 1)
    h_ref[0] = jnp.dot(u_ref[0:te, :], xn_ref[...], preferred_element_type=F32)
    acc_ref[...] += jnp.dot(vta_ref[...], a_ref[1], preferred_element_type=F32)
    gate_stage(0, 2 * k)
    h_ref[1] = jnp.dot(u_ref[te:2 * te, :], xn_ref[...], preferred_element_type=F32)
    acc_ref[...] += jnp.dot(vtb_ref[...], a_ref[0], preferred_element_type=F32)

    @pl.when(k == nk)
    def _():
        o_ref[...] = res_ref[...] + acc_ref[...].T


def peer_main(xn_t, u, vt, rk2, e2, n1, c1, res, layer, tm=512):
    d, t = xn_t.shape
    te = vt.shape[-1]
    nk = u.shape[1] // (2 * te)
    nh = PEER_HEADS
    once = pl.Buffered(1)
    stat = pl.BlockSpec((nh, PEER_N_KEYS, tm), lambda i, k: (0, 0, i), pipeline_mode=once)
    return pl.pallas_call(
        _peer_main_body,
        grid=(t // tm, nk + 1),
        in_specs=[
            pl.BlockSpec((d, tm), lambda i, k: (0, i), pipeline_mode=once),
            pl.BlockSpec((None, 2 * te, d), lambda i, k: (layer, jnp.minimum(k, nk - 1), 0)),
            pl.BlockSpec((None, None, d, te), lambda i, k: (layer, jnp.clip(2 * k - 1, 0, 2 * nk - 1), 0, 0)),
            pl.BlockSpec((None, None, d, te), lambda i, k: (layer, jnp.minimum(2 * k, 2 * nk - 1), 0, 0)),
            stat, stat, stat, stat,
            pl.BlockSpec((tm, d), lambda i, k: (i, 0), pipeline_mode=once),
        ],
        out_specs=pl.BlockSpec((tm, d), lambda i, k: (i, 0)),
        out_shape=jax.ShapeDtypeStruct((t, d), F32),
        scratch_shapes=[pltpu.VMEM((d, tm), F32), pltpu.VMEM((2, te, tm), F32), pltpu.VMEM((2, te, tm), BF16)],
        compiler_params=_cparams(("parallel", "arbitrary")),
        name="peer_main",
    )(xn_t, u, vt, vt, rk2, e2, n1, c1, res)


PEER_BLOCK = 512


def peer_weights(w_q, sub_keys, u, v):
    n_layers, n_e, d = v.shape
    wq_t = jnp.swapaxes(w_q, 1, 2).astype(BF16)
    vt = jnp.swapaxes(v.reshape(n_layers, n_e // PEER_BLOCK, PEER_BLOCK, d), 2, 3).astype(BF16)
    return wq_t, sub_keys.astype(BF16), u.astype(BF16), vt


def peer_ffn(x, gain, weights, layer):
    wq_t, sk, u, vt = weights
    xn_t, rk2, e2, n1, c1 = peer_prep(x, gain, wq_t, sk, layer)
    return peer_main(xn_t, u, vt, rk2, e2, n1, c1, x, layer)


def _row128(v):
    return v.reshape(1, LANES).astype(F32)


def mla_mixer(x, gain, positions, w_in, g_qa, w_qb, g_kva, w_kvb, g_q, g_k, w_o, b, s):
    t = b * s
    nh = MLA_HEADS
    w_in_p = jnp.pad(w_in, ((0, 0), (0, 64))).astype(BF16)
    z = norm_matmul(x, w_in_p, gain=gain, tn=384, name="mla_in")
    wq = w_qb.reshape(MLA_Q_RANK, nh, MLA_QK)
    wq = jnp.concatenate([wq[:, :, :MLA_NOPE].reshape(MLA_Q_RANK, -1), wq[:, :, MLA_NOPE:].reshape(MLA_Q_RANK, -1)], 1)
    qraw = norm_matmul(z, wq.astype(BF16), gain=g_qa, a_col_block=0, name="mla_qb")
    kvraw = norm_matmul(z, w_kvb.astype(BF16), gain=g_kva, a_col_block=1, name="mla_kvb")

    scale = 1.0 / math.sqrt(MLA_QK)
    inv_freq = ROPE_THETA ** (-jnp.arange(0, MLA_ROPE, 2, dtype=F32) / MLA_ROPE)
    sign = jnp.where((jnp.arange(LANES) % 64) < 32, -1.0, 1.0)
    rows = [
        _row128(jnp.tile(inv_freq, 4)),
        _row128(sign),
        _row128(g_q[:MLA_NOPE] * scale),
        _row128(jnp.tile(g_q[MLA_NOPE:], 2) * scale),
        _row128(g_k[:MLA_NOPE]),
        _row128(jnp.pad(g_k[MLA_NOPE:], (0, 64))),
    ]
    pos_col = positions.astype(F32).reshape(t, 1)
    qf, kf, vf = mla_prep(qraw, kvraw, z, pos_col, rows, b, s)
    o = attention(qf, kf, vf, fox=False)
    return norm_matmul(o, w_o.astype(BF16), residual=x, name="mla_out")


def fox_mixer(x, gain, w_in, b_f, g_q, g_k, w_o, b, s):
    nh = FOX_HEADS
    d = x.shape[1]
    w_in_p = jnp.pad(w_in, ((0, 0), (0, LANES - nh))).astype(BF16)
    zmain = norm_matmul(x, w_in_p, gain=gain, tm=1024, tn=640, name="fox_in")
    scale = 1.0 / math.sqrt(FOX_HEAD_DIM)
    rows = [_row128(jnp.pad(b_f, (0, LANES - nh))), _row128(g_q * scale), _row128(g_k)]
    qf, kf, vf, ct = fox_prep(zmain, rows, b, s)
    crow = ct[:, :nh, :]
    o = attention(qf, kf, vf, fox=True, ccol=crow[..., None], crow=crow, gate=zmain, gate_col0=3 * nh)
    return norm_matmul(o, w_o.astype(BF16), residual=x, name="fox_out")


def kernel(x, positions, norm_mix_g, norm_ffn_g, mla_w_in, mla_g_qa, mla_w_qb, mla_g_kva, mla_w_kvb, mla_g_q, mla_g_k, mla_w_o, fox_w_in, fox_b_f, fox_g_q, fox_g_k, fox_w_o, peer_w_q, peer_sub_keys, peer_u, peer_v):
    b, s, d = x.shape
    depth = norm_mix_g.shape[0]
    xt = x.reshape(b * s, d)
    peer_w = peer_weights(peer_w_q, peer_sub_keys, peer_u, peer_v)
    for i in range(depth):
        j = i // 2
        if i % 2 == 0:
            xt = mla_mixer(xt, norm_mix_g[i], positions, mla_w_in[j], mla_g_qa[j], mla_w_qb[j], mla_g_kva[j],
                           mla_w_kvb[j], mla_g_q[j], mla_g_k[j], mla_w_o[j], b, s)
        else:
            xt = fox_mixer(xt, norm_mix_g[i], fox_w_in[j], fox_b_f[j], fox_g_q[j], fox_g_k[j], fox_w_o[j], b, s)
        xt = peer_ffn(xt, norm_ffn_g[i], peer_w, i)
    return xt.reshape(b, s, d)
```

```python
import functools
import math

import jax
import jax.numpy as jnp
from jax import lax
from jax.experimental import pallas as pl
from jax.experimental.pallas import tpu as pltpu

F32 = jnp.float32
BF16 = jnp.bfloat16

RMS_EPS = 1e-6
NEG_INF = -1e30
CHUNK_SHIFT = 6
ROPE_THETA = 10000.0

MLA_HEADS = 16
MLA_Q_RANK = 512
MLA_KV_RANK = 512
MLA_NOPE = 128
MLA_ROPE = 64
MLA_V = 128
MLA_QK = MLA_NOPE + MLA_ROPE

FOX_HEADS = 16
FOX_HEAD_DIM = 128

PEER_HEADS = 8
PEER_N_KEYS = 128
PEER_D_HALF = 128
PEER_TOPK = 16

LANES = 128
VMEM_LIMIT = 56 * 1024 * 1024


def _cparams(sem, flags=None):
    return pltpu.CompilerParams(dimension_semantics=sem, vmem_limit_bytes=VMEM_LIMIT, flags=flags)


def _nm_body(*refs, norm, has_res):
    it = iter(refs)
    a_ref = next(it)
    g_ref = next(it) if norm else None
    w_ref = next(it)
    r_ref = next(it) if has_res else None
    o_ref = next(it)
    an_ref = next(it) if norm else None

    if norm:
        @pl.when(pl.program_id(1) == 0)
        def _():
            a = a_ref[...].astype(F32)
            y = a * lax.rsqrt(jnp.mean(a * a, axis=-1, keepdims=True) + RMS_EPS)
            an_ref[...] = (y * g_ref[...]).astype(BF16)

        a_bf = an_ref[...]
    else:
        a_bf = a_ref[...]
    acc = jnp.dot(a_bf, w_ref[...], preferred_element_type=F32)
    if has_res:
        acc = acc + r_ref[...]
    o_ref[...] = acc.astype(o_ref.dtype)


def norm_matmul(a, w, *, gain=None, residual=None, a_col_block=0, out_dtype=F32, tm=512, tn=512, name):
    m = a.shape[0]
    k, n = w.shape
    tn = min(tn, n)
    assert m % tm == 0 and n % tn == 0
    norm = gain is not None
    in_specs = [pl.BlockSpec((tm, k), lambda i, j: (i, a_col_block))]
    args = [a]
    if norm:
        in_specs.append(pl.BlockSpec((1, k), lambda i, j: (0, 0)))
        args.append(gain.reshape(1, k).astype(F32))
    in_specs.append(pl.BlockSpec((k, tn), lambda i, j: (0, j)))
    args.append(w)
    if residual is not None:
        in_specs.append(pl.BlockSpec((tm, tn), lambda i, j: (i, j)))
        args.append(residual)
    return pl.pallas_call(
        functools.partial(_nm_body, norm=norm, has_res=residual is not None),
        grid=(m // tm, n // tn),
        in_specs=in_specs,
        out_specs=pl.BlockSpec((tm, tn), lambda i, j: (i, j)),
        out_shape=jax.ShapeDtypeStruct((m, n), out_dtype),
        scratch_shapes=[pltpu.VMEM((tm, k), BF16)] if norm else [],
        compiler_params=_cparams(("parallel", "arbitrary")),
        name=name,
    )(*args)


def _swap_halves(x, first_half):
    return jnp.where(first_half, pltpu.roll(x, 96, 1), pltpu.roll(x, 32, 1))


def _mla_prep_body(q_ref, kv_ref, kr_ref, pos_ref, invf_ref, sign_ref, gqn_ref, gqr_ref, gkn_ref, gkr_ref,
                   qf_ref, kf_ref, vf_ref):
    ts = q_ref.shape[0]
    lane = lax.broadcasted_iota(jnp.int32, (ts, LANES), 1)
    first_half = (lane & 63) < 32
    low64 = lane < 64
    ang = pos_ref[...] * invf_ref[...]
    cosv = jnp.cos(ang)
    sinv = jnp.sin(ang) * sign_ref[...]

    def rot(x):
        return x * cosv + _swap_halves(x, first_half) * sinv

    inv_d = 1.0 / MLA_QK
    zeros = jnp.zeros((ts, LANES), F32)

    kr = kr_ref[...]
    ss_kr = jnp.sum(kr * kr, axis=-1, keepdims=True)
    kr_rot = rot(kr * gkr_ref[...])
    for h in range(MLA_HEADS):
        kn = kv_ref[:, h * 256:h * 256 + 128].astype(F32)
        r = lax.rsqrt((jnp.sum(kn * kn, axis=-1, keepdims=True) + ss_kr) * inv_d + RMS_EPS)
        kf_ref[0, h, :, 0:128] = (kn * r * gkn_ref[...]).astype(BF16)
        kf_ref[0, h, :, 128:256] = (kr_rot * r).astype(BF16)
        vf_ref[0, h] = kv_ref[:, h * 256 + 128:(h + 1) * 256]

    for p in range(MLA_HEADS // 2):
        xr = q_ref[:, 2048 + p * 128:2048 + (p + 1) * 128].astype(F32)
        xr2 = xr * xr
        ss_e = jnp.sum(jnp.where(low64, xr2, 0.0), axis=-1, keepdims=True)
        ss_o = jnp.sum(jnp.where(low64, 0.0, xr2), axis=-1, keepdims=True)
        qn_e = q_ref[:, (2 * p) * 128:(2 * p + 1) * 128].astype(F32)
        qn_o = q_ref[:, (2 * p + 1) * 128:(2 * p + 2) * 128].astype(F32)
        r_e = lax.rsqrt((jnp.sum(qn_e * qn_e, axis=-1, keepdims=True) + ss_e) * inv_d + RMS_EPS)
        r_o = lax.rsqrt((jnp.sum(qn_o * qn_o, axis=-1, keepdims=True) + ss_o) * inv_d + RMS_EPS)
        xr_rot = rot(xr * jnp.where(low64, r_e, r_o) * gqr_ref[...])
        qf_ref[0, 2 * p, :, 0:128] = (qn_e * r_e * gqn_ref[...]).astype(BF16)
        qf_ref[0, 2 * p, :, 128:256] = jnp.where(low64, xr_rot, zeros).astype(BF16)
        qf_ref[0, 2 * p + 1, :, 0:128] = (qn_o * r_o * gqn_ref[...]).astype(BF16)
        qf_ref[0, 2 * p + 1, :, 128:256] = jnp.where(low64, pltpu.roll(xr_rot, 64, 1), zeros).astype(BF16)


def mla_prep(qraw, kvraw, z, pos_col, rows, b, s, ts=256):
    ns = s // ts
    h = MLA_HEADS
    row = lambda bi, si: (bi * ns + si, 0)
    vec = pl.BlockSpec((1, LANES), lambda bi, si: (0, 0))
    head_out = lambda d: pl.BlockSpec((1, h, ts, d), lambda bi, si: (bi, 0, si, 0))
    return pl.pallas_call(
        _mla_prep_body,
        grid=(b, ns),
        in_specs=[
            pl.BlockSpec((ts, qraw.shape[1]), row),
            pl.BlockSpec((ts, kvraw.shape[1]), row),
            pl.BlockSpec((ts, LANES), lambda bi, si: (bi * ns + si, (MLA_Q_RANK + MLA_KV_RANK) // LANES)),
            pl.BlockSpec((ts, 1), row),
        ] + [vec] * 6,
        out_specs=[head_out(256), head_out(256), head_out(128)],
        out_shape=[
            jax.ShapeDtypeStruct((b, h, s, 256), BF16),
            jax.ShapeDtypeStruct((b, h, s, 256), BF16),
            jax.ShapeDtypeStruct((b, h, s, 128), BF16),
        ],
        compiler_params=_cparams(("parallel", "parallel")),
        name="mla_prep",
    )(qraw, kvraw, z, pos_col, *rows)


def _split3(x):
    hi = x.astype(BF16)
    r1 = x - hi.astype(F32)
    mid = r1.astype(BF16)
    lo = (r1 - mid.astype(F32)).astype(BF16)
    return hi, mid, lo


def _fox_prep_body(z_ref, f_ref, bf_ref, gq_ref, gk_ref, qf_ref, kf_ref, vf_ref, ct_ref, carry_ref):
    ts = z_ref.shape[0]
    d = FOX_HEAD_DIM
    nh = FOX_HEADS

    @pl.when(pl.program_id(1) == 0)
    def _():
        carry_ref[...] = jnp.zeros_like(carry_ref)

    x = f_ref[...] + bf_ref[...]
    logf = jnp.minimum(x, 0.0) - jnp.log1p(jnp.exp(-jnp.abs(x)))
    tri = (lax.broadcasted_iota(jnp.int32, (ts, ts), 0) >= lax.broadcasted_iota(jnp.int32, (ts, ts), 1)).astype(BF16)
    hi, mid, lo = _split3(logf)
    local = (jnp.dot(tri, hi, preferred_element_type=F32) + jnp.dot(tri, mid, preferred_element_type=F32)
             + jnp.dot(tri, lo, preferred_element_type=F32))
    c = carry_ref[0:1, :] + local
    carry_ref[0:1, :] = c[ts - 1:ts, :]
    ct_ref[0] = c.T

    inv_d = 1.0 / d
    for h in range(nh):
        q = z_ref[:, h * d:(h + 1) * d].astype(F32)
        k = z_ref[:, (nh + h) * d:(nh + h + 1) * d].astype(F32)
        rq = lax.rsqrt(jnp.sum(q * q, axis=-1, keepdims=True) * inv_d + RMS_EPS)
        rk = lax.rsqrt(jnp.sum(k * k, axis=-1, keepdims=True) * inv_d + RMS_EPS)
        qf_ref[0, h] = (q * rq * gq_ref[...]).astype(BF16)
        kf_ref[0, h] = (k * rk * gk_ref[...]).astype(BF16)
        vf_ref[0, h] = z_ref[:, (2 * nh + h) * d:(2 * nh + h + 1) * d]


def fox_prep(zmain, flog, rows, b, s, ts=256):
    ns = s // ts
    h = FOX_HEADS
    row = lambda bi, si: (bi * ns + si, 0)
    vec = pl.BlockSpec((1, LANES), lambda bi, si: (0, 0))
    head_out = pl.BlockSpec((1, h, ts, 128), lambda bi, si: (bi, 0, si, 0))
    return pl.pallas_call(
        _fox_prep_body,
        grid=(b, ns),
        in_specs=[pl.BlockSpec((ts, zmain.shape[1]), row), pl.BlockSpec((ts, LANES), row), vec, vec, vec],
        out_specs=[head_out, head_out, head_out, pl.BlockSpec((1, LANES, ts), lambda bi, si: (bi, 0, si))],
        out_shape=[jax.ShapeDtypeStruct((b, h, s, 128), BF16)] * 3 + [jax.ShapeDtypeStruct((b, LANES, s), F32)],
        scratch_shapes=[pltpu.VMEM((8, LANES), F32)],
        compiler_params=_cparams(("parallel", "arbitrary")),
        name="fox_prep",
    )(zmain, flog, *rows)


_ATTN_HEADS = 2


def _attn_body(*refs, fox, tq, nq):
    if fox:
        q_ref, k_ref, v_ref, ccol_ref, crow_ref, gate_ref, o_ref, s_ref, p_ref = refs
    else:
        q_ref, k_ref, v_ref, o_ref, s_ref, p_ref = refs
    nt = (((1,), (1,)), ((), ()))
    half = tq // 2
    dv = v_ref.shape[-1]
    r = lax.broadcasted_iota(jnp.int32, (tq, tq), 0)
    c = lax.broadcasted_iota(jnp.int32, (tq, tq), 1)
    allowed = (c <= r) if fox else ((c >> CHUNK_SHIFT) <= (r >> CHUNK_SHIFT))
    for i in range(nq):
        for hh in range(_ATTN_HEADS):
            buf = 2 * hh + i % 2
            rows = slice(i * tq, (i + 1) * tq)
            q = q_ref[0, hh, rows, :]
            mx = jnp.full((tq, half), NEG_INF, F32)
            for j in range(i + 1):
                cols = slice(j * tq, (j + 1) * tq)
                s = lax.dot_general(q, k_ref[0, hh, cols, :], nt, preferred_element_type=F32)
                if fox:
                    s = s + (ccol_ref[0, hh, rows, :] - crow_ref[0, hh, :, cols])
                if j == i:
                    s = jnp.where(allowed, s, NEG_INF)
                s_ref[buf, :, cols] = s
                mx = jnp.maximum(mx, jnp.maximum(s[:, :half], s[:, half:]))
            m = jnp.max(mx, axis=-1, keepdims=True)
            ps = jnp.zeros((tq, half), F32)
            for j in range(i + 1):
                cols = slice(j * tq, (j + 1) * tq)
                p = jnp.exp(s_ref[buf, :, cols] - m)
                ps = ps + (p[:, :half] + p[:, half:])
                p_ref[buf, :, cols] = p.astype(BF16)
            l = jnp.sum(ps, axis=-1, keepdims=True)
            n_keys = (i + 1) * tq
            o = jnp.dot(p_ref[buf, :, :n_keys], v_ref[0, hh, :n_keys, :], preferred_element_type=F32) / l
            out_cols = slice(hh * dv, (hh + 1) * dv)
            if fox:
                o = o * (1.0 / (1.0 + jnp.exp(-gate_ref[rows, out_cols].astype(F32))))
            o_ref[rows, out_cols] = o.astype(o_ref.dtype)


def attention(qf, kf, vf, *, fox, ccol=None, crow=None, gate=None, gate_col0=0, tq=256):
    b, h, s, dk = qf.shape
    dv = vf.shape[-1]
    nq = s // tq
    hps = _ATTN_HEADS
    head = lambda d: pl.BlockSpec((1, hps, s, d), lambda bi, hi: (bi, hi, 0, 0))
    in_specs = [head(dk), head(dk), head(dv)]
    args = [qf, kf, vf]
    if fox:
        in_specs += [
            head(1),
            pl.BlockSpec((1, hps, 1, s), lambda bi, hi: (bi, hi, 0, 0)),
            pl.BlockSpec((s, hps * dv), lambda bi, hi: (bi, gate_col0 // hps + hi)),
        ]
        args += [ccol, crow.reshape(b, h, 1, s), gate]
    return pl.pallas_call(
        functools.partial(_attn_body, fox=fox, tq=tq, nq=nq),
        grid=(b, h // hps),
        in_specs=in_specs,
        out_specs=pl.BlockSpec((s, hps * dv), lambda bi, hi: (bi, hi)),
        out_shape=jax.ShapeDtypeStruct((b * s, h * dv), BF16),
        scratch_shapes=[pltpu.VMEM((2 * hps, tq, s), F32), pltpu.VMEM((2 * hps, tq, s), BF16)],
        compiler_params=_cparams(("parallel", "parallel")),
        name="fox_attn" if fox else "mla_attn",
    )(*args)


_CAND_PAIRS = [(p, q) for p in range(PEER_TOPK) for q in range(PEER_TOPK) if (p + 1) * (q + 1) <= PEER_TOPK]
_CAND_ROWS = 56


def _top_rows(v, n, want_rank=False):
    rank = jnp.full(v.shape, float(n), F32)
    out = []
    for r in range(n):
        m = jnp.max(v, axis=0, keepdims=True)
        out.append(m)
        hit = v == m
        if want_rank:
            rank = jnp.where(hit, float(r), rank)
        if r + 1 < n:
            v = jnp.where(hit, -jnp.inf, v)
    return (out, rank) if want_rank else out


def _peer_prep_body(x_ref, g_ref, wq_ref, sk_ref, xn_ref, rk2_ref, e2_ref, n1_ref, c1_ref, s_ref, cand_ref):
    @pl.when(pl.program_id(1) == 0)
    def _():
        a = x_ref[...]
        y = a * lax.rsqrt(jnp.mean(a * a, axis=-1, keepdims=True) + RMS_EPS)
        xn_ref[...] = (y * g_ref[...]).T.astype(BF16)

    qt = jnp.dot(wq_ref[...], xn_ref[...], preferred_element_type=F32)
    s_ref[0] = jnp.dot(sk_ref[0], qt[0:128].astype(BF16), preferred_element_type=F32)
    s_ref[1] = jnp.dot(sk_ref[1], qt[128:256].astype(BF16), preferred_element_type=F32)
    cand_ref[...] = jnp.full(cand_ref.shape, -jnp.inf, F32)

    def lane_group(c, carry):
        lanes = pl.ds(pl.multiple_of(c * LANES, LANES), LANES)
        s1 = s_ref[0, :, lanes]
        s2 = s_ref[1, :, lanes]
        a1, rank1 = _top_rows(s1, PEER_TOPK, want_rank=True)
        a2, rank2 = _top_rows(s2, PEER_TOPK, want_rank=True)
        for r, (p, q) in enumerate(_CAND_PAIRS):
            cand_ref[r:r + 1, :] = a1[p] + a2[q]
        best = _top_rows(cand_ref[...], PEER_TOPK)
        tau = best[PEER_TOPK - 1]
        z = jnp.ones_like(tau)
        for r in range(1, PEER_TOPK):
            z = z + jnp.exp(best[r] - best[0])
        n1 = jnp.zeros(s1.shape, F32)
        for p in range(PEER_TOPK):
            n_p = jnp.zeros_like(tau)
            for q in range(PEER_TOPK // (p + 1)):
                n_p = n_p + jnp.where(a1[p] + a2[q] >= tau, 1.0, 0.0)
            n1 = jnp.where(rank1 == float(p), n_p, n1)
        rk2_ref[0, :, lanes] = rank2.astype(BF16)
        e2_ref[0, :, lanes] = jnp.exp(s2 - a2[0]).astype(BF16)
        n1_ref[0, :, lanes] = n1
        c1_ref[0, :, lanes] = jnp.exp(s1 - a1[0]) / z
        return carry

    lax.fori_loop(0, s_ref.shape[-1] // LANES, lane_group, 0, unroll=2)


def peer_prep(x, gain, wq_t, sk, layer, tm=512):
    t, d = x.shape
    nh = PEER_HEADS
    stat = pl.BlockSpec((1, PEER_N_KEYS, tm), lambda i, h: (h, 0, i))
    stat_f32 = jax.ShapeDtypeStruct((nh, PEER_N_KEYS, t), F32)
    stat_bf16 = jax.ShapeDtypeStruct((nh, PEER_N_KEYS, t), BF16)
    return pl.pallas_call(
        _peer_prep_body,
        grid=(t // tm, nh),
        in_specs=[
            pl.BlockSpec((tm, d), lambda i, h: (i, 0)),
            pl.BlockSpec((1, d), lambda i, h: (0, 0)),
            pl.BlockSpec((None, 2 * PEER_D_HALF, d), lambda i, h: (layer, h, 0)),
            pl.BlockSpec((None, 2, PEER_N_KEYS, PEER_D_HALF), lambda i, h: (layer, 0, 0, 0)),
        ],
        out_specs=[pl.BlockSpec((d, tm), lambda i, h: (0, i)), stat, stat, stat, stat],
        out_shape=[jax.ShapeDtypeStruct((d, t), BF16), stat_bf16, stat_bf16, stat_f32, stat_f32],
        scratch_shapes=[pltpu.VMEM((2, PEER_N_KEYS, tm), F32), pltpu.VMEM((_CAND_ROWS, LANES), F32)],
        compiler_params=_cparams(("parallel", "arbitrary")),
        name="peer_prep",
    )(x, gain.reshape(1, d).astype(F32), wq_t, sk)


_SLAB = 16
_GATE_LANES = 256


def _peer_main_body(xn_ref, u_ref, vta_ref, vtb_ref, rk2_ref, e2_ref, n1_ref, c1_ref, res_ref, o_ref,
                    acc_ref, h_ref, a_ref):
    k = pl.program_id(1)
    nk = pl.num_programs(1) - 1
    te, tm = h_ref.shape[1:]
    n_i = te // PEER_N_KEYS
    n_slab = PEER_N_KEYS // _SLAB
    zero = jnp.zeros((_SLAB, _GATE_LANES), BF16)

    def gate_stage(slot, blk):
        for lc in range(tm // _GATE_LANES):
            lanes = slice(lc * _GATE_LANES, (lc + 1) * _GATE_LANES)
            for ii in range(n_i):
                i = blk * n_i + ii
                g = [None] * n_slab
                for h in range(PEER_HEADS):
                    nb = jnp.broadcast_to(n1_ref[h, pl.ds(i, 1), lanes], (_SLAB, _GATE_LANES)).astype(BF16)
                    cb = jnp.broadcast_to(c1_ref[h, pl.ds(i, 1), lanes], (_SLAB, _GATE_LANES)).astype(BF16)
                    for sl in range(n_slab):
                        rows = slice(sl * _SLAB, (sl + 1) * _SLAB)
                        w = jnp.where(rk2_ref[h, rows, lanes] < nb, e2_ref[h, rows, lanes] * cb, zero)
                        g[sl] = w if g[sl] is None else g[sl] + w
                for sl in range(n_slab):
                    rows = slice(ii * PEER_N_KEYS + sl * _SLAB, ii * PEER_N_KEYS + (sl + 1) * _SLAB)
                    hh = h_ref[slot, rows, lanes]
                    act = 0.5 * hh * (1.0 + lax.erf(hh * (1.0 / math.sqrt(2.0))))
                    a_ref[slot, rows, lanes] = act.astype(BF16) * g[sl]

    def pre_activations(slot):
        h_ref[slot] = jnp.dot(u_ref[slot * te:(slot + 1) * te, :], xn_ref[...], preferred_element_type=F32)

    def block_output(slot, vt_ref):
        return jnp.dot(vt_ref[...], a_ref[slot], preferred_element_type=F32)

    @pl.when(k == 0)
    def _():
        pre_activations(0)
        gate_stage(0, 0)
        pre_activations(1)
        acc_ref[...] = block_output(0, vtb_ref)

    @pl.when((k > 0) & (k < nk))
    def _():
        gate_stage(1, 2 * k - 1)
        pre_activations(0)
        acc_ref[...] += block_output(1, vta_ref)
        gate_stage(0, 2 * k)
        pre_activations(1)
        acc_ref[...] += block_output(0, vtb_ref)

    @pl.when(k == nk)
    def _():
        gate_stage(1, 2 * k - 1)
        o_ref[...] = res_ref[...] + (acc_ref[...] + block_output(1, vta_ref)).T


def peer_main(xn_t, u, vt, rk2, e2, n1, c1, res, layer, tm=512):
    d, t = xn_t.shape
    te = vt.shape[-1]
    nk = u.shape[1] // (2 * te)
    nh = PEER_HEADS
    once = pl.Buffered(1)
    stat = pl.BlockSpec((nh, PEER_N_KEYS, tm), lambda i, k: (0, 0, i), pipeline_mode=once)
    return pl.pallas_call(
        _peer_main_body,
        grid=(t // tm, nk + 1),
        in_specs=[
            pl.BlockSpec((d, tm), lambda i, k: (0, i), pipeline_mode=once),
            pl.BlockSpec((None, 2 * te, d), lambda i, k: (layer, jnp.minimum(k, nk - 1), 0)),
            pl.BlockSpec((None, None, d, te), lambda i, k: (layer, jnp.clip(2 * k - 1, 0, 2 * nk - 1), 0, 0)),
            pl.BlockSpec((None, None, d, te), lambda i, k: (layer, jnp.minimum(2 * k, 2 * nk - 1), 0, 0)),
            stat, stat, stat, stat,
            pl.BlockSpec((tm, d), lambda i, k: (i, 0), pipeline_mode=once),
        ],
        out_specs=pl.BlockSpec((tm, d), lambda i, k: (i, 0)),
        out_shape=jax.ShapeDtypeStruct((t, d), F32),
        scratch_shapes=[pltpu.VMEM((d, tm), F32), pltpu.VMEM((2, te, tm), F32), pltpu.VMEM((2, te, tm), BF16)],
        compiler_params=_cparams(("parallel", "arbitrary")),
        name="peer_main",
    )(xn_t, u, vt, vt, rk2, e2, n1, c1, res)


PEER_BLOCK = 512


def peer_weights(w_q, sub_keys, u, v):
    n_layers, n_e, d = v.shape
    wq_t = jnp.swapaxes(w_q, 1, 2).astype(BF16)
    vt = jnp.swapaxes(v.reshape(n_layers, n_e // PEER_BLOCK, PEER_BLOCK, d), 2, 3).astype(BF16)
    return wq_t, sub_keys.astype(BF16), u.astype(BF16), vt


def peer_ffn(x, gain, weights, layer):
    wq_t, sk, u, vt = weights
    xn_t, rk2, e2, n1, c1 = peer_prep(x, gain, wq_t, sk, layer)
    return peer_main(xn_t, u, vt, rk2, e2, n1, c1, x, layer)


def _row128(v):
    return v.reshape(1, LANES).astype(F32)


def mla_mixer(x, gain, positions, w_in, g_qa, w_qb, g_kva, w_kvb, g_q, g_k, w_o, b, s):
    t = b * s
    nh = MLA_HEADS
    w_in_p = jnp.pad(w_in, ((0, 0), (0, 64))).astype(BF16)
    z = norm_matmul(x, w_in_p, gain=gain, tn=384, name="mla_in")
    wq = w_qb.reshape(MLA_Q_RANK, nh, MLA_QK)
    wq = jnp.concatenate([wq[:, :, :MLA_NOPE].reshape(MLA_Q_RANK, -1), wq[:, :, MLA_NOPE:].reshape(MLA_Q_RANK, -1)], 1)
    qraw = norm_matmul(z, wq.astype(BF16), gain=g_qa, a_col_block=0, out_dtype=BF16, name="mla_qb")
    kvraw = norm_matmul(z, w_kvb.astype(BF16), gain=g_kva, a_col_block=1, out_dtype=BF16, name="mla_kvb")

    scale = 1.0 / math.sqrt(MLA_QK)
    inv_freq = ROPE_THETA ** (-jnp.arange(0, MLA_ROPE, 2, dtype=F32) / MLA_ROPE)
    sign = jnp.where((jnp.arange(LANES) % 64) < 32, -1.0, 1.0)
    rows = [
        _row128(jnp.tile(inv_freq, 4)),
        _row128(sign),
        _row128(g_q[:MLA_NOPE] * scale),
        _row128(jnp.tile(g_q[MLA_NOPE:], 2) * scale),
        _row128(g_k[:MLA_NOPE]),
        _row128(jnp.pad(g_k[MLA_NOPE:], (0, 64))),
    ]
    pos_col = positions.astype(F32).reshape(t, 1)
    qf, kf, vf = mla_prep(qraw, kvraw, z, pos_col, rows, b, s)
    o = attention(qf, kf, vf, fox=False)
    return norm_matmul(o, w_o.astype(BF16), residual=x, name="mla_out")


def fox_mixer(x, gain, w_in, b_f, g_q, g_k, w_o, b, s):
    nh = FOX_HEADS
    d = x.shape[1]
    zmain = norm_matmul(x, w_in[:, :4 * d].astype(BF16), gain=gain, tm=1024, out_dtype=BF16, name="fox_in")
    w_f = jnp.pad(w_in[:, 4 * d:], ((0, 0), (0, LANES - nh))).astype(BF16)
    flog = norm_matmul(x, w_f, gain=gain, tm=1024, name="fox_f")
    scale = 1.0 / math.sqrt(FOX_HEAD_DIM)
    rows = [_row128(jnp.pad(b_f, (0, LANES - nh))), _row128(g_q * scale), _row128(g_k)]
    qf, kf, vf, ct = fox_prep(zmain, flog, rows, b, s)
    crow = ct[:, :nh, :]
    o = attention(qf, kf, vf, fox=True, ccol=crow[..., None], crow=crow, gate=zmain, gate_col0=3 * nh)
    return norm_matmul(o, w_o.astype(BF16), residual=x, name="fox_out")


def kernel(x, positions, norm_mix_g, norm_ffn_g, mla_w_in, mla_g_qa, mla_w_qb, mla_g_kva, mla_w_kvb, mla_g_q, mla_g_k, mla_w_o, fox_w_in, fox_b_f, fox_g_q, fox_g_k, fox_w_o, peer_w_q, peer_sub_keys, peer_u, peer_v):
    b, s, d = x.shape
    depth = norm_mix_g.shape[0]
    xt = x.reshape(b * s, d)
    peer_w = peer_weights(peer_w_q, peer_sub_keys, peer_u, peer_v)
    for i in range(depth):
        j = i // 2
        if i % 2 == 0:
            xt = mla_mixer(xt, norm_mix_g[i], positions, mla_w_in[j], mla_g_qa[j], mla_w_qb[j], mla_g_kva[j],
                           mla_w_kvb[j], mla_g_q[j], mla_g_k[j], mla_w_o[j], b, s)
        else:
            xt = fox_mixer(xt, norm_mix_g[i], fox_w_in[j], fox_b_f[j], fox_g_q[j], fox_g_k[j], fox_w_o[j], b, s)
        xt = peer_ffn(xt, norm_ffn_g[i], peer_w, i)
    return xt.reshape(b, s, d)
```

```python
import functools
import math

import jax
import jax.numpy as jnp
from jax import lax
from jax.experimental import pallas as pl
from jax.experimental.pallas import tpu as pltpu

F32 = jnp.float32
BF16 = jnp.bfloat16

RMS_EPS = 1e-6
NEG_INF = -1e30
LOG2_E = math.log2(math.e)
CHUNK_SHIFT = 6
ROPE_THETA = 10000.0

MLA_HEADS = 16
MLA_Q_RANK = 512
MLA_KV_RANK = 512
MLA_NOPE = 128
MLA_ROPE = 64
MLA_V = 128
MLA_QK = MLA_NOPE + MLA_ROPE

FOX_HEADS = 16
FOX_HEAD_DIM = 128

PEER_HEADS = 8
PEER_N_KEYS = 128
PEER_D_HALF = 128
PEER_TOPK = 16

LANES = 128
VMEM_LIMIT = 56 * 1024 * 1024


def _cparams(sem, flags=None):
    return pltpu.CompilerParams(dimension_semantics=sem, vmem_limit_bytes=VMEM_LIMIT, flags=flags)


def _nm_body(*refs, norm, has_res):
    it = iter(refs)
    a_ref = next(it)
    g_ref = next(it) if norm else None
    w_ref = next(it)
    r_ref = next(it) if has_res else None
    o_ref = next(it)
    an_ref = next(it) if norm else None

    if norm:
        @pl.when(pl.program_id(1) == 0)
        def _():
            a = a_ref[...].astype(F32)
            y = a * lax.rsqrt(jnp.mean(a * a, axis=-1, keepdims=True) + RMS_EPS)
            an_ref[...] = (y * g_ref[...]).astype(BF16)

        a_bf = an_ref[...]
    else:
        a_bf = a_ref[...]
    acc = jnp.dot(a_bf, w_ref[...], preferred_element_type=F32)
    if has_res:
        acc = acc + r_ref[...]
    o_ref[...] = acc.astype(o_ref.dtype)


def norm_matmul(a, w, *, gain=None, residual=None, a_col_block=0, out_dtype=F32, tm=512, tn=512, name):
    m = a.shape[0]
    k, n = w.shape
    tn = min(tn, n)
    assert m % tm == 0 and n % tn == 0
    norm = gain is not None
    in_specs = [pl.BlockSpec((tm, k), lambda i, j: (i, a_col_block))]
    args = [a]
    if norm:
        in_specs.append(pl.BlockSpec((1, k), lambda i, j: (0, 0)))
        args.append(gain.reshape(1, k).astype(F32))
    in_specs.append(pl.BlockSpec((k, tn), lambda i, j: (0, j)))
    args.append(w)
    if residual is not None:
        in_specs.append(pl.BlockSpec((tm, tn), lambda i, j: (i, j)))
        args.append(residual)
    return pl.pallas_call(
        functools.partial(_nm_body, norm=norm, has_res=residual is not None),
        grid=(m // tm, n // tn),
        in_specs=in_specs,
        out_specs=pl.BlockSpec((tm, tn), lambda i, j: (i, j)),
        out_shape=jax.ShapeDtypeStruct((m, n), out_dtype),
        scratch_shapes=[pltpu.VMEM((tm, k), BF16)] if norm else [],
        compiler_params=_cparams(("parallel", "arbitrary")),
        name=name,
    )(*args)


def _swap_halves(x, first_half):
    return jnp.where(first_half, pltpu.roll(x, 96, 1), pltpu.roll(x, 32, 1))


def _mla_prep_body(q_ref, kv_ref, kr_ref, pos_ref, invf_ref, sign_ref, gqn_ref, gqr_ref, gkn_ref, gkr_ref,
                   qf_ref, kf_ref, vf_ref):
    ts = q_ref.shape[0]
    lane = lax.broadcasted_iota(jnp.int32, (ts, LANES), 1)
    first_half = (lane & 63) < 32
    low64 = lane < 64
    ang = pos_ref[...] * invf_ref[...]
    cosv = jnp.cos(ang)
    sinv = jnp.sin(ang) * sign_ref[...]

    def rot(x):
        return x * cosv + _swap_halves(x, first_half) * sinv

    inv_d = 1.0 / MLA_QK
    zeros = jnp.zeros((ts, LANES), F32)

    kr = kr_ref[...]
    ss_kr = jnp.sum(kr * kr, axis=-1, keepdims=True)
    kr_rot = rot(kr * gkr_ref[...])
    for h in range(MLA_HEADS):
        kn = kv_ref[:, h * 256:h * 256 + 128].astype(F32)
        r = lax.rsqrt((jnp.sum(kn * kn, axis=-1, keepdims=True) + ss_kr) * inv_d + RMS_EPS)
        kf_ref[0, h, :, 0:128] = (kn * r * gkn_ref[...]).astype(BF16)
        kf_ref[0, h, :, 128:256] = (kr_rot * r).astype(BF16)
        vf_ref[0, h] = kv_ref[:, h * 256 + 128:(h + 1) * 256]

    for p in range(MLA_HEADS // 2):
        xr = q_ref[:, 2048 + p * 128:2048 + (p + 1) * 128].astype(F32)
        xr2 = xr * xr
        ss_e = jnp.sum(jnp.where(low64, xr2, 0.0), axis=-1, keepdims=True)
        ss_o = jnp.sum(jnp.where(low64, 0.0, xr2), axis=-1, keepdims=True)
        qn_e = q_ref[:, (2 * p) * 128:(2 * p + 1) * 128].astype(F32)
        qn_o = q_ref[:, (2 * p + 1) * 128:(2 * p + 2) * 128].astype(F32)
        r_e = lax.rsqrt((jnp.sum(qn_e * qn_e, axis=-1, keepdims=True) + ss_e) * inv_d + RMS_EPS)
        r_o = lax.rsqrt((jnp.sum(qn_o * qn_o, axis=-1, keepdims=True) + ss_o) * inv_d + RMS_EPS)
        xr_rot = rot(xr * jnp.where(low64, r_e, r_o) * gqr_ref[...])
        qf_ref[0, 2 * p, :, 0:128] = (qn_e * r_e * gqn_ref[...]).astype(BF16)
        qf_ref[0, 2 * p, :, 128:256] = jnp.where(low64, xr_rot, zeros).astype(BF16)
        qf_ref[0, 2 * p + 1, :, 0:128] = (qn_o * r_o * gqn_ref[...]).astype(BF16)
        qf_ref[0, 2 * p + 1, :, 128:256] = jnp.where(low64, pltpu.roll(xr_rot, 64, 1), zeros).astype(BF16)


def mla_prep(qraw, kvraw, z, pos_col, rows, b, s, ts=256):
    ns = s // ts
    h = MLA_HEADS
    row = lambda bi, si: (bi * ns + si, 0)
    vec = pl.BlockSpec((1, LANES), lambda bi, si: (0, 0))
    head_out = lambda d: pl.BlockSpec((1, h, ts, d), lambda bi, si: (bi, 0, si, 0))
    return pl.pallas_call(
        _mla_prep_body,
        grid=(b, ns),
        in_specs=[
            pl.BlockSpec((ts, qraw.shape[1]), row),
            pl.BlockSpec((ts, kvraw.shape[1]), row),
            pl.BlockSpec((ts, LANES), lambda bi, si: (bi * ns + si, (MLA_Q_RANK + MLA_KV_RANK) // LANES)),
            pl.BlockSpec((ts, 1), row),
        ] + [vec] * 6,
        out_specs=[head_out(256), head_out(256), head_out(128)],
        out_shape=[
            jax.ShapeDtypeStruct((b, h, s, 256), BF16),
            jax.ShapeDtypeStruct((b, h, s, 256), BF16),
            jax.ShapeDtypeStruct((b, h, s, 128), BF16),
        ],
        compiler_params=_cparams(("parallel", "parallel")),
        name="mla_prep",
    )(qraw, kvraw, z, pos_col, *rows)


def _split3(x):
    hi = x.astype(BF16)
    r1 = x - hi.astype(F32)
    mid = r1.astype(BF16)
    lo = (r1 - mid.astype(F32)).astype(BF16)
    return hi, mid, lo


def _fox_prep_body(z_ref, f_ref, bf_ref, gq_ref, gk_ref, qf_ref, kf_ref, vf_ref, ct_ref, carry_ref):
    ts = z_ref.shape[0]
    d = FOX_HEAD_DIM
    nh = FOX_HEADS

    @pl.when(pl.program_id(1) == 0)
    def _():
        carry_ref[...] = jnp.zeros_like(carry_ref)

    x = f_ref[...] + bf_ref[...]
    logf = jnp.minimum(x, 0.0) - jnp.log1p(jnp.exp(-jnp.abs(x)))
    tri = (lax.broadcasted_iota(jnp.int32, (ts, ts), 0) >= lax.broadcasted_iota(jnp.int32, (ts, ts), 1)).astype(BF16)
    hi, mid, lo = _split3(logf)
    local = (jnp.dot(tri, hi, preferred_element_type=F32) + jnp.dot(tri, mid, preferred_element_type=F32)
             + jnp.dot(tri, lo, preferred_element_type=F32))
    c = carry_ref[0:1, :] + local
    carry_ref[0:1, :] = c[ts - 1:ts, :]
    ct_ref[0] = (c * LOG2_E).T

    inv_d = 1.0 / d
    for h in range(nh):
        q = z_ref[:, h * d:(h + 1) * d].astype(F32)
        k = z_ref[:, (nh + h) * d:(nh + h + 1) * d].astype(F32)
        rq = lax.rsqrt(jnp.sum(q * q, axis=-1, keepdims=True) * inv_d + RMS_EPS)
        rk = lax.rsqrt(jnp.sum(k * k, axis=-1, keepdims=True) * inv_d + RMS_EPS)
        qf_ref[0, h] = (q * rq * gq_ref[...]).astype(BF16)
        kf_ref[0, h] = (k * rk * gk_ref[...]).astype(BF16)
        vf_ref[0, h] = z_ref[:, (2 * nh + h) * d:(2 * nh + h + 1) * d]


def fox_prep(zmain, flog, rows, b, s, ts=256):
    ns = s // ts
    h = FOX_HEADS
    row = lambda bi, si: (bi * ns + si, 0)
    vec = pl.BlockSpec((1, LANES), lambda bi, si: (0, 0))
    head_out = pl.BlockSpec((1, h, ts, 128), lambda bi, si: (bi, 0, si, 0))
    return pl.pallas_call(
        _fox_prep_body,
        grid=(b, ns),
        in_specs=[pl.BlockSpec((ts, zmain.shape[1]), row), pl.BlockSpec((ts, LANES), row), vec, vec, vec],
        out_specs=[head_out, head_out, head_out, pl.BlockSpec((1, LANES, ts), lambda bi, si: (bi, 0, si))],
        out_shape=[jax.ShapeDtypeStruct((b, h, s, 128), BF16)] * 3 + [jax.ShapeDtypeStruct((b, LANES, s), F32)],
        scratch_shapes=[pltpu.VMEM((8, LANES), F32)],
        compiler_params=_cparams(("parallel", "arbitrary")),
        name="fox_prep",
    )(zmain, flog, *rows)


_ATTN_HEADS = 2


def _attn_body(*refs, fox, tq, nq):
    if fox:
        q_ref, k_ref, v_ref, ccol_ref, crow_ref, gate_ref, o_ref, s_ref, p_ref = refs
    else:
        q_ref, k_ref, v_ref, o_ref, s_ref, p_ref = refs
    nt = (((1,), (1,)), ((), ()))
    half = tq // 2
    dv = v_ref.shape[-1]
    r = lax.broadcasted_iota(jnp.int32, (tq, tq), 0)
    c = lax.broadcasted_iota(jnp.int32, (tq, tq), 1)
    allowed = (c <= r) if fox else ((c >> CHUNK_SHIFT) <= (r >> CHUNK_SHIFT))
    for i in range(nq):
        for hh in range(_ATTN_HEADS):
            buf = 2 * hh + i % 2
            rows = slice(i * tq, (i + 1) * tq)
            q = q_ref[0, hh, rows, :]
            mx = jnp.full((tq, half), NEG_INF, F32)
            for j in range(i + 1):
                cols = slice(j * tq, (j + 1) * tq)
                s = lax.dot_general(q, k_ref[0, hh, cols, :], nt, preferred_element_type=F32)
                if fox:
                    s = s + (ccol_ref[0, hh, rows, :] - crow_ref[0, hh, :, cols])
                if j == i:
                    s = jnp.where(allowed, s, NEG_INF)
                s_ref[buf, :, cols] = s
                mx = jnp.maximum(mx, jnp.maximum(s[:, :half], s[:, half:]))
            m = jnp.max(mx, axis=-1, keepdims=True)
            ps = jnp.zeros((tq, half), F32)
            for j in range(i + 1):
                cols = slice(j * tq, (j + 1) * tq)
                p = jnp.exp2(s_ref[buf, :, cols] - m)
                ps = ps + (p[:, :half] + p[:, half:])
                p_ref[buf, :, cols] = p.astype(BF16)
            l = jnp.sum(ps, axis=-1, keepdims=True)
            n_keys = (i + 1) * tq
            o = jnp.dot(p_ref[buf, :, :n_keys], v_ref[0, hh, :n_keys, :], preferred_element_type=F32) / l
            out_cols = slice(hh * dv, (hh + 1) * dv)
            if fox:
                o = o * (1.0 / (1.0 + jnp.exp(-gate_ref[rows, out_cols].astype(F32))))
            o_ref[rows, out_cols] = o.astype(o_ref.dtype)


def attention(qf, kf, vf, *, fox, ccol=None, crow=None, gate=None, gate_col0=0, tq=256):
    b, h, s, dk = qf.shape
    dv = vf.shape[-1]
    nq = s // tq
    hps = _ATTN_HEADS
    head = lambda d: pl.BlockSpec((1, hps, s, d), lambda bi, hi: (bi, hi, 0, 0))
    in_specs = [head(dk), head(dk), head(dv)]
    args = [qf, kf, vf]
    if fox:
        in_specs += [
            head(1),
            pl.BlockSpec((1, hps, 1, s), lambda bi, hi: (bi, hi, 0, 0)),
            pl.BlockSpec((s, hps * dv), lambda bi, hi: (bi, gate_col0 // hps + hi)),
        ]
        args += [ccol, crow.reshape(b, h, 1, s), gate]
    return pl.pallas_call(
        functools.partial(_attn_body, fox=fox, tq=tq, nq=nq),
        grid=(b, h // hps),
        in_specs=in_specs,
        out_specs=pl.BlockSpec((s, hps * dv), lambda bi, hi: (bi, hi)),
        out_shape=jax.ShapeDtypeStruct((b * s, h * dv), BF16),
        scratch_shapes=[pltpu.VMEM((2 * hps, tq, s), F32), pltpu.VMEM((2 * hps, tq, s), BF16)],
        compiler_params=_cparams(("parallel", "parallel")),
        name="fox_attn" if fox else "mla_attn",
    )(*args)


_CAND_PAIRS = [(p, q) for p in range(PEER_TOPK) for q in range(PEER_TOPK) if (p + 1) * (q + 1) <= PEER_TOPK]
_CAND_ROWS = 56


def _top_rows(v, n, want_rank=False):
    rank = jnp.full(v.shape, float(n), F32)
    out = []
    for r in range(n):
        m = jnp.max(v, axis=0, keepdims=True)
        out.append(m)
        hit = v == m
        if want_rank:
            rank = jnp.where(hit, float(r), rank)
        if r + 1 < n:
            v = jnp.where(hit, -jnp.inf, v)
    return (out, rank) if want_rank else out


def _peer_prep_body(x_ref, g_ref, wq_ref, sk_ref, xn_ref, rk2_ref, e2_ref, n1_ref, c1_ref, s_ref, cand_ref):
    @pl.when(pl.program_id(1) == 0)
    def _():
        a = x_ref[...]
        y = a * lax.rsqrt(jnp.mean(a * a, axis=-1, keepdims=True) + RMS_EPS)
        xn_ref[...] = (y * g_ref[...]).T.astype(BF16)

    qt = jnp.dot(wq_ref[...], xn_ref[...], preferred_element_type=F32)
    s_ref[0] = jnp.dot(sk_ref[0], qt[0:128].astype(BF16), preferred_element_type=F32)
    s_ref[1] = jnp.dot(sk_ref[1], qt[128:256].astype(BF16), preferred_element_type=F32)
    cand_ref[...] = jnp.full(cand_ref.shape, -jnp.inf, F32)

    def lane_group(c, carry):
        lanes = pl.ds(pl.multiple_of(c * LANES, LANES), LANES)
        s1 = s_ref[0, :, lanes]
        s2 = s_ref[1, :, lanes]
        a1 = _top_rows(s1, PEER_TOPK)
        a2, rank2 = _top_rows(s2, PEER_TOPK, want_rank=True)
        for r, (p, q) in enumerate(_CAND_PAIRS):
            cand_ref[r:r + 1, :] = a1[p] + a2[q]
        best = _top_rows(cand_ref[...], PEER_TOPK)
        tau = best[PEER_TOPK - 1]
        z = jnp.ones_like(tau)
        for r in range(1, PEER_TOPK):
            z = z + jnp.exp(best[r] - best[0])
        n1 = jnp.zeros(s1.shape, F32)
        for p in range(PEER_TOPK):
            n_p = jnp.zeros_like(tau)
            for q in range(PEER_TOPK // (p + 1)):
                n_p = n_p + jnp.where(a1[p] + a2[q] >= tau, 1.0, 0.0)
            n1 = jnp.where(s1 == a1[p], n_p, n1)
        rk2_ref[0, :, lanes] = rank2.astype(BF16)
        e2_ref[0, :, lanes] = jnp.exp(s2 - a2[0]).astype(BF16)
        n1_ref[0, :, lanes] = n1
        c1_ref[0, :, lanes] = jnp.exp(s1 - a1[0]) / z
        return carry

    lax.fori_loop(0, s_ref.shape[-1] // LANES, lane_group, 0, unroll=2)


def peer_prep(x, gain, wq_t, sk, layer, tm=512):
    t, d = x.shape
    nh = PEER_HEADS
    stat = pl.BlockSpec((1, PEER_N_KEYS, tm), lambda i, h: (h, 0, i))
    stat_f32 = jax.ShapeDtypeStruct((nh, PEER_N_KEYS, t), F32)
    stat_bf16 = jax.ShapeDtypeStruct((nh, PEER_N_KEYS, t), BF16)
    return pl.pallas_call(
        _peer_prep_body,
        grid=(t // tm, nh),
        in_specs=[
            pl.BlockSpec((tm, d), lambda i, h: (i, 0)),
            pl.BlockSpec((1, d), lambda i, h: (0, 0)),
            pl.BlockSpec((None, 2 * PEER_D_HALF, d), lambda i, h: (layer, h, 0)),
            pl.BlockSpec((None, 2, PEER_N_KEYS, PEER_D_HALF), lambda i, h: (layer, 0, 0, 0)),
        ],
        out_specs=[pl.BlockSpec((d, tm), lambda i, h: (0, i)), stat, stat, stat, stat],
        out_shape=[jax.ShapeDtypeStruct((d, t), BF16), stat_bf16, stat_bf16, stat_f32, stat_f32],
        scratch_shapes=[pltpu.VMEM((2, PEER_N_KEYS, tm), F32), pltpu.VMEM((_CAND_ROWS, LANES), F32)],
        compiler_params=_cparams(("parallel", "arbitrary")),
        name="peer_prep",
    )(x, gain.reshape(1, d).astype(F32), wq_t, sk)


_SLAB = 16
_GATE_LANES = 256


def _peer_main_body(xn_ref, u_ref, vta_ref, vtb_ref, rk2_ref, e2_ref, n1_ref, c1_ref, res_ref, o_ref,
                    acc_ref, h_ref, a_ref):
    k = pl.program_id(1)
    nk = pl.num_programs(1) - 1
    te, tm = h_ref.shape[1:]
    n_i = te // PEER_N_KEYS
    n_slab = PEER_N_KEYS // _SLAB
    zero = jnp.zeros((_SLAB, _GATE_LANES), BF16)

    def gate_stage(slot, blk):
        for lc in range(tm // _GATE_LANES):
            lanes = slice(lc * _GATE_LANES, (lc + 1) * _GATE_LANES)
            for ii in range(n_i):
                i = blk * n_i + ii
                g = [None] * n_slab
                for h in range(PEER_HEADS):
                    nb = jnp.broadcast_to(n1_ref[h, pl.ds(i, 1), lanes], (_SLAB, _GATE_LANES)).astype(BF16)
                    cb = jnp.broadcast_to(c1_ref[h, pl.ds(i, 1), lanes], (_SLAB, _GATE_LANES)).astype(BF16)
                    for sl in range(n_slab):
                        rows = slice(sl * _SLAB, (sl + 1) * _SLAB)
                        w = jnp.where(rk2_ref[h, rows, lanes] < nb, e2_ref[h, rows, lanes] * cb, zero)
                        g[sl] = w if g[sl] is None else g[sl] + w
                for sl in range(n_slab):
                    rows = slice(ii * PEER_N_KEYS + sl * _SLAB, ii * PEER_N_KEYS + (sl + 1) * _SLAB)
                    hh = h_ref[slot, rows, lanes]
                    act = 0.5 * hh * (1.0 + lax.erf(hh * (1.0 / math.sqrt(2.0))))
                    a_ref[slot, rows, lanes] = act.astype(BF16) * g[sl]

    def pre_activations(slot):
        h_ref[slot] = jnp.dot(u_ref[slot * te:(slot + 1) * te, :], xn_ref[...], preferred_element_type=F32)

    def block_output(slot, vt_ref):
        return jnp.dot(vt_ref[...], a_ref[slot], preferred_element_type=F32)

    @pl.when(k == 0)
    def _():
        pre_activations(0)
        gate_stage(0, 0)
        pre_activations(1)
        acc_ref[...] = block_output(0, vtb_ref)

    @pl.when((k > 0) & (k < nk))
    def _():
        gate_stage(1, 2 * k - 1)
        pre_activations(0)
        acc_ref[...] += block_output(1, vta_ref)
        gate_stage(0, 2 * k)
        pre_activations(1)
        acc_ref[...] += block_output(0, vtb_ref)

    @pl.when(k == nk)
    def _():
        gate_stage(1, 2 * k - 1)
        o_ref[...] = res_ref[...] + (acc_ref[...] + block_output(1, vta_ref)).T


def peer_main(xn_t, u, vt, rk2, e2, n1, c1, res, layer, tm=512):
    d, t = xn_t.shape
    te = vt.shape[-1]
    nk = u.shape[1] // (2 * te)
    nh = PEER_HEADS
    once = pl.Buffered(1)
    stat = pl.BlockSpec((nh, PEER_N_KEYS, tm), lambda i, k: (0, 0, i), pipeline_mode=once)
    return pl.pallas_call(
        _peer_main_body,
        grid=(t // tm, nk + 1),
        in_specs=[
            pl.BlockSpec((d, tm), lambda i, k: (0, i), pipeline_mode=once),
            pl.BlockSpec((None, 2 * te, d), lambda i, k: (layer, jnp.minimum(k, nk - 1), 0)),
            pl.BlockSpec((None, None, d, te), lambda i, k: (layer, jnp.clip(2 * k - 1, 0, 2 * nk - 1), 0, 0)),
            pl.BlockSpec((None, None, d, te), lambda i, k: (layer, jnp.minimum(2 * k, 2 * nk - 1), 0, 0)),
            stat, stat, stat, stat,
            pl.BlockSpec((tm, d), lambda i, k: (i, 0), pipeline_mode=once),
        ],
        out_specs=pl.BlockSpec((tm, d), lambda i, k: (i, 0)),
        out_shape=jax.ShapeDtypeStruct((t, d), F32),
        scratch_shapes=[pltpu.VMEM((d, tm), F32), pltpu.VMEM((2, te, tm), F32), pltpu.VMEM((2, te, tm), BF16)],
        compiler_params=_cparams(("parallel", "arbitrary")),
        name="peer_main",
    )(xn_t, u, vt, vt, rk2, e2, n1, c1, res)


PEER_BLOCK = 512


def peer_weights(w_q, sub_keys, u, v):
    n_layers, n_e, d = v.shape
    wq_t = jnp.swapaxes(w_q, 1, 2).astype(BF16)
    vt = jnp.swapaxes(v.reshape(n_layers, n_e // PEER_BLOCK, PEER_BLOCK, d), 2, 3).astype(BF16)
    return wq_t, sub_keys.astype(BF16), u.astype(BF16), vt


def peer_ffn(x, gain, weights, layer):
    wq_t, sk, u, vt = weights
    xn_t, rk2, e2, n1, c1 = peer_prep(x, gain, wq_t, sk, layer)
    return peer_main(xn_t, u, vt, rk2, e2, n1, c1, x, layer)


def _row128(v):
    return v.reshape(1, LANES).astype(F32)


def mla_mixer(x, gain, positions, w_in, g_qa, w_qb, g_kva, w_kvb, g_q, g_k, w_o, b, s):
    t = b * s
    nh = MLA_HEADS
    w_in_p = jnp.pad(w_in, ((0, 0), (0, 64))).astype(BF16)
    z = norm_matmul(x, w_in_p, gain=gain, tn=w_in_p.shape[1], name="mla_in")
    wq = w_qb.reshape(MLA_Q_RANK, nh, MLA_QK)
    wq = jnp.concatenate([wq[:, :, :MLA_NOPE].reshape(MLA_Q_RANK, -1), wq[:, :, MLA_NOPE:].reshape(MLA_Q_RANK, -1)], 1)
    qraw = norm_matmul(z, wq.astype(BF16), gain=g_qa, a_col_block=0, out_dtype=BF16, tn=wq.shape[1], name="mla_qb")
    kvraw = norm_matmul(z, w_kvb.astype(BF16), gain=g_kva, a_col_block=1, out_dtype=BF16, tn=w_kvb.shape[1],
                        name="mla_kvb")

    scale = LOG2_E / math.sqrt(MLA_QK)
    inv_freq = ROPE_THETA ** (-jnp.arange(0, MLA_ROPE, 2, dtype=F32) / MLA_ROPE)
    sign = jnp.where((jnp.arange(LANES) % 64) < 32, -1.0, 1.0)
    rows = [
        _row128(jnp.tile(inv_freq, 4)),
        _row128(sign),
        _row128(g_q[:MLA_NOPE] * scale),
        _row128(jnp.tile(g_q[MLA_NOPE:], 2) * scale),
        _row128(g_k[:MLA_NOPE]),
        _row128(jnp.pad(g_k[MLA_NOPE:], (0, 64))),
    ]
    pos_col = positions.astype(F32).reshape(t, 1)
    qf, kf, vf = mla_prep(qraw, kvraw, z, pos_col, rows, b, s)
    o = attention(qf, kf, vf, fox=False)
    return norm_matmul(o, w_o.astype(BF16), residual=x, tn=w_o.shape[1], name="mla_out")


def fox_mixer(x, gain, w_in, b_f, g_q, g_k, w_o, b, s):
    nh = FOX_HEADS
    d = x.shape[1]
    zmain = norm_matmul(x, w_in[:, :4 * d].astype(BF16), gain=gain, tm=1024, out_dtype=BF16, name="fox_in")
    w_f = jnp.pad(w_in[:, 4 * d:], ((0, 0), (0, LANES - nh))).astype(BF16)
    flog = norm_matmul(x, w_f, gain=gain, tm=1024, name="fox_f")
    scale = LOG2_E / math.sqrt(FOX_HEAD_DIM)
    rows = [_row128(jnp.pad(b_f, (0, LANES - nh))), _row128(g_q * scale), _row128(g_k)]
    qf, kf, vf, ct = fox_prep(zmain, flog, rows, b, s)
    crow = ct[:, :nh, :]
    o = attention(qf, kf, vf, fox=True, ccol=crow[..., None], crow=crow, gate=zmain, gate_col0=3 * nh)
    return norm_matmul(o, w_o.astype(BF16), residual=x, tn=w_o.shape[1], name="fox_out")


def kernel(x, positions, norm_mix_g, norm_ffn_g, mla_w_in, mla_g_qa, mla_w_qb, mla_g_kva, mla_w_kvb, mla_g_q, mla_g_k, mla_w_o, fox_w_in, fox_b_f, fox_g_q, fox_g_k, fox_w_o, peer_w_q, peer_sub_keys, peer_u, peer_v):
    b, s, d = x.shape
    depth = norm_mix_g.shape[0]
    xt = x.reshape(b * s, d)
    peer_w = peer_weights(peer_w_q, peer_sub_keys, peer_u, peer_v)
    for i in range(depth):
        j = i // 2
        if i % 2 == 0:
            xt = mla_mixer(xt, norm_mix_g[i], positions, mla_w_in[j], mla_g_qa[j], mla_w_qb[j], mla_g_kva[j],
                           mla_w_kvb[j], mla_g_q[j], mla_g_k[j], mla_w_o[j], b, s)
        else:
            xt = fox_mixer(xt, norm_mix_g[i], fox_w_in[j], fox_b_f[j], fox_g_q[j], fox_g_k[j], fox_w_o[j], b, s)
        xt = peer_ffn(xt, norm_ffn_g[i], peer_w, i)
    return xt.reshape(b, s, d)
```

```python
import functools
import math

import jax
import jax.numpy as jnp
from jax import lax
from jax.experimental import pallas as pl
from jax.experimental.pallas import tpu as pltpu

F32 = jnp.float32
BF16 = jnp.bfloat16

RMS_EPS = 1e-6
NEG_INF = -1e30
LOG2_E = math.log2(math.e)
CHUNK_SHIFT = 6
ROPE_THETA = 10000.0

MLA_HEADS = 16
MLA_Q_RANK = 512
MLA_KV_RANK = 512
MLA_NOPE = 128
MLA_ROPE = 64
MLA_V = 128
MLA_QK = MLA_NOPE + MLA_ROPE

FOX_HEADS = 16
FOX_HEAD_DIM = 128

PEER_HEADS = 8
PEER_N_KEYS = 128
PEER_D_HALF = 128
PEER_TOPK = 16

LANES = 128
VMEM_LIMIT = 56 * 1024 * 1024


def _cparams(sem, flags=None):
    return pltpu.CompilerParams(dimension_semantics=sem, vmem_limit_bytes=VMEM_LIMIT, flags=flags)


def _nm_body(*refs, norm, has_res):
    it = iter(refs)
    a_ref = next(it)
    g_ref = next(it) if norm else None
    w_ref = next(it)
    r_ref = next(it) if has_res else None
    o_ref = next(it)
    an_ref = next(it) if norm else None

    if norm:
        @pl.when(pl.program_id(1) == 0)
        def _():
            a = a_ref[...].astype(F32)
            y = a * lax.rsqrt(jnp.mean(a * a, axis=-1, keepdims=True) + RMS_EPS)
            an_ref[...] = (y * g_ref[...]).astype(BF16)

        a_bf = an_ref[...]
    else:
        a_bf = a_ref[...]
    acc = jnp.dot(a_bf, w_ref[...], preferred_element_type=F32)
    if has_res:
        acc = acc + r_ref[...]
    o_ref[...] = acc.astype(o_ref.dtype)


def norm_matmul(a, w, *, gain=None, residual=None, a_col_block=0, n_cols=None, out_dtype=F32, tm=512, tn=512, name):
    m = a.shape[0]
    k = w.shape[0]
    n = w.shape[1] if n_cols is None else n_cols
    tn = min(tn, n)
    assert m % tm == 0 and n % tn == 0
    norm = gain is not None
    in_specs = [pl.BlockSpec((tm, k), lambda i, j: (i, a_col_block))]
    args = [a]
    if norm:
        in_specs.append(pl.BlockSpec((1, k), lambda i, j: (0, 0)))
        args.append(gain.reshape(1, k).astype(F32))
    in_specs.append(pl.BlockSpec((k, tn), lambda i, j: (0, j)))
    args.append(w)
    if residual is not None:
        in_specs.append(pl.BlockSpec((tm, tn), lambda i, j: (i, j)))
        args.append(residual)
    return pl.pallas_call(
        functools.partial(_nm_body, norm=norm, has_res=residual is not None),
        grid=(m // tm, n // tn),
        in_specs=in_specs,
        out_specs=pl.BlockSpec((tm, tn), lambda i, j: (i, j)),
        out_shape=jax.ShapeDtypeStruct((m, n), out_dtype),
        scratch_shapes=[pltpu.VMEM((tm, k), BF16)] if norm else [],
        compiler_params=_cparams(("parallel", "arbitrary")),
        name=name,
    )(*args)


def _swap_halves(x, first_half):
    return jnp.where(first_half, pltpu.roll(x, 96, 1), pltpu.roll(x, 32, 1))


def _mla_prep_body(q_ref, kv_ref, kr_ref, pos_ref, invf_ref, sign_ref, gqn_ref, gqr_ref, gkn_ref, gkr_ref,
                   qf_ref, kf_ref, vf_ref):
    ts = q_ref.shape[0]
    lane = lax.broadcasted_iota(jnp.int32, (ts, LANES), 1)
    first_half = (lane & 63) < 32
    low64 = lane < 64
    ang = pos_ref[...] * invf_ref[...]
    cosv = jnp.cos(ang)
    sinv = jnp.sin(ang) * sign_ref[...]

    def rot(x):
        return x * cosv + _swap_halves(x, first_half) * sinv

    inv_d = 1.0 / MLA_QK
    zeros = jnp.zeros((ts, LANES), F32)

    kr = kr_ref[...]
    ss_kr = jnp.sum(kr * kr, axis=-1, keepdims=True)
    kr_rot = rot(kr * gkr_ref[...])
    for h in range(MLA_HEADS):
        kn = kv_ref[:, h * 256:h * 256 + 128].astype(F32)
        r = lax.rsqrt((jnp.sum(kn * kn, axis=-1, keepdims=True) + ss_kr) * inv_d + RMS_EPS)
        kf_ref[0, h, :, 0:128] = (kn * r * gkn_ref[...]).astype(BF16)
        kf_ref[0, h, :, 128:256] = (kr_rot * r).astype(BF16)
        vf_ref[0, h] = kv_ref[:, h * 256 + 128:(h + 1) * 256]

    for p in range(MLA_HEADS // 2):
        xr = q_ref[:, 2048 + p * 128:2048 + (p + 1) * 128].astype(F32)
        xr2 = xr * xr
        ss_e = jnp.sum(jnp.where(low64, xr2, 0.0), axis=-1, keepdims=True)
        ss_o = jnp.sum(jnp.where(low64, 0.0, xr2), axis=-1, keepdims=True)
        qn_e = q_ref[:, (2 * p) * 128:(2 * p + 1) * 128].astype(F32)
        qn_o = q_ref[:, (2 * p + 1) * 128:(2 * p + 2) * 128].astype(F32)
        r_e = lax.rsqrt((jnp.sum(qn_e * qn_e, axis=-1, keepdims=True) + ss_e) * inv_d + RMS_EPS)
        r_o = lax.rsqrt((jnp.sum(qn_o * qn_o, axis=-1, keepdims=True) + ss_o) * inv_d + RMS_EPS)
        xr_rot = rot(xr * jnp.where(low64, r_e, r_o) * gqr_ref[...])
        qf_ref[0, 2 * p, :, 0:128] = (qn_e * r_e * gqn_ref[...]).astype(BF16)
        qf_ref[0, 2 * p, :, 128:256] = jnp.where(low64, xr_rot, zeros).astype(BF16)
        qf_ref[0, 2 * p + 1, :, 0:128] = (qn_o * r_o * gqn_ref[...]).astype(BF16)
        qf_ref[0, 2 * p + 1, :, 128:256] = jnp.where(low64, pltpu.roll(xr_rot, 64, 1), zeros).astype(BF16)


def mla_prep(qraw, kvraw, z, pos_col, rows, b, s, ts=256):
    ns = s // ts
    h = MLA_HEADS
    row = lambda bi, si: (bi * ns + si, 0)
    vec = pl.BlockSpec((1, LANES), lambda bi, si: (0, 0))
    head_out = lambda d: pl.BlockSpec((1, h, ts, d), lambda bi, si: (bi, 0, si, 0))
    return pl.pallas_call(
        _mla_prep_body,
        grid=(b, ns),
        in_specs=[
            pl.BlockSpec((ts, qraw.shape[1]), row),
            pl.BlockSpec((ts, kvraw.shape[1]), row),
            pl.BlockSpec((ts, LANES), lambda bi, si: (bi * ns + si, (MLA_Q_RANK + MLA_KV_RANK) // LANES)),
            pl.BlockSpec((ts, 1), row),
        ] + [vec] * 6,
        out_specs=[head_out(256), head_out(256), head_out(128)],
        out_shape=[
            jax.ShapeDtypeStruct((b, h, s, 256), BF16),
            jax.ShapeDtypeStruct((b, h, s, 256), BF16),
            jax.ShapeDtypeStruct((b, h, s, 128), BF16),
        ],
        compiler_params=_cparams(("parallel", "parallel")),
        name="mla_prep",
    )(qraw, kvraw, z, pos_col, *rows)


def _split3(x):
    hi = x.astype(BF16)
    r1 = x - hi.astype(F32)
    mid = r1.astype(BF16)
    lo = (r1 - mid.astype(F32)).astype(BF16)
    return hi, mid, lo


def _fox_prep_body(z_ref, f_ref, bf_ref, gq_ref, gk_ref, qf_ref, kf_ref, vf_ref, ct_ref, carry_ref):
    ts = z_ref.shape[0]
    d = FOX_HEAD_DIM
    nh = FOX_HEADS

    @pl.when(pl.program_id(1) == 0)
    def _():
        carry_ref[...] = jnp.zeros_like(carry_ref)

    x = f_ref[...] + bf_ref[...]
    logf = jnp.minimum(x, 0.0) - jnp.log1p(jnp.exp(-jnp.abs(x)))
    tri = (lax.broadcasted_iota(jnp.int32, (ts, ts), 0) >= lax.broadcasted_iota(jnp.int32, (ts, ts), 1)).astype(BF16)
    hi, mid, lo = _split3(logf)
    local = (jnp.dot(tri, hi, preferred_element_type=F32) + jnp.dot(tri, mid, preferred_element_type=F32)
             + jnp.dot(tri, lo, preferred_element_type=F32))
    c = carry_ref[0:1, :] + local
    carry_ref[0:1, :] = c[ts - 1:ts, :]
    ct_ref[0] = (c * LOG2_E).T

    inv_d = 1.0 / d
    for h in range(nh):
        q = z_ref[:, h * d:(h + 1) * d].astype(F32)
        k = z_ref[:, (nh + h) * d:(nh + h + 1) * d].astype(F32)
        rq = lax.rsqrt(jnp.sum(q * q, axis=-1, keepdims=True) * inv_d + RMS_EPS)
        rk = lax.rsqrt(jnp.sum(k * k, axis=-1, keepdims=True) * inv_d + RMS_EPS)
        qf_ref[0, h] = (q * rq * gq_ref[...]).astype(BF16)
        kf_ref[0, h] = (k * rk * gk_ref[...]).astype(BF16)
        vf_ref[0, h] = z_ref[:, (2 * nh + h) * d:(2 * nh + h + 1) * d]


def fox_prep(zmain, flog, rows, b, s, ts=256):
    ns = s // ts
    h = FOX_HEADS
    row = lambda bi, si: (bi * ns + si, 0)
    vec = pl.BlockSpec((1, LANES), lambda bi, si: (0, 0))
    head_out = pl.BlockSpec((1, h, ts, 128), lambda bi, si: (bi, 0, si, 0))
    return pl.pallas_call(
        _fox_prep_body,
        grid=(b, ns),
        in_specs=[pl.BlockSpec((ts, zmain.shape[1]), row), pl.BlockSpec((ts, LANES), row), vec, vec, vec],
        out_specs=[head_out, head_out, head_out, pl.BlockSpec((1, LANES, ts), lambda bi, si: (bi, 0, si))],
        out_shape=[jax.ShapeDtypeStruct((b, h, s, 128), BF16)] * 3 + [jax.ShapeDtypeStruct((b, LANES, s), F32)],
        scratch_shapes=[pltpu.VMEM((8, LANES), F32)],
        compiler_params=_cparams(("parallel", "arbitrary")),
        name="fox_prep",
    )(zmain, flog, *rows)


_ATTN_HEADS = 2


def _attn_body(*refs, fox, tq, nq):
    if fox:
        q_ref, k_ref, v_ref, ccol_ref, crow_ref, gate_ref, o_ref, s_ref, p_ref = refs
    else:
        q_ref, k_ref, v_ref, o_ref, s_ref, p_ref = refs
    nt = (((1,), (1,)), ((), ()))
    half = tq // 2
    dv = v_ref.shape[-1]
    r = lax.broadcasted_iota(jnp.int32, (tq, tq), 0)
    c = lax.broadcasted_iota(jnp.int32, (tq, tq), 1)
    allowed = (c <= r) if fox else ((c >> CHUNK_SHIFT) <= (r >> CHUNK_SHIFT))
    for i in range(nq):
        for hh in range(_ATTN_HEADS):
            buf = 2 * hh + i % 2
            rows = slice(i * tq, (i + 1) * tq)
            q = q_ref[0, hh, rows, :]
            mx = jnp.full((tq, half), NEG_INF, F32)
            for j in range(i + 1):
                cols = slice(j * tq, (j + 1) * tq)
                s = lax.dot_general(q, k_ref[0, hh, cols, :], nt, preferred_element_type=F32)
                if fox:
                    s = s + (ccol_ref[0, hh, rows, :] - crow_ref[0, hh, :, cols])
                if j == i:
                    s = jnp.where(allowed, s, NEG_INF)
                s_ref[buf, :, cols] = s
                mx = jnp.maximum(mx, jnp.maximum(s[:, :half], s[:, half:]))
            m = jnp.max(mx, axis=-1, keepdims=True)
            ps = jnp.zeros((tq, half), F32)
            for j in range(i + 1):
                cols = slice(j * tq, (j + 1) * tq)
                p = jnp.exp2(s_ref[buf, :, cols] - m)
                ps = ps + (p[:, :half] + p[:, half:])
                p_ref[buf, :, cols] = p.astype(BF16)
            l = jnp.sum(ps, axis=-1, keepdims=True)
            n_keys = (i + 1) * tq
            o = jnp.dot(p_ref[buf, :, :n_keys], v_ref[0, hh, :n_keys, :], preferred_element_type=F32) / l
            out_cols = slice(hh * dv, (hh + 1) * dv)
            if fox:
                o = o * (1.0 / (1.0 + jnp.exp(-gate_ref[rows, out_cols].astype(F32))))
            o_ref[rows, out_cols] = o.astype(o_ref.dtype)


def attention(qf, kf, vf, *, fox, ccol=None, crow=None, gate=None, gate_col0=0, tq=256):
    b, h, s, dk = qf.shape
    dv = vf.shape[-1]
    nq = s // tq
    hps = _ATTN_HEADS
    head = lambda d: pl.BlockSpec((1, hps, s, d), lambda bi, hi: (bi, hi, 0, 0))
    in_specs = [head(dk), head(dk), head(dv)]
    args = [qf, kf, vf]
    if fox:
        in_specs += [
            head(1),
            pl.BlockSpec((1, hps, 1, s), lambda bi, hi: (bi, hi, 0, 0)),
            pl.BlockSpec((s, hps * dv), lambda bi, hi: (bi, gate_col0 // hps + hi)),
        ]
        args += [ccol, crow.reshape(b, h, 1, s), gate]
    return pl.pallas_call(
        functools.partial(_attn_body, fox=fox, tq=tq, nq=nq),
        grid=(b, h // hps),
        in_specs=in_specs,
        out_specs=pl.BlockSpec((s, hps * dv), lambda bi, hi: (bi, hi)),
        out_shape=jax.ShapeDtypeStruct((b * s, h * dv), BF16),
        scratch_shapes=[pltpu.VMEM((2 * hps, tq, s), F32), pltpu.VMEM((2 * hps, tq, s), BF16)],
        compiler_params=_cparams(("parallel", "parallel")),
        name="fox_attn" if fox else "mla_attn",
    )(*args)


_CAND_PAIRS = [(p, q) for p in range(PEER_TOPK) for q in range(PEER_TOPK) if (p + 1) * (q + 1) <= PEER_TOPK]
_CAND_ROWS = 56


def _top_rows(v, n, want_rank=False):
    rank = jnp.full(v.shape, float(n), F32)
    out = []
    for r in range(n):
        m = jnp.max(v, axis=0, keepdims=True)
        out.append(m)
        hit = v == m
        if want_rank:
            rank = jnp.where(hit, float(r), rank)
        if r + 1 < n:
            v = jnp.where(hit, -jnp.inf, v)
    return (out, rank) if want_rank else out


def _peer_prep_body(x_ref, g_ref, wq_ref, sk_ref, xn_ref, rk2_ref, e2_ref, n1_ref, c1_ref, s_ref, cand_ref):
    @pl.when(pl.program_id(1) == 0)
    def _():
        a = x_ref[...]
        y = a * lax.rsqrt(jnp.mean(a * a, axis=-1, keepdims=True) + RMS_EPS)
        xn_ref[...] = (y * g_ref[...]).T.astype(BF16)

    qt = jnp.dot(wq_ref[...], xn_ref[...], preferred_element_type=F32)
    s_ref[0] = jnp.dot(sk_ref[0], qt[0:128].astype(BF16), preferred_element_type=F32)
    s_ref[1] = jnp.dot(sk_ref[1], qt[128:256].astype(BF16), preferred_element_type=F32)
    cand_ref[...] = jnp.full(cand_ref.shape, -jnp.inf, F32)

    def lane_group(c, carry):
        lanes = pl.ds(pl.multiple_of(c * LANES, LANES), LANES)
        s1 = s_ref[0, :, lanes]
        s2 = s_ref[1, :, lanes]
        a1 = _top_rows(s1, PEER_TOPK)
        a2, rank2 = _top_rows(s2, PEER_TOPK, want_rank=True)
        for r, (p, q) in enumerate(_CAND_PAIRS):
            cand_ref[r:r + 1, :] = a1[p] + a2[q]
        best = _top_rows(cand_ref[...], PEER_TOPK)
        tau = best[PEER_TOPK - 1]
        z = jnp.ones_like(tau)
        for r in range(1, PEER_TOPK):
            z = z + jnp.exp(best[r] - best[0])
        n1 = jnp.zeros(s1.shape, F32)
        for p in range(PEER_TOPK):
            n_p = jnp.zeros_like(tau)
            for q in range(PEER_TOPK // (p + 1)):
                n_p = n_p + jnp.where(a1[p] + a2[q] >= tau, 1.0, 0.0)
            n1 = jnp.where(s1 == a1[p], n_p, n1)
        rk2_ref[0, :, lanes] = rank2.astype(BF16)
        e2_ref[0, :, lanes] = jnp.exp(s2 - a2[0]).astype(BF16)
        n1_ref[0, :, lanes] = n1
        c1_ref[0, :, lanes] = jnp.exp(s1 - a1[0]) / z
        return carry

    lax.fori_loop(0, s_ref.shape[-1] // LANES, lane_group, 0, unroll=2)


def peer_prep(x, gain, wq_t, sk, layer, tm=512):
    t, d = x.shape
    nh = PEER_HEADS
    stat = pl.BlockSpec((1, PEER_N_KEYS, tm), lambda i, h: (h, 0, i))
    stat_f32 = jax.ShapeDtypeStruct((nh, PEER_N_KEYS, t), F32)
    stat_bf16 = jax.ShapeDtypeStruct((nh, PEER_N_KEYS, t), BF16)
    return pl.pallas_call(
        _peer_prep_body,
        grid=(t // tm, nh),
        in_specs=[
            pl.BlockSpec((tm, d), lambda i, h: (i, 0)),
            pl.BlockSpec((1, d), lambda i, h: (0, 0)),
            pl.BlockSpec((None, 2 * PEER_D_HALF, d), lambda i, h: (layer, h, 0)),
            pl.BlockSpec((None, 2, PEER_N_KEYS, PEER_D_HALF), lambda i, h: (layer, 0, 0, 0)),
        ],
        out_specs=[pl.BlockSpec((d, tm), lambda i, h: (0, i)), stat, stat, stat, stat],
        out_shape=[jax.ShapeDtypeStruct((d, t), BF16), stat_bf16, stat_bf16, stat_f32, stat_f32],
        scratch_shapes=[pltpu.VMEM((2, PEER_N_KEYS, tm), F32), pltpu.VMEM((_CAND_ROWS, LANES), F32)],
        compiler_params=_cparams(("parallel", "arbitrary")),
        name="peer_prep",
    )(x, gain.reshape(1, d).astype(F32), wq_t, sk)


_SLAB = 16
_GATE_LANES = 256


def _peer_main_body(xn_ref, u_ref, vta_ref, vtb_ref, rk2_ref, e2_ref, n1_ref, c1_ref, res_ref, o_ref,
                    acc_ref, h_ref, a_ref):
    k = pl.program_id(1)
    nk = pl.num_programs(1) - 1
    te, tm = h_ref.shape[1:]
    n_i = te // PEER_N_KEYS
    n_slab = PEER_N_KEYS // _SLAB
    zero = jnp.zeros((_SLAB, _GATE_LANES), BF16)

    def gate_stage(slot, blk):
        for lc in range(tm // _GATE_LANES):
            lanes = slice(lc * _GATE_LANES, (lc + 1) * _GATE_LANES)
            for ii in range(n_i):
                i = blk * n_i + ii
                g = [None] * n_slab
                for h in range(PEER_HEADS):
                    nb = jnp.broadcast_to(n1_ref[h, pl.ds(i, 1), lanes], (_SLAB, _GATE_LANES)).astype(BF16)
                    cb = jnp.broadcast_to(c1_ref[h, pl.ds(i, 1), lanes], (_SLAB, _GATE_LANES)).astype(BF16)
                    for sl in range(n_slab):
                        rows = slice(sl * _SLAB, (sl + 1) * _SLAB)
                        w = jnp.where(rk2_ref[h, rows, lanes] < nb, e2_ref[h, rows, lanes] * cb, zero)
                        g[sl] = w if g[sl] is None else g[sl] + w
                for sl in range(n_slab):
                    rows = slice(ii * PEER_N_KEYS + sl * _SLAB, ii * PEER_N_KEYS + (sl + 1) * _SLAB)
                    hh = h_ref[slot, rows, lanes]
                    act = 0.5 * hh * (1.0 + lax.erf(hh * (1.0 / math.sqrt(2.0))))
                    a_ref[slot, rows, lanes] = act.astype(BF16) * g[sl]

    def pre_activations(slot):
        h_ref[slot] = jnp.dot(u_ref[slot * te:(slot + 1) * te, :], xn_ref[...], preferred_element_type=F32)

    def block_output(slot, vt_ref):
        return jnp.dot(vt_ref[...], a_ref[slot], preferred_element_type=F32)

    @pl.when(k == 0)
    def _():
        pre_activations(0)
        gate_stage(0, 0)
        pre_activations(1)
        acc_ref[...] = block_output(0, vtb_ref)

    @pl.when((k > 0) & (k < nk))
    def _():
        gate_stage(1, 2 * k - 1)
        pre_activations(0)
        acc_ref[...] += block_output(1, vta_ref)
        gate_stage(0, 2 * k)
        pre_activations(1)
        acc_ref[...] += block_output(0, vtb_ref)

    @pl.when(k == nk)
    def _():
        gate_stage(1, 2 * k - 1)
        o_ref[...] = res_ref[...] + (acc_ref[...] + block_output(1, vta_ref)).T


def peer_main(xn_t, u, vt, rk2, e2, n1, c1, res, layer, tm=512):
    d, t = xn_t.shape
    te = vt.shape[-1]
    nk = u.shape[1] // (2 * te)
    nh = PEER_HEADS
    once = pl.Buffered(1)
    stat = pl.BlockSpec((nh, PEER_N_KEYS, tm), lambda i, k: (0, 0, i), pipeline_mode=once)
    return pl.pallas_call(
        _peer_main_body,
        grid=(t // tm, nk + 1),
        in_specs=[
            pl.BlockSpec((d, tm), lambda i, k: (0, i), pipeline_mode=once),
            pl.BlockSpec((None, 2 * te, d), lambda i, k: (layer, jnp.minimum(k, nk - 1), 0)),
            pl.BlockSpec((None, None, d, te), lambda i, k: (layer, jnp.clip(2 * k - 1, 0, 2 * nk - 1), 0, 0)),
            pl.BlockSpec((None, None, d, te), lambda i, k: (layer, jnp.minimum(2 * k, 2 * nk - 1), 0, 0)),
            stat, stat, stat, stat,
            pl.BlockSpec((tm, d), lambda i, k: (i, 0), pipeline_mode=once),
        ],
        out_specs=pl.BlockSpec((tm, d), lambda i, k: (i, 0)),
        out_shape=jax.ShapeDtypeStruct((t, d), F32),
        scratch_shapes=[pltpu.VMEM((d, tm), F32), pltpu.VMEM((2, te, tm), F32), pltpu.VMEM((2, te, tm), BF16)],
        compiler_params=_cparams(("parallel", "arbitrary")),
        name="peer_main",
    )(xn_t, u, vt, vt, rk2, e2, n1, c1, res)


PEER_BLOCK = 512


def _transpose_block_body(v_ref, o_ref):
    o_ref[...] = v_ref[...].T.astype(BF16)


def peer_v_blocks(v):
    n_layers, n_e, d = v.shape
    te = PEER_BLOCK
    return pl.pallas_call(
        _transpose_block_body,
        grid=(n_layers, n_e // te),
        in_specs=[pl.BlockSpec((None, te, d), lambda l, e: (l, e, 0))],
        out_specs=pl.BlockSpec((None, None, d, te), lambda l, e: (l, e, 0, 0)),
        out_shape=jax.ShapeDtypeStruct((n_layers, n_e // te, d, te), BF16),
        compiler_params=_cparams(("parallel", "parallel")),
        name="peer_v_blocks",
    )(v)


def peer_weights(w_q, sub_keys, u, v):
    wq_t = jnp.swapaxes(w_q, 1, 2).astype(BF16)
    return wq_t, sub_keys.astype(BF16), u.astype(BF16), peer_v_blocks(v)


def peer_ffn(x, gain, weights, layer):
    wq_t, sk, u, vt = weights
    xn_t, rk2, e2, n1, c1 = peer_prep(x, gain, wq_t, sk, layer)
    return peer_main(xn_t, u, vt, rk2, e2, n1, c1, x, layer)


def _row128(v):
    return v.reshape(1, LANES).astype(F32)


def mla_mixer(x, gain, positions, w_in, g_qa, w_qb, g_kva, w_kvb, g_q, g_k, w_o, b, s):
    t = b * s
    nh = MLA_HEADS
    w_in_p = jnp.pad(w_in, ((0, 0), (0, 64))).astype(BF16)
    z = norm_matmul(x, w_in_p, gain=gain, tn=w_in_p.shape[1], name="mla_in")
    wq = w_qb.reshape(MLA_Q_RANK, nh, MLA_QK)
    wq = jnp.concatenate([wq[:, :, :MLA_NOPE].reshape(MLA_Q_RANK, -1), wq[:, :, MLA_NOPE:].reshape(MLA_Q_RANK, -1)], 1)
    qraw = norm_matmul(z, wq.astype(BF16), gain=g_qa, a_col_block=0, out_dtype=BF16, tn=wq.shape[1], name="mla_qb")
    kvraw = norm_matmul(z, w_kvb.astype(BF16), gain=g_kva, a_col_block=1, out_dtype=BF16, tn=w_kvb.shape[1],
                        name="mla_kvb")

    scale = LOG2_E / math.sqrt(MLA_QK)
    inv_freq = ROPE_THETA ** (-jnp.arange(0, MLA_ROPE, 2, dtype=F32) / MLA_ROPE)
    sign = jnp.where((jnp.arange(LANES) % 64) < 32, -1.0, 1.0)
    rows = [
        _row128(jnp.tile(inv_freq, 4)),
        _row128(sign),
        _row128(g_q[:MLA_NOPE] * scale),
        _row128(jnp.tile(g_q[MLA_NOPE:], 2) * scale),
        _row128(g_k[:MLA_NOPE]),
        _row128(jnp.pad(g_k[MLA_NOPE:], (0, 64))),
    ]
    pos_col = positions.astype(F32).reshape(t, 1)
    qf, kf, vf = mla_prep(qraw, kvraw, z, pos_col, rows, b, s)
    o = attention(qf, kf, vf, fox=False)
    return norm_matmul(o, w_o.astype(BF16), residual=x, tn=w_o.shape[1], name="mla_out")


def fox_mixer(x, gain, w_in, b_f, g_q, g_k, w_o, b, s):
    nh = FOX_HEADS
    d = x.shape[1]
    zmain = norm_matmul(x, w_in.astype(BF16), gain=gain, n_cols=4 * d, tm=1024, out_dtype=BF16, name="fox_in")
    w_f = jnp.pad(w_in[:, 4 * d:], ((0, 0), (0, LANES - nh))).astype(BF16)
    flog = norm_matmul(x, w_f, gain=gain, tm=1024, name="fox_f")
    scale = LOG2_E / math.sqrt(FOX_HEAD_DIM)
    rows = [_row128(jnp.pad(b_f, (0, LANES - nh))), _row128(g_q * scale), _row128(g_k)]
    qf, kf, vf, ct = fox_prep(zmain, flog, rows, b, s)
    crow = ct[:, :nh, :]
    o = attention(qf, kf, vf, fox=True, ccol=crow[..., None], crow=crow, gate=zmain, gate_col0=3 * nh)
    return norm_matmul(o, w_o.astype(BF16), residual=x, tn=w_o.shape[1], name="fox_out")


def kernel(x, positions, norm_mix_g, norm_ffn_g, mla_w_in, mla_g_qa, mla_w_qb, mla_g_kva, mla_w_kvb, mla_g_q, mla_g_k, mla_w_o, fox_w_in, fox_b_f, fox_g_q, fox_g_k, fox_w_o, peer_w_q, peer_sub_keys, peer_u, peer_v):
    b, s, d = x.shape
    depth = norm_mix_g.shape[0]
    xt = x.reshape(b * s, d)
    peer_w = peer_weights(peer_w_q, peer_sub_keys, peer_u, peer_v)
    for i in range(depth):
        j = i // 2
        if i % 2 == 0:
            xt = mla_mixer(xt, norm_mix_g[i], positions, mla_w_in[j], mla_g_qa[j], mla_w_qb[j], mla_g_kva[j],
                           mla_w_kvb[j], mla_g_q[j], mla_g_k[j], mla_w_o[j], b, s)
        else:
            xt = fox_mixer(xt, norm_mix_g[i], fox_w_in[j], fox_b_f[j], fox_g_q[j], fox_g_k[j], fox_w_o[j], b, s)
        xt = peer_ffn(xt, norm_ffn_g[i], peer_w, i)
    return xt.reshape(b, s, d)
```

```python
import functools
import math

import jax
import jax.numpy as jnp
from jax import lax
from jax.experimental import pallas as pl
from jax.experimental.pallas import tpu as pltpu

F32 = jnp.float32
BF16 = jnp.bfloat16

RMS_EPS = 1e-6
NEG_INF = -1e30
LOG2_E = math.log2(math.e)
CHUNK_SHIFT = 6
ROPE_THETA = 10000.0

MLA_HEADS = 16
MLA_Q_RANK = 512
MLA_KV_RANK = 512
MLA_NOPE = 128
MLA_ROPE = 64
MLA_V = 128
MLA_QK = MLA_NOPE + MLA_ROPE

FOX_HEADS = 16
FOX_HEAD_DIM = 128

PEER_HEADS = 8
PEER_N_KEYS = 128
PEER_D_HALF = 128
PEER_TOPK = 16

LANES = 128
VMEM_LIMIT = 56 * 1024 * 1024


def _cparams(sem, flags=None):
    return pltpu.CompilerParams(dimension_semantics=sem, vmem_limit_bytes=VMEM_LIMIT, flags=flags)


def _nm_body(*refs, norm, has_res):
    it = iter(refs)
    a_ref = next(it)
    g_ref = next(it) if norm else None
    w_ref = next(it)
    r_ref = next(it) if has_res else None
    o_ref = next(it)
    an_ref = next(it) if norm else None

    if norm:
        @pl.when(pl.program_id(1) == 0)
        def _():
            a = a_ref[...].astype(F32)
            y = a * lax.rsqrt(jnp.mean(a * a, axis=-1, keepdims=True) + RMS_EPS)
            an_ref[...] = (y * g_ref[...]).astype(BF16)

        a_bf = an_ref[...]
    else:
        a_bf = a_ref[...]
    acc = jnp.dot(a_bf, w_ref[...], preferred_element_type=F32)
    if has_res:
        acc = acc + r_ref[...]
    o_ref[...] = acc.astype(o_ref.dtype)


def norm_matmul(a, w, *, gain=None, residual=None, a_col_block=0, n_cols=None, out_dtype=F32, tm=512, tn=512, name):
    m = a.shape[0]
    k = w.shape[0]
    n = w.shape[1] if n_cols is None else n_cols
    tn = min(tn, n)
    assert m % tm == 0 and n % tn == 0
    norm = gain is not None
    in_specs = [pl.BlockSpec((tm, k), lambda i, j: (i, a_col_block))]
    args = [a]
    if norm:
        in_specs.append(pl.BlockSpec((1, k), lambda i, j: (0, 0)))
        args.append(gain.reshape(1, k).astype(F32))
    in_specs.append(pl.BlockSpec((k, tn), lambda i, j: (0, j)))
    args.append(w)
    if residual is not None:
        in_specs.append(pl.BlockSpec((tm, tn), lambda i, j: (i, j)))
        args.append(residual)
    return pl.pallas_call(
        functools.partial(_nm_body, norm=norm, has_res=residual is not None),
        grid=(m // tm, n // tn),
        in_specs=in_specs,
        out_specs=pl.BlockSpec((tm, tn), lambda i, j: (i, j)),
        out_shape=jax.ShapeDtypeStruct((m, n), out_dtype),
        scratch_shapes=[pltpu.VMEM((tm, k), BF16)] if norm else [],
        compiler_params=_cparams(("parallel", "arbitrary")),
        name=name,
    )(*args)


def _swap_halves(x, first_half):
    return jnp.where(first_half, pltpu.roll(x, 96, 1), pltpu.roll(x, 32, 1))


def _mla_prep_body(q_ref, kv_ref, kr_ref, pos_ref, invf_ref, sign_ref, gqn_ref, gqr_ref, gkn_ref, gkr_ref,
                   qf_ref, kf_ref, vf_ref):
    ts = q_ref.shape[0]
    lane = lax.broadcasted_iota(jnp.int32, (ts, LANES), 1)
    first_half = (lane & 63) < 32
    low64 = lane < 64
    ang = pos_ref[...] * invf_ref[...]
    cosv = jnp.cos(ang)
    sinv = jnp.sin(ang) * sign_ref[...]

    def rot(x):
        return x * cosv + _swap_halves(x, first_half) * sinv

    inv_d = 1.0 / MLA_QK
    zeros = jnp.zeros((ts, LANES), F32)

    kr = kr_ref[...]
    ss_kr = jnp.sum(kr * kr, axis=-1, keepdims=True)
    kr_rot = rot(kr * gkr_ref[...])
    for h in range(MLA_HEADS):
        kn = kv_ref[:, h * 256:h * 256 + 128].astype(F32)
        r = lax.rsqrt((jnp.sum(kn * kn, axis=-1, keepdims=True) + ss_kr) * inv_d + RMS_EPS)
        kf_ref[0, h, :, 0:128] = (kn * r * gkn_ref[...]).astype(BF16)
        kf_ref[0, h, :, 128:256] = (kr_rot * r).astype(BF16)
        vf_ref[0, h] = kv_ref[:, h * 256 + 128:(h + 1) * 256].astype(BF16)

    for p in range(MLA_HEADS // 2):
        xr = q_ref[:, 2048 + p * 128:2048 + (p + 1) * 128].astype(F32)
        xr2 = xr * xr
        ss_e = jnp.sum(jnp.where(low64, xr2, 0.0), axis=-1, keepdims=True)
        ss_o = jnp.sum(jnp.where(low64, 0.0, xr2), axis=-1, keepdims=True)
        qn_e = q_ref[:, (2 * p) * 128:(2 * p + 1) * 128].astype(F32)
        qn_o = q_ref[:, (2 * p + 1) * 128:(2 * p + 2) * 128].astype(F32)
        r_e = lax.rsqrt((jnp.sum(qn_e * qn_e, axis=-1, keepdims=True) + ss_e) * inv_d + RMS_EPS)
        r_o = lax.rsqrt((jnp.sum(qn_o * qn_o, axis=-1, keepdims=True) + ss_o) * inv_d + RMS_EPS)
        xr_rot = rot(xr * jnp.where(low64, r_e, r_o) * gqr_ref[...])
        qf_ref[0, 2 * p, :, 0:128] = (qn_e * r_e * gqn_ref[...]).astype(BF16)
        qf_ref[0, 2 * p, :, 128:256] = jnp.where(low64, xr_rot, zeros).astype(BF16)
        qf_ref[0, 2 * p + 1, :, 0:128] = (qn_o * r_o * gqn_ref[...]).astype(BF16)
        qf_ref[0, 2 * p + 1, :, 128:256] = jnp.where(low64, pltpu.roll(xr_rot, 64, 1), zeros).astype(BF16)


def _mla_qkv_body(cq_ref, ckv_ref, gqa_ref, gkva_ref, wq_ref, wkv_ref, kr_ref, pos_ref, *rest):
    vec_refs, (qf_ref, kf_ref, vf_ref, q_s, kv_s) = rest[:6], rest[6:]

    def up_project(c_ref, g_ref, w_ref):
        c = c_ref[...]
        y = c * lax.rsqrt(jnp.mean(c * c, axis=-1, keepdims=True) + RMS_EPS)
        return jnp.dot((y * g_ref[...]).astype(BF16), w_ref[...], preferred_element_type=F32)

    q_s[...] = up_project(cq_ref, gqa_ref, wq_ref)
    kv_s[...] = up_project(ckv_ref, gkva_ref, wkv_ref)
    _mla_prep_body(q_s, kv_s, kr_ref, pos_ref, *vec_refs, qf_ref, kf_ref, vf_ref)


def mla_qkv_prep(z, wq, wkv, g_qa, g_kva, pos_col, rows, b, s, ts=256):
    ns = s // ts
    h = MLA_HEADS
    row = lambda blk: (lambda bi, si: (bi * ns + si, blk))
    vec = pl.BlockSpec((1, LANES), lambda bi, si: (0, 0))
    gain = lambda r: pl.BlockSpec((1, r), lambda bi, si: (0, 0))
    whole = lambda w: pl.BlockSpec(w.shape, lambda bi, si: (0, 0))
    head_out = lambda d: pl.BlockSpec((1, h, ts, d), lambda bi, si: (bi, 0, si, 0))
    return pl.pallas_call(
        _mla_qkv_body,
        grid=(b, ns),
        in_specs=[
            pl.BlockSpec((ts, MLA_Q_RANK), row(0)),
            pl.BlockSpec((ts, MLA_KV_RANK), row(MLA_Q_RANK // MLA_KV_RANK)),
            gain(MLA_Q_RANK), gain(MLA_KV_RANK), whole(wq), whole(wkv),
            pl.BlockSpec((ts, LANES), row((MLA_Q_RANK + MLA_KV_RANK) // LANES)),
            pl.BlockSpec((ts, 1), row(0)),
        ] + [vec] * 6,
        out_specs=[head_out(256), head_out(256), head_out(128)],
        out_shape=[
            jax.ShapeDtypeStruct((b, h, s, 256), BF16),
            jax.ShapeDtypeStruct((b, h, s, 256), BF16),
            jax.ShapeDtypeStruct((b, h, s, 128), BF16),
        ],
        scratch_shapes=[pltpu.VMEM((ts, wq.shape[1]), F32), pltpu.VMEM((ts, wkv.shape[1]), F32)],
        compiler_params=_cparams(("parallel", "parallel")),
        name="mla_qkv_prep",
    )(z, z, g_qa.reshape(1, -1).astype(F32), g_kva.reshape(1, -1).astype(F32), wq, wkv, z, pos_col, *rows)


def _split3(x):
    hi = x.astype(BF16)
    r1 = x - hi.astype(F32)
    mid = r1.astype(BF16)
    lo = (r1 - mid.astype(F32)).astype(BF16)
    return hi, mid, lo


def _fox_prep_body(z_ref, f_ref, bf_ref, gq_ref, gk_ref, qf_ref, kf_ref, vf_ref, ct_ref, carry_ref):
    ts = z_ref.shape[0]
    d = FOX_HEAD_DIM
    nh = FOX_HEADS

    @pl.when(pl.program_id(1) == 0)
    def _():
        carry_ref[...] = jnp.zeros_like(carry_ref)

    x = f_ref[...] + bf_ref[...]
    logf = jnp.minimum(x, 0.0) - jnp.log1p(jnp.exp(-jnp.abs(x)))
    tri = (lax.broadcasted_iota(jnp.int32, (ts, ts), 0) >= lax.broadcasted_iota(jnp.int32, (ts, ts), 1)).astype(BF16)
    hi, mid, lo = _split3(logf)
    local = (jnp.dot(tri, hi, preferred_element_type=F32) + jnp.dot(tri, mid, preferred_element_type=F32)
             + jnp.dot(tri, lo, preferred_element_type=F32))
    c = carry_ref[0:1, :] + local
    carry_ref[0:1, :] = c[ts - 1:ts, :]
    ct_ref[0] = (c * LOG2_E).T

    inv_d = 1.0 / d
    for h in range(nh):
        q = z_ref[:, h * d:(h + 1) * d].astype(F32)
        k = z_ref[:, (nh + h) * d:(nh + h + 1) * d].astype(F32)
        rq = lax.rsqrt(jnp.sum(q * q, axis=-1, keepdims=True) * inv_d + RMS_EPS)
        rk = lax.rsqrt(jnp.sum(k * k, axis=-1, keepdims=True) * inv_d + RMS_EPS)
        qf_ref[0, h] = (q * rq * gq_ref[...]).astype(BF16)
        kf_ref[0, h] = (k * rk * gk_ref[...]).astype(BF16)
        vf_ref[0, h] = z_ref[:, (2 * nh + h) * d:(2 * nh + h + 1) * d]


def fox_prep(zmain, flog, rows, b, s, ts=256):
    ns = s // ts
    h = FOX_HEADS
    row = lambda bi, si: (bi * ns + si, 0)
    vec = pl.BlockSpec((1, LANES), lambda bi, si: (0, 0))
    head_out = pl.BlockSpec((1, h, ts, 128), lambda bi, si: (bi, 0, si, 0))
    return pl.pallas_call(
        _fox_prep_body,
        grid=(b, ns),
        in_specs=[pl.BlockSpec((ts, zmain.shape[1]), row), pl.BlockSpec((ts, LANES), row), vec, vec, vec],
        out_specs=[head_out, head_out, head_out, pl.BlockSpec((1, LANES, ts), lambda bi, si: (bi, 0, si))],
        out_shape=[jax.ShapeDtypeStruct((b, h, s, 128), BF16)] * 3 + [jax.ShapeDtypeStruct((b, LANES, s), F32)],
        scratch_shapes=[pltpu.VMEM((8, LANES), F32)],
        compiler_params=_cparams(("parallel", "arbitrary")),
        name="fox_prep",
    )(zmain, flog, *rows)


_ATTN_HEADS = 2


def _attn_body(*refs, fox, tq, nq):
    if fox:
        q_ref, k_ref, v_ref, ccol_ref, crow_ref, gate_ref, o_ref, s_ref, p_ref = refs
    else:
        q_ref, k_ref, v_ref, o_ref, s_ref, p_ref = refs
    nt = (((1,), (1,)), ((), ()))
    half = tq // 2
    dv = v_ref.shape[-1]
    r = lax.broadcasted_iota(jnp.int32, (tq, tq), 0)
    c = lax.broadcasted_iota(jnp.int32, (tq, tq), 1)
    allowed = (c <= r) if fox else ((c >> CHUNK_SHIFT) <= (r >> CHUNK_SHIFT))
    for i in range(nq):
        for hh in range(_ATTN_HEADS):
            buf = 2 * hh + i % 2
            rows = slice(i * tq, (i + 1) * tq)
            q = q_ref[0, hh, rows, :]
            mx = jnp.full((tq, half), NEG_INF, F32)
            for j in range(i + 1):
                cols = slice(j * tq, (j + 1) * tq)
                s = lax.dot_general(q, k_ref[0, hh, cols, :], nt, preferred_element_type=F32)
                if fox:
                    s = s + (ccol_ref[0, hh, rows, :] - crow_ref[0, hh, :, cols])
                if j == i:
                    s = jnp.where(allowed, s, NEG_INF)
                s_ref[buf, :, cols] = s
                mx = jnp.maximum(mx, jnp.maximum(s[:, :half], s[:, half:]))
            m = jnp.max(mx, axis=-1, keepdims=True)
            ps = jnp.zeros((tq, half), F32)
            for j in range(i + 1):
                cols = slice(j * tq, (j + 1) * tq)
                p = jnp.exp2(s_ref[buf, :, cols] - m)
                ps = ps + (p[:, :half] + p[:, half:])
                p_ref[buf, :, cols] = p.astype(BF16)
            l = jnp.sum(ps, axis=-1, keepdims=True)
            n_keys = (i + 1) * tq
            o = jnp.dot(p_ref[buf, :, :n_keys], v_ref[0, hh, :n_keys, :], preferred_element_type=F32) / l
            out_cols = slice(hh * dv, (hh + 1) * dv)
            if fox:
                o = o * (1.0 / (1.0 + jnp.exp(-gate_ref[rows, out_cols].astype(F32))))
            o_ref[rows, out_cols] = o.astype(o_ref.dtype)


def attention(qf, kf, vf, *, fox, ccol=None, crow=None, gate=None, gate_col0=0, tq=256):
    b, h, s, dk = qf.shape
    dv = vf.shape[-1]
    nq = s // tq
    hps = _ATTN_HEADS
    head = lambda d: pl.BlockSpec((1, hps, s, d), lambda bi, hi: (bi, hi, 0, 0))
    in_specs = [head(dk), head(dk), head(dv)]
    args = [qf, kf, vf]
    if fox:
        in_specs += [
            head(1),
            pl.BlockSpec((1, hps, 1, s), lambda bi, hi: (bi, hi, 0, 0)),
            pl.BlockSpec((s, hps * dv), lambda bi, hi: (bi, gate_col0 // hps + hi)),
        ]
        args += [ccol, crow.reshape(b, h, 1, s), gate]
    return pl.pallas_call(
        functools.partial(_attn_body, fox=fox, tq=tq, nq=nq),
        grid=(b, h // hps),
        in_specs=in_specs,
        out_specs=pl.BlockSpec((s, hps * dv), lambda bi, hi: (bi, hi)),
        out_shape=jax.ShapeDtypeStruct((b * s, h * dv), BF16),
        scratch_shapes=[pltpu.VMEM((2 * hps, tq, s), F32), pltpu.VMEM((2 * hps, tq, s), BF16)],
        compiler_params=_cparams(("parallel", "parallel")),
        name="fox_attn" if fox else "mla_attn",
    )(*args)


_CAND_PAIRS = [(p, q) for p in range(PEER_TOPK) for q in range(PEER_TOPK) if (p + 1) * (q + 1) <= PEER_TOPK]
_CAND_ROWS = 56


def _top_rows(v, n, want_rank=False):
    rank = jnp.full(v.shape, float(n), F32)
    out = []
    for r in range(n):
        m = jnp.max(v, axis=0, keepdims=True)
        out.append(m)
        hit = v == m
        if want_rank:
            rank = jnp.where(hit, float(r), rank)
        if r + 1 < n:
            v = jnp.where(hit, -jnp.inf, v)
    return (out, rank) if want_rank else out


def _peer_prep_body(x_ref, g_ref, wq_ref, sk_ref, xn_ref, rk2_ref, e2_ref, n1_ref, c1_ref, s_ref, cand_ref):
    @pl.when(pl.program_id(1) == 0)
    def _():
        a = x_ref[...]
        y = a * lax.rsqrt(jnp.mean(a * a, axis=-1, keepdims=True) + RMS_EPS)
        xn_ref[...] = (y * g_ref[...]).T.astype(BF16)

    qt = jnp.dot(wq_ref[...], xn_ref[...], preferred_element_type=F32)
    s_ref[0] = jnp.dot(sk_ref[0], qt[0:128].astype(BF16), preferred_element_type=F32)
    s_ref[1] = jnp.dot(sk_ref[1], qt[128:256].astype(BF16), preferred_element_type=F32)
    cand_ref[...] = jnp.full(cand_ref.shape, -jnp.inf, F32)

    def lane_group(c, carry):
        lanes = pl.ds(pl.multiple_of(c * LANES, LANES), LANES)
        s1 = s_ref[0, :, lanes]
        s2 = s_ref[1, :, lanes]
        a1 = _top_rows(s1, PEER_TOPK)
        a2, rank2 = _top_rows(s2, PEER_TOPK, want_rank=True)
        for r, (p, q) in enumerate(_CAND_PAIRS):
            cand_ref[r:r + 1, :] = a1[p] + a2[q]
        best = _top_rows(cand_ref[...], PEER_TOPK)
        tau = best[PEER_TOPK - 1]
        z = jnp.ones_like(tau)
        for r in range(1, PEER_TOPK):
            z = z + jnp.exp(best[r] - best[0])
        n1 = jnp.zeros(s1.shape, F32)
        for p in range(PEER_TOPK):
            n_p = jnp.zeros_like(tau)
            for q in range(PEER_TOPK // (p + 1)):
                n_p = n_p + jnp.where(a1[p] + a2[q] >= tau, 1.0, 0.0)
            n1 = jnp.where(s1 == a1[p], n_p, n1)
        rk2_ref[0, :, lanes] = rank2.astype(BF16)
        e2_ref[0, :, lanes] = jnp.exp(s2 - a2[0]).astype(BF16)
        n1_ref[0, :, lanes] = n1
        c1_ref[0, :, lanes] = 0.5 * jnp.exp(s1 - a1[0]) / z
        return carry

    lax.fori_loop(0, s_ref.shape[-1] // LANES, lane_group, 0, unroll=2)


def peer_prep(x, gain, wq_t, sk, layer, tm=512):
    t, d = x.shape
    nh = PEER_HEADS
    stat = pl.BlockSpec((1, PEER_N_KEYS, tm), lambda i, h: (h, 0, i))
    stat_f32 = jax.ShapeDtypeStruct((nh, PEER_N_KEYS, t), F32)
    stat_bf16 = jax.ShapeDtypeStruct((nh, PEER_N_KEYS, t), BF16)
    return pl.pallas_call(
        _peer_prep_body,
        grid=(t // tm, nh),
        in_specs=[
            pl.BlockSpec((tm, d), lambda i, h: (i, 0)),
            pl.BlockSpec((1, d), lambda i, h: (0, 0)),
            pl.BlockSpec((None, 2 * PEER_D_HALF, d), lambda i, h: (layer, h, 0)),
            pl.BlockSpec((None, 2, PEER_N_KEYS, PEER_D_HALF), lambda i, h: (layer, 0, 0, 0)),
        ],
        out_specs=[pl.BlockSpec((d, tm), lambda i, h: (0, i)), stat, stat, stat, stat],
        out_shape=[jax.ShapeDtypeStruct((d, t), BF16), stat_bf16, stat_bf16, stat_f32, stat_f32],
        scratch_shapes=[pltpu.VMEM((2, PEER_N_KEYS, tm), F32), pltpu.VMEM((_CAND_ROWS, LANES), F32)],
        compiler_params=_cparams(("parallel", "arbitrary")),
        name="peer_prep",
    )(x, gain.reshape(1, d).astype(F32), wq_t, sk)


_SLAB = 16
_GATE_LANES = 256


def _peer_main_body(xn_ref, u_ref, vta_ref, vtb_ref, rk2_ref, e2_ref, n1_ref, c1_ref, res_ref, o_ref,
                    acc_ref, h_ref, a_ref):
    k = pl.program_id(1)
    nk = pl.num_programs(1) - 1
    te, tm = h_ref.shape[1:]
    n_i = te // PEER_N_KEYS
    n_slab = PEER_N_KEYS // _SLAB
    zero = jnp.zeros((_SLAB, _GATE_LANES), BF16)

    def gate_stage(slot, blk):
        for lc in range(tm // _GATE_LANES):
            lanes = slice(lc * _GATE_LANES, (lc + 1) * _GATE_LANES)
            for ii in range(n_i):
                i = blk * n_i + ii
                g = [None] * n_slab
                for h in range(PEER_HEADS):
                    nb = jnp.broadcast_to(n1_ref[h, pl.ds(i, 1), lanes], (_SLAB, _GATE_LANES)).astype(BF16)
                    cb = jnp.broadcast_to(c1_ref[h, pl.ds(i, 1), lanes], (_SLAB, _GATE_LANES)).astype(BF16)
                    for sl in range(n_slab):
                        rows = slice(sl * _SLAB, (sl + 1) * _SLAB)
                        w = jnp.where(rk2_ref[h, rows, lanes] < nb, e2_ref[h, rows, lanes] * cb, zero)
                        g[sl] = w if g[sl] is None else g[sl] + w
                for sl in range(n_slab):
                    rows = slice(ii * PEER_N_KEYS + sl * _SLAB, ii * PEER_N_KEYS + (sl + 1) * _SLAB)
                    hh = h_ref[slot, rows, lanes]
                    act = hh * (1.0 + lax.erf(hh * (1.0 / math.sqrt(2.0))))
                    a_ref[slot, rows, lanes] = act.astype(BF16) * g[sl]

    def pre_activations(slot):
        h_ref[slot] = jnp.dot(u_ref[slot * te:(slot + 1) * te, :], xn_ref[...], preferred_element_type=F32)

    def block_output(slot, vt_ref):
        return jnp.dot(vt_ref[...], a_ref[slot], preferred_element_type=F32)

    @pl.when(k == 0)
    def _():
        pre_activations(0)
        gate_stage(0, 0)
        pre_activations(1)
        acc_ref[...] = block_output(0, vtb_ref)

    @pl.when((k > 0) & (k < nk))
    def _():
        gate_stage(1, 2 * k - 1)
        pre_activations(0)
        acc_ref[...] += block_output(1, vta_ref)
        gate_stage(0, 2 * k)
        pre_activations(1)
        acc_ref[...] += block_output(0, vtb_ref)

    @pl.when(k == nk)
    def _():
        gate_stage(1, 2 * k - 1)
        o_ref[...] = res_ref[...] + (acc_ref[...] + block_output(1, vta_ref)).T


def peer_main(xn_t, u, vt, rk2, e2, n1, c1, res, layer, tm=512):
    d, t = xn_t.shape
    te = vt.shape[-1]
    nk = u.shape[1] // (2 * te)
    nh = PEER_HEADS
    once = pl.Buffered(1)
    stat = pl.BlockSpec((nh, PEER_N_KEYS, tm), lambda i, k: (0, 0, i), pipeline_mode=once)
    return pl.pallas_call(
        _peer_main_body,
        grid=(t // tm, nk + 1),
        in_specs=[
            pl.BlockSpec((d, tm), lambda i, k: (0, i), pipeline_mode=once),
            pl.BlockSpec((None, 2 * te, d), lambda i, k: (layer, jnp.minimum(k, nk - 1), 0)),
            pl.BlockSpec((None, None, d, te), lambda i, k: (layer, jnp.clip(2 * k - 1, 0, 2 * nk - 1), 0, 0)),
            pl.BlockSpec((None, None, d, te), lambda i, k: (layer, jnp.minimum(2 * k, 2 * nk - 1), 0, 0)),
            stat, stat, stat, stat,
            pl.BlockSpec((tm, d), lambda i, k: (i, 0), pipeline_mode=once),
        ],
        out_specs=pl.BlockSpec((tm, d), lambda i, k: (i, 0)),
        out_shape=jax.ShapeDtypeStruct((t, d), F32),
        scratch_shapes=[pltpu.VMEM((d, tm), F32), pltpu.VMEM((2, te, tm), F32), pltpu.VMEM((2, te, tm), BF16)],
        compiler_params=_cparams(("parallel", "arbitrary")),
        name="peer_main",
    )(xn_t, u, vt, vt, rk2, e2, n1, c1, res)


PEER_BLOCK = 512


def _transpose_block_body(v_ref, o_ref):
    o_ref[...] = v_ref[...].T.astype(BF16)


def peer_v_blocks(v):
    n_layers, n_e, d = v.shape
    te = PEER_BLOCK
    return pl.pallas_call(
        _transpose_block_body,
        grid=(n_layers, n_e // te),
        in_specs=[pl.BlockSpec((None, te, d), lambda l, e: (l, e, 0))],
        out_specs=pl.BlockSpec((None, None, d, te), lambda l, e: (l, e, 0, 0)),
        out_shape=jax.ShapeDtypeStruct((n_layers, n_e // te, d, te), BF16),
        compiler_params=_cparams(("parallel", "parallel")),
        name="peer_v_blocks",
    )(v)


def peer_weights(w_q, sub_keys, u, v):
    wq_t = jnp.swapaxes(w_q, 1, 2).astype(BF16)
    return wq_t, sub_keys.astype(BF16), u.astype(BF16), peer_v_blocks(v)


def peer_ffn(x, gain, weights, layer):
    wq_t, sk, u, vt = weights
    xn_t, rk2, e2, n1, c1 = peer_prep(x, gain, wq_t, sk, layer)
    return peer_main(xn_t, u, vt, rk2, e2, n1, c1, x, layer)


def _row128(v):
    return v.reshape(1, LANES).astype(F32)


def mla_mixer(x, gain, positions, w_in, g_qa, w_qb, g_kva, w_kvb, g_q, g_k, w_o, b, s):
    t = b * s
    nh = MLA_HEADS
    w_in_p = jnp.pad(w_in, ((0, 0), (0, 64))).astype(BF16)
    z = norm_matmul(x, w_in_p, gain=gain, tn=w_in_p.shape[1], name="mla_in")
    wq = w_qb.reshape(MLA_Q_RANK, nh, MLA_QK)
    wq = jnp.concatenate([wq[:, :, :MLA_NOPE].reshape(MLA_Q_RANK, -1), wq[:, :, MLA_NOPE:].reshape(MLA_Q_RANK, -1)], 1)

    scale = LOG2_E / math.sqrt(MLA_QK)
    inv_freq = ROPE_THETA ** (-jnp.arange(0, MLA_ROPE, 2, dtype=F32) / MLA_ROPE)
    sign = jnp.where((jnp.arange(LANES) % 64) < 32, -1.0, 1.0)
    rows = [
        _row128(jnp.tile(inv_freq, 4)),
        _row128(sign),
        _row128(g_q[:MLA_NOPE] * scale),
        _row128(jnp.tile(g_q[MLA_NOPE:], 2) * scale),
        _row128(g_k[:MLA_NOPE]),
        _row128(jnp.pad(g_k[MLA_NOPE:], (0, 64))),
    ]
    pos_col = positions.astype(F32).reshape(t, 1)
    qf, kf, vf = mla_qkv_prep(z, wq.astype(BF16), w_kvb.astype(BF16), g_qa, g_kva, pos_col, rows, b, s)
    o = attention(qf, kf, vf, fox=False)
    return norm_matmul(o, w_o.astype(BF16), residual=x, tn=w_o.shape[1], name="mla_out")


def fox_mixer(x, gain, w_in, b_f, g_q, g_k, w_o, b, s):
    nh = FOX_HEADS
    d = x.shape[1]
    zmain = norm_matmul(x, w_in.astype(BF16), gain=gain, n_cols=4 * d, tm=1024, out_dtype=BF16, name="fox_in")
    w_f = jnp.pad(w_in[:, 4 * d:], ((0, 0), (0, LANES - nh))).astype(BF16)
    flog = norm_matmul(x, w_f, gain=gain, tm=1024, name="fox_f")
    scale = LOG2_E / math.sqrt(FOX_HEAD_DIM)
    rows = [_row128(jnp.pad(b_f, (0, LANES - nh))), _row128(g_q * scale), _row128(g_k)]
    qf, kf, vf, ct = fox_prep(zmain, flog, rows, b, s)
    crow = ct[:, :nh, :]
    o = attention(qf, kf, vf, fox=True, ccol=crow[..., None], crow=crow, gate=zmain, gate_col0=3 * nh)
    return norm_matmul(o, w_o.astype(BF16), residual=x, tn=w_o.shape[1], name="fox_out")


def kernel(x, positions, norm_mix_g, norm_ffn_g, mla_w_in, mla_g_qa, mla_w_qb, mla_g_kva, mla_w_kvb, mla_g_q, mla_g_k, mla_w_o, fox_w_in, fox_b_f, fox_g_q, fox_g_k, fox_w_o, peer_w_q, peer_sub_keys, peer_u, peer_v):
    b, s, d = x.shape
    depth = norm_mix_g.shape[0]
    xt = x.reshape(b * s, d)
    peer_w = peer_weights(peer_w_q, peer_sub_keys, peer_u, peer_v)
    for i in range(depth):
        j = i // 2
        if i % 2 == 0:
            xt = mla_mixer(xt, norm_mix_g[i], positions, mla_w_in[j], mla_g_qa[j], mla_w_qb[j], mla_g_kva[j],
                           mla_w_kvb[j], mla_g_q[j], mla_g_k[j], mla_w_o[j], b, s)
        else:
            xt = fox_mixer(xt, norm_mix_g[i], fox_w_in[j], fox_b_f[j], fox_g_q[j], fox_g_k[j], fox_w_o[j], b, s)
        xt = peer_ffn(xt, norm_ffn_g[i], peer_w, i)
    return xt.reshape(b, s, d)
```

```python
import functools
import math

import jax
import jax.numpy as jnp
from jax import lax
from jax.experimental import pallas as pl
from jax.experimental.pallas import tpu as pltpu

F32 = jnp.float32
BF16 = jnp.bfloat16

RMS_EPS = 1e-6
NEG_INF = -1e30
LOG2_E = math.log2(math.e)
CHUNK_SHIFT = 6
ROPE_THETA = 10000.0

MLA_HEADS = 16
MLA_Q_RANK = 512
MLA_KV_RANK = 512
MLA_NOPE = 128
MLA_ROPE = 64
MLA_V = 128
MLA_QK = MLA_NOPE + MLA_ROPE

FOX_HEADS = 16
FOX_HEAD_DIM = 128

PEER_HEADS = 8
PEER_N_KEYS = 128
PEER_D_HALF = 128
PEER_TOPK = 16

LANES = 128
VMEM_LIMIT = 56 * 1024 * 1024


def _cparams(sem, flags=None):
    return pltpu.CompilerParams(dimension_semantics=sem, vmem_limit_bytes=VMEM_LIMIT, flags=flags)


def _nm_body(*refs, norm, has_res):
    it = iter(refs)
    a_ref = next(it)
    g_ref = next(it) if norm else None
    w_ref = next(it)
    r_ref = next(it) if has_res else None
    o_ref = next(it)
    an_ref = next(it) if norm else None

    if norm:
        @pl.when(pl.program_id(1) == 0)
        def _():
            a = a_ref[...].astype(F32)
            y = a * lax.rsqrt(jnp.mean(a * a, axis=-1, keepdims=True) + RMS_EPS)
            an_ref[...] = (y * g_ref[...]).astype(BF16)

        a_bf = an_ref[...]
    else:
        a_bf = a_ref[...]
    acc = jnp.dot(a_bf, w_ref[...], preferred_element_type=F32)
    if has_res:
        acc = acc + r_ref[...]
    o_ref[...] = acc.astype(o_ref.dtype)


def norm_matmul(a, w, *, gain=None, residual=None, a_col_block=0, n_cols=None, out_dtype=F32, tm=512, tn=512, name):
    m = a.shape[0]
    k = w.shape[0]
    n = w.shape[1] if n_cols is None else n_cols
    tn = min(tn, n)
    assert m % tm == 0 and n % tn == 0
    norm = gain is not None
    in_specs = [pl.BlockSpec((tm, k), lambda i, j: (i, a_col_block))]
    args = [a]
    if norm:
        in_specs.append(pl.BlockSpec((1, k), lambda i, j: (0, 0)))
        args.append(gain.reshape(1, k).astype(F32))
    in_specs.append(pl.BlockSpec((k, tn), lambda i, j: (0, j)))
    args.append(w)
    if residual is not None:
        in_specs.append(pl.BlockSpec((tm, tn), lambda i, j: (i, j)))
        args.append(residual)
    return pl.pallas_call(
        functools.partial(_nm_body, norm=norm, has_res=residual is not None),
        grid=(m // tm, n // tn),
        in_specs=in_specs,
        out_specs=pl.BlockSpec((tm, tn), lambda i, j: (i, j)),
        out_shape=jax.ShapeDtypeStruct((m, n), out_dtype),
        scratch_shapes=[pltpu.VMEM((tm, k), BF16)] if norm else [],
        compiler_params=_cparams(("parallel", "arbitrary")),
        name=name,
    )(*args)


def _swap_halves(x, first_half):
    return jnp.where(first_half, pltpu.roll(x, 96, 1), pltpu.roll(x, 32, 1))


def _mla_prep_body(q_ref, kv_ref, kr_ref, pos_ref, invf_ref, sign_ref, gqn_ref, gqr_ref, gkn_ref, gkr_ref,
                   qf_ref, kf_ref, vf_ref):
    ts = q_ref.shape[0]
    lane = lax.broadcasted_iota(jnp.int32, (ts, LANES), 1)
    first_half = (lane & 63) < 32
    low64 = lane < 64
    ang = pos_ref[...] * invf_ref[...]
    cosv = jnp.cos(ang)
    sinv = jnp.sin(ang) * sign_ref[...]

    def rot(x):
        return x * cosv + _swap_halves(x, first_half) * sinv

    inv_d = 1.0 / MLA_QK
    zeros = jnp.zeros((ts, LANES), F32)

    kr = kr_ref[...]
    ss_kr = jnp.sum(kr * kr, axis=-1, keepdims=True)
    kr_rot = rot(kr * gkr_ref[...])
    for h in range(MLA_HEADS):
        kn = kv_ref[:, h * 256:h * 256 + 128].astype(F32)
        r = lax.rsqrt((jnp.sum(kn * kn, axis=-1, keepdims=True) + ss_kr) * inv_d + RMS_EPS)
        kf_ref[0, h, :, 0:128] = (kn * r * gkn_ref[...]).astype(BF16)
        kf_ref[0, h, :, 128:256] = (kr_rot * r).astype(BF16)
        vf_ref[0, h] = kv_ref[:, h * 256 + 128:(h + 1) * 256].astype(BF16)

    for p in range(MLA_HEADS // 2):
        xr = q_ref[:, 2048 + p * 128:2048 + (p + 1) * 128].astype(F32)
        xr2 = xr * xr
        ss_e = jnp.sum(jnp.where(low64, xr2, 0.0), axis=-1, keepdims=True)
        ss_o = jnp.sum(jnp.where(low64, 0.0, xr2), axis=-1, keepdims=True)
        qn_e = q_ref[:, (2 * p) * 128:(2 * p + 1) * 128].astype(F32)
        qn_o = q_ref[:, (2 * p + 1) * 128:(2 * p + 2) * 128].astype(F32)
        r_e = lax.rsqrt((jnp.sum(qn_e * qn_e, axis=-1, keepdims=True) + ss_e) * inv_d + RMS_EPS)
        r_o = lax.rsqrt((jnp.sum(qn_o * qn_o, axis=-1, keepdims=True) + ss_o) * inv_d + RMS_EPS)
        xr_rot = rot(xr * jnp.where(low64, r_e, r_o) * gqr_ref[...])
        qf_ref[0, 2 * p, :, 0:128] = (qn_e * r_e * gqn_ref[...]).astype(BF16)
        qf_ref[0, 2 * p, :, 128:256] = jnp.where(low64, xr_rot, zeros).astype(BF16)
        qf_ref[0, 2 * p + 1, :, 0:128] = (qn_o * r_o * gqn_ref[...]).astype(BF16)
        qf_ref[0, 2 * p + 1, :, 128:256] = jnp.where(low64, pltpu.roll(xr_rot, 64, 1), zeros).astype(BF16)


def _mla_qkv_body(cq_ref, ckv_ref, gqa_ref, gkva_ref, wq_ref, wkv_ref, kr_ref, pos_ref, *rest):
    vec_refs, (qf_ref, kf_ref, vf_ref, q_s, kv_s) = rest[:6], rest[6:]

    def up_project(c_ref, g_ref, w_ref):
        c = c_ref[...]
        y = c * lax.rsqrt(jnp.mean(c * c, axis=-1, keepdims=True) + RMS_EPS)
        return jnp.dot((y * g_ref[...]).astype(BF16), w_ref[...], preferred_element_type=F32)

    q_s[...] = up_project(cq_ref, gqa_ref, wq_ref)
    kv_s[...] = up_project(ckv_ref, gkva_ref, wkv_ref)
    _mla_prep_body(q_s, kv_s, kr_ref, pos_ref, *vec_refs, qf_ref, kf_ref, vf_ref)


def mla_qkv_prep(z, wq, wkv, g_qa, g_kva, pos_col, rows, b, s, ts=256):
    ns = s // ts
    h = MLA_HEADS
    row = lambda blk: (lambda bi, si: (bi * ns + si, blk))
    vec = pl.BlockSpec((1, LANES), lambda bi, si: (0, 0))
    gain = lambda r: pl.BlockSpec((1, r), lambda bi, si: (0, 0))
    whole = lambda w: pl.BlockSpec(w.shape, lambda bi, si: (0, 0))
    head_out = lambda d: pl.BlockSpec((1, h, ts, d), lambda bi, si: (bi, 0, si, 0))
    return pl.pallas_call(
        _mla_qkv_body,
        grid=(b, ns),
        in_specs=[
            pl.BlockSpec((ts, MLA_Q_RANK), row(0)),
            pl.BlockSpec((ts, MLA_KV_RANK), row(MLA_Q_RANK // MLA_KV_RANK)),
            gain(MLA_Q_RANK), gain(MLA_KV_RANK), whole(wq), whole(wkv),
            pl.BlockSpec((ts, LANES), row((MLA_Q_RANK + MLA_KV_RANK) // LANES)),
            pl.BlockSpec((ts, 1), row(0)),
        ] + [vec] * 6,
        out_specs=[head_out(256), head_out(256), head_out(128)],
        out_shape=[
            jax.ShapeDtypeStruct((b, h, s, 256), BF16),
            jax.ShapeDtypeStruct((b, h, s, 256), BF16),
            jax.ShapeDtypeStruct((b, h, s, 128), BF16),
        ],
        scratch_shapes=[pltpu.VMEM((ts, wq.shape[1]), F32), pltpu.VMEM((ts, wkv.shape[1]), F32)],
        compiler_params=_cparams(("parallel", "parallel")),
        name="mla_qkv_prep",
    )(z, z, g_qa.reshape(1, -1).astype(F32), g_kva.reshape(1, -1).astype(F32), wq, wkv, z, pos_col, *rows)


def _split3(x):
    hi = x.astype(BF16)
    r1 = x - hi.astype(F32)
    mid = r1.astype(BF16)
    lo = (r1 - mid.astype(F32)).astype(BF16)
    return hi, mid, lo


def _fox_prep_body(z_ref, f_ref, bf_ref, gq_ref, gk_ref, qf_ref, kf_ref, vf_ref, ct_ref, carry_ref):
    ts = z_ref.shape[0]
    d = FOX_HEAD_DIM
    nh = FOX_HEADS

    @pl.when(pl.program_id(1) == 0)
    def _():
        carry_ref[...] = jnp.zeros_like(carry_ref)

    x = f_ref[...] + bf_ref[...]
    logf = jnp.minimum(x, 0.0) - jnp.log1p(jnp.exp(-jnp.abs(x)))
    tri = (lax.broadcasted_iota(jnp.int32, (ts, ts), 0) >= lax.broadcasted_iota(jnp.int32, (ts, ts), 1)).astype(BF16)
    hi, mid, lo = _split3(logf)
    local = (jnp.dot(tri, hi, preferred_element_type=F32) + jnp.dot(tri, mid, preferred_element_type=F32)
             + jnp.dot(tri, lo, preferred_element_type=F32))
    c = carry_ref[0:1, :] + local
    carry_ref[0:1, :] = c[ts - 1:ts, :]
    ct_ref[0] = (c * LOG2_E).T

    inv_d = 1.0 / d
    for h in range(nh):
        q = z_ref[:, h * d:(h + 1) * d].astype(F32)
        k = z_ref[:, (nh + h) * d:(nh + h + 1) * d].astype(F32)
        rq = lax.rsqrt(jnp.sum(q * q, axis=-1, keepdims=True) * inv_d + RMS_EPS)
        rk = lax.rsqrt(jnp.sum(k * k, axis=-1, keepdims=True) * inv_d + RMS_EPS)
        qf_ref[0, h] = (q * rq * gq_ref[...]).astype(BF16)
        kf_ref[0, h] = (k * rk * gk_ref[...]).astype(BF16)
        vf_ref[0, h] = z_ref[:, (2 * nh + h) * d:(2 * nh + h + 1) * d]


def fox_prep(zmain, flog, rows, b, s, ts=256):
    ns = s // ts
    h = FOX_HEADS
    row = lambda bi, si: (bi * ns + si, 0)
    vec = pl.BlockSpec((1, LANES), lambda bi, si: (0, 0))
    head_out = pl.BlockSpec((1, h, ts, 128), lambda bi, si: (bi, 0, si, 0))
    return pl.pallas_call(
        _fox_prep_body,
        grid=(b, ns),
        in_specs=[pl.BlockSpec((ts, zmain.shape[1]), row), pl.BlockSpec((ts, LANES), row), vec, vec, vec],
        out_specs=[head_out, head_out, head_out, pl.BlockSpec((1, LANES, ts), lambda bi, si: (bi, 0, si))],
        out_shape=[jax.ShapeDtypeStruct((b, h, s, 128), BF16)] * 3 + [jax.ShapeDtypeStruct((b, LANES, s), F32)],
        scratch_shapes=[pltpu.VMEM((8, LANES), F32)],
        compiler_params=_cparams(("parallel", "arbitrary")),
        name="fox_prep",
    )(zmain, flog, *rows)


_ATTN_HEADS = 2


def _attn_body(*refs, fox, tq, nq):
    if fox:
        q_ref, k_ref, v_ref, ccol_ref, crow_ref, gate_ref, o_ref, s_ref, p_ref = refs
    else:
        q_ref, k_ref, v_ref, o_ref, s_ref, p_ref = refs
    nt = (((1,), (1,)), ((), ()))
    half = tq // 2
    dv = v_ref.shape[-1]
    r = lax.broadcasted_iota(jnp.int32, (tq, tq), 0)
    c = lax.broadcasted_iota(jnp.int32, (tq, tq), 1)
    allowed = (c <= r) if fox else ((c >> CHUNK_SHIFT) <= (r >> CHUNK_SHIFT))
    for i in range(nq):
        for hh in range(_ATTN_HEADS):
            buf = 2 * hh + i % 2
            rows = slice(i * tq, (i + 1) * tq)
            q = q_ref[0, hh, rows, :]
            mx = jnp.full((tq, half), NEG_INF, F32)
            for j in range(i + 1):
                cols = slice(j * tq, (j + 1) * tq)
                s = lax.dot_general(q, k_ref[0, hh, cols, :], nt, preferred_element_type=F32)
                if fox:
                    s = s + (ccol_ref[0, hh, rows, :] - crow_ref[0, hh, :, cols])
                if j == i:
                    s = jnp.where(allowed, s, NEG_INF)
                s_ref[buf, :, cols] = s
                mx = jnp.maximum(mx, jnp.maximum(s[:, :half], s[:, half:]))
            m = jnp.max(mx, axis=-1, keepdims=True)
            ps = jnp.zeros((tq, half), F32)
            for j in range(i + 1):
                cols = slice(j * tq, (j + 1) * tq)
                p = jnp.exp2(s_ref[buf, :, cols] - m)
                ps = ps + (p[:, :half] + p[:, half:])
                p_ref[buf, :, cols] = p.astype(BF16)
            l = jnp.sum(ps, axis=-1, keepdims=True)
            n_keys = (i + 1) * tq
            o = jnp.dot(p_ref[buf, :, :n_keys], v_ref[0, hh, :n_keys, :], preferred_element_type=F32) / l
            out_cols = slice(hh * dv, (hh + 1) * dv)
            if fox:
                o = o * (1.0 / (1.0 + jnp.exp(-gate_ref[rows, out_cols].astype(F32))))
            o_ref[rows, out_cols] = o.astype(o_ref.dtype)


def attention(qf, kf, vf, *, fox, ccol=None, crow=None, gate=None, gate_col0=0, tq=256):
    b, h, s, dk = qf.shape
    dv = vf.shape[-1]
    nq = s // tq
    hps = _ATTN_HEADS
    head = lambda d: pl.BlockSpec((1, hps, s, d), lambda bi, hi: (bi, hi, 0, 0))
    in_specs = [head(dk), head(dk), head(dv)]
    args = [qf, kf, vf]
    if fox:
        in_specs += [
            head(1),
            pl.BlockSpec((1, hps, 1, s), lambda bi, hi: (bi, hi, 0, 0)),
            pl.BlockSpec((s, hps * dv), lambda bi, hi: (bi, gate_col0 // hps + hi)),
        ]
        args += [ccol, crow.reshape(b, h, 1, s), gate]
    return pl.pallas_call(
        functools.partial(_attn_body, fox=fox, tq=tq, nq=nq),
        grid=(b, h // hps),
        in_specs=in_specs,
        out_specs=pl.BlockSpec((s, hps * dv), lambda bi, hi: (bi, hi)),
        out_shape=jax.ShapeDtypeStruct((b * s, h * dv), BF16),
        scratch_shapes=[pltpu.VMEM((2 * hps, tq, s), F32), pltpu.VMEM((2 * hps, tq, s), BF16)],
        compiler_params=_cparams(("parallel", "parallel")),
        name="fox_attn" if fox else "mla_attn",
    )(*args)


_CAND_PAIRS = [(p, q) for p in range(PEER_TOPK) for q in range(PEER_TOPK) if (p + 1) * (q + 1) <= PEER_TOPK]
_CAND_ROWS = 56


def _top_rows(v, n, want_rank=False, exact=False):
    row_id = lax.broadcasted_iota(jnp.int32, v.shape, 0).astype(F32) if exact else None
    rank = jnp.full(v.shape, float(n), F32)
    out = []
    for r in range(n):
        m = jnp.max(v, axis=0, keepdims=True)
        out.append(m)
        hit = v == m
        if exact:
            hit = row_id == jnp.min(jnp.where(hit, row_id, float(v.shape[0])), axis=0, keepdims=True)
        if want_rank:
            rank = jnp.where(hit, float(r), rank)
        if r + 1 < n or exact:
            v = jnp.where(hit, -jnp.inf, v)
    return (out,) + ((rank,) if want_rank else ()) + ((v,) if exact else ())


def _count_ge(v, row):
    return jnp.sum(jnp.where(v >= row, 1.0, 0.0), axis=0, keepdims=True)


def _select_lane_group(s1, s2, cand_ref, exact):
    k = PEER_TOPK
    if exact:
        a1, rank1 = _top_rows(s1, k, want_rank=True, exact=True)[:2]
        a2, rank2 = _top_rows(s2, k, want_rank=True, exact=True)[:2]
    else:
        a1, = _top_rows(s1, k)
        a2, rank2 = _top_rows(s2, k, want_rank=True)
    for r, (p, q) in enumerate(_CAND_PAIRS):
        cand_ref[r:r + 1, :] = a1[p] + a2[q]
    cand = cand_ref[...]
    if exact:
        best, left = _top_rows(cand, k, exact=True)
        cand_ref[...] = jnp.where(left != cand, 1.0, 0.0)
    else:
        best, = _top_rows(cand, k)
    tau = best[k - 1]
    z = jnp.ones_like(tau)
    for r in range(1, k):
        z = z + jnp.exp(best[r] - best[0])
    n1 = jnp.zeros(s1.shape, F32)
    r = 0
    for p in range(k):
        n_p = jnp.zeros_like(tau)
        for q in range(k // (p + 1)):
            n_p = n_p + (cand_ref[r:r + 1, :] if exact else jnp.where(a1[p] + a2[q] >= tau, 1.0, 0.0))
            r += 1
        n1 = jnp.where((rank1 == float(p)) if exact else (s1 == a1[p]), n_p, n1)
    stats = (rank2.astype(BF16), jnp.exp(s2 - a2[0]).astype(BF16), n1, 0.5 * jnp.exp(s1 - a1[0]) / z)
    if exact:
        return stats
    ties = jnp.maximum(jnp.maximum(_count_ge(s1, a1[k - 1]), _count_ge(s2, a2[k - 1])), _count_ge(cand, tau)) > k
    return stats + (ties,)


def _peer_prep_body(x_ref, g_ref, wq_ref, sk_ref, xn_ref, rk2_ref, e2_ref, n1_ref, c1_ref, s_ref, cand_ref):
    @pl.when(pl.program_id(1) == 0)
    def _():
        a = x_ref[...]
        y = a * lax.rsqrt(jnp.mean(a * a, axis=-1, keepdims=True) + RMS_EPS)
        xn_ref[...] = (y * g_ref[...]).T.astype(BF16)

    qt = jnp.dot(wq_ref[...], xn_ref[...], preferred_element_type=F32)
    s_ref[0] = jnp.dot(sk_ref[0], qt[0:128].astype(BF16), preferred_element_type=F32)
    s_ref[1] = jnp.dot(sk_ref[1], qt[128:256].astype(BF16), preferred_element_type=F32)
    cand_ref[...] = jnp.full(cand_ref.shape, -jnp.inf, F32)

    def lane_groups(c, carry):
        def column(g):
            lanes = pl.ds(pl.multiple_of((2 * c + g) * LANES, LANES), LANES)
            slot = cand_ref.at[g]

            def store(rank2, e2, n1, c1):
                rk2_ref[0, :, lanes] = rank2
                e2_ref[0, :, lanes] = e2
                n1_ref[0, :, lanes] = n1
                c1_ref[0, :, lanes] = c1

            def select(exact):
                return _select_lane_group(s_ref[0, :, lanes], s_ref[1, :, lanes], slot, exact)

            *stats, ties = select(exact=False)
            store(*stats)

            def redo():
                store(*select(exact=True))
                slot[...] = jnp.full(slot.shape, -jnp.inf, F32)

            return jnp.max(jnp.where(ties, 1.0, 0.0)) > 0.0, redo

        fixes = [column(g) for g in range(2)]
        for tied, redo in fixes:
            pl.when(tied)(redo)
        return carry

    lax.fori_loop(0, s_ref.shape[-1] // (2 * LANES), lane_groups, 0)


def peer_prep(x, gain, wq_t, sk, layer, tm=512):
    t, d = x.shape
    nh = PEER_HEADS
    stat = pl.BlockSpec((1, PEER_N_KEYS, tm), lambda i, h: (h, 0, i))
    stat_f32 = jax.ShapeDtypeStruct((nh, PEER_N_KEYS, t), F32)
    stat_bf16 = jax.ShapeDtypeStruct((nh, PEER_N_KEYS, t), BF16)
    return pl.pallas_call(
        _peer_prep_body,
        grid=(t // tm, nh),
        in_specs=[
            pl.BlockSpec((tm, d), lambda i, h: (i, 0)),
            pl.BlockSpec((1, d), lambda i, h: (0, 0)),
            pl.BlockSpec((None, 2 * PEER_D_HALF, d), lambda i, h: (layer, h, 0)),
            pl.BlockSpec((None, 2, PEER_N_KEYS, PEER_D_HALF), lambda i, h: (layer, 0, 0, 0)),
        ],
        out_specs=[pl.BlockSpec((d, tm), lambda i, h: (0, i)), stat, stat, stat, stat],
        out_shape=[jax.ShapeDtypeStruct((d, t), BF16), stat_bf16, stat_bf16, stat_f32, stat_f32],
        scratch_shapes=[pltpu.VMEM((2, PEER_N_KEYS, tm), F32), pltpu.VMEM((2, _CAND_ROWS, LANES), F32)],
        compiler_params=_cparams(("parallel", "arbitrary")),
        name="peer_prep",
    )(x, gain.reshape(1, d).astype(F32), wq_t, sk)


_SLAB = 16
_GATE_LANES = 256


def _peer_main_body(xn_ref, u_ref, vta_ref, vtb_ref, rk2_ref, e2_ref, n1_ref, c1_ref, res_ref, o_ref,
                    acc_ref, h_ref, a_ref):
    k = pl.program_id(1)
    nk = pl.num_programs(1) - 1
    te, tm = h_ref.shape[1:]
    n_i = te // PEER_N_KEYS
    n_slab = PEER_N_KEYS // _SLAB
    zero = jnp.zeros((_SLAB, _GATE_LANES), BF16)

    def gate_stage(slot, blk):
        for lc in range(tm // _GATE_LANES):
            lanes = slice(lc * _GATE_LANES, (lc + 1) * _GATE_LANES)
            for ii in range(n_i):
                i = blk * n_i + ii
                g = [None] * n_slab
                for h in range(PEER_HEADS):
                    nb = jnp.broadcast_to(n1_ref[h, pl.ds(i, 1), lanes], (_SLAB, _GATE_LANES)).astype(BF16)
                    cb = jnp.broadcast_to(c1_ref[h, pl.ds(i, 1), lanes], (_SLAB, _GATE_LANES)).astype(BF16)
                    for sl in range(n_slab):
                        rows = slice(sl * _SLAB, (sl + 1) * _SLAB)
                        w = jnp.where(rk2_ref[h, rows, lanes] < nb, e2_ref[h, rows, lanes] * cb, zero)
                        g[sl] = w if g[sl] is None else g[sl] + w
                for sl in range(n_slab):
                    rows = slice(ii * PEER_N_KEYS + sl * _SLAB, ii * PEER_N_KEYS + (sl + 1) * _SLAB)
                    hh = h_ref[slot, rows, lanes]
                    act = hh * (1.0 + lax.erf(hh * (1.0 / math.sqrt(2.0))))
                    a_ref[slot, rows, lanes] = act.astype(BF16) * g[sl]

    def pre_activations(slot):
        h_ref[slot] = jnp.dot(u_ref[slot * te:(slot + 1) * te, :], xn_ref[...], preferred_element_type=F32)

    def block_output(slot, vt_ref):
        return jnp.dot(vt_ref[...], a_ref[slot], preferred_element_type=F32)

    @pl.when(k == 0)
    def _():
        pre_activations(0)
        gate_stage(0, 0)
        pre_activations(1)
        acc_ref[...] = block_output(0, vtb_ref)

    @pl.when((k > 0) & (k < nk))
    def _():
        gate_stage(1, 2 * k - 1)
        pre_activations(0)
        acc_ref[...] += block_output(1, vta_ref)
        gate_stage(0, 2 * k)
        pre_activations(1)
        acc_ref[...] += block_output(0, vtb_ref)

    @pl.when(k == nk)
    def _():
        gate_stage(1, 2 * k - 1)
        o_ref[...] = res_ref[...] + (acc_ref[...] + block_output(1, vta_ref)).T


def peer_main(xn_t, u, vt, rk2, e2, n1, c1, res, layer, tm=512):
    d, t = xn_t.shape
    te = vt.shape[-1]
    nk = u.shape[1] // (2 * te)
    nh = PEER_HEADS
    once = pl.Buffered(1)
    stat = pl.BlockSpec((nh, PEER_N_KEYS, tm), lambda i, k: (0, 0, i), pipeline_mode=once)
    return pl.pallas_call(
        _peer_main_body,
        grid=(t // tm, nk + 1),
        in_specs=[
            pl.BlockSpec((d, tm), lambda i, k: (0, i), pipeline_mode=once),
            pl.BlockSpec((None, 2 * te, d), lambda i, k: (layer, jnp.minimum(k, nk - 1), 0)),
            pl.BlockSpec((None, None, d, te), lambda i, k: (layer, jnp.clip(2 * k - 1, 0, 2 * nk - 1), 0, 0)),
            pl.BlockSpec((None, None, d, te), lambda i, k: (layer, jnp.minimum(2 * k, 2 * nk - 1), 0, 0)),
            stat, stat, stat, stat,
            pl.BlockSpec((tm, d), lambda i, k: (i, 0), pipeline_mode=once),
        ],
        out_specs=pl.BlockSpec((tm, d), lambda i, k: (i, 0)),
        out_shape=jax.ShapeDtypeStruct((t, d), F32),
        scratch_shapes=[pltpu.VMEM((d, tm), F32), pltpu.VMEM((2, te, tm), F32), pltpu.VMEM((2, te, tm), BF16)],
        compiler_params=_cparams(("parallel", "arbitrary")),
        name="peer_main",
    )(xn_t, u, vt, vt, rk2, e2, n1, c1, res)


PEER_BLOCK = 512


def _transpose_block_body(v_ref, o_ref):
    o_ref[...] = v_ref[...].T.astype(BF16)


def peer_v_blocks(v):
    n_layers, n_e, d = v.shape
    te = PEER_BLOCK
    return pl.pallas_call(
        _transpose_block_body,
        grid=(n_layers, n_e // te),
        in_specs=[pl.BlockSpec((None, te, d), lambda l, e: (l, e, 0))],
        out_specs=pl.BlockSpec((None, None, d, te), lambda l, e: (l, e, 0, 0)),
        out_shape=jax.ShapeDtypeStruct((n_layers, n_e // te, d, te), BF16),
        compiler_params=_cparams(("parallel", "parallel")),
        name="peer_v_blocks",
    )(v)


def peer_weights(w_q, sub_keys, u, v):
    wq_t = jnp.swapaxes(w_q, 1, 2).astype(BF16)
    return wq_t, sub_keys.astype(BF16), u.astype(BF16), peer_v_blocks(v)


def peer_ffn(x, gain, weights, layer):
    wq_t, sk, u, vt = weights
    xn_t, rk2, e2, n1, c1 = peer_prep(x, gain, wq_t, sk, layer)
    return peer_main(xn_t, u, vt, rk2, e2, n1, c1, x, layer)


def _row128(v):
    return v.reshape(1, LANES).astype(F32)


def mla_mixer(x, gain, positions, w_in, g_qa, w_qb, g_kva, w_kvb, g_q, g_k, w_o, b, s):
    t = b * s
    nh = MLA_HEADS
    w_in_p = jnp.pad(w_in, ((0, 0), (0, 64))).astype(BF16)
    z = norm_matmul(x, w_in_p, gain=gain, tn=w_in_p.shape[1], name="mla_in")
    wq = w_qb.reshape(MLA_Q_RANK, nh, MLA_QK)
    wq = jnp.concatenate([wq[:, :, :MLA_NOPE].reshape(MLA_Q_RANK, -1), wq[:, :, MLA_NOPE:].reshape(MLA_Q_RANK, -1)], 1)

    scale = LOG2_E / math.sqrt(MLA_QK)
    inv_freq = ROPE_THETA ** (-jnp.arange(0, MLA_ROPE, 2, dtype=F32) / MLA_ROPE)
    sign = jnp.where((jnp.arange(LANES) % 64) < 32, -1.0, 1.0)
    rows = [
        _row128(jnp.tile(inv_freq, 4)),
        _row128(sign),
        _row128(g_q[:MLA_NOPE] * scale),
        _row128(jnp.tile(g_q[MLA_NOPE:], 2) * scale),
        _row128(g_k[:MLA_NOPE]),
        _row128(jnp.pad(g_k[MLA_NOPE:], (0, 64))),
    ]
    pos_col = positions.astype(F32).reshape(t, 1)
    qf, kf, vf = mla_qkv_prep(z, wq.astype(BF16), w_kvb.astype(BF16), g_qa, g_kva, pos_col, rows, b, s)
    o = attention(qf, kf, vf, fox=False)
    return norm_matmul(o, w_o.astype(BF16), residual=x, tn=w_o.shape[1], name="mla_out")


def fox_mixer(x, gain, w_in, b_f, g_q, g_k, w_o, b, s):
    nh = FOX_HEADS
    d = x.shape[1]
    zmain = norm_matmul(x, w_in.astype(BF16), gain=gain, n_cols=4 * d, tm=1024, out_dtype=BF16, name="fox_in")
    w_f = jnp.pad(w_in[:, 4 * d:], ((0, 0), (0, LANES - nh))).astype(BF16)
    flog = norm_matmul(x, w_f, gain=gain, tm=1024, name="fox_f")
    scale = LOG2_E / math.sqrt(FOX_HEAD_DIM)
    rows = [_row128(jnp.pad(b_f, (0, LANES - nh))), _row128(g_q * scale), _row128(g_k)]
    qf, kf, vf, ct = fox_prep(zmain, flog, rows, b, s)
    crow = ct[:, :nh, :]
    o = attention(qf, kf, vf, fox=True, ccol=crow[..., None], crow=crow, gate=zmain, gate_col0=3 * nh)
    return norm_matmul(o, w_o.astype(BF16), residual=x, tn=w_o.shape[1], name="fox_out")


def kernel(x, positions, norm_mix_g, norm_ffn_g, mla_w_in, mla_g_qa, mla_w_qb, mla_g_kva, mla_w_kvb, mla_g_q, mla_g_k, mla_w_o, fox_w_in, fox_b_f, fox_g_q, fox_g_k, fox_w_o, peer_w_q, peer_sub_keys, peer_u, peer_v):
    b, s, d = x.shape
    depth = norm_mix_g.shape[0]
    xt = x.reshape(b * s, d)
    peer_w = peer_weights(peer_w_q, peer_sub_keys, peer_u, peer_v)
    for i in range(depth):
        j = i // 2
        if i % 2 == 0:
            xt = mla_mixer(xt, norm_mix_g[i], positions, mla_w_in[j], mla_g_qa[j], mla_w_qb[j], mla_g_kva[j],
                           mla_w_kvb[j], mla_g_q[j], mla_g_k[j], mla_w_o[j], b, s)
        else:
            xt = fox_mixer(xt, norm_mix_g[i], fox_w_in[j], fox_b_f[j], fox_g_q[j], fox_g_k[j], fox_w_o[j], b, s)
        xt = peer_ffn(xt, norm_ffn_g[i], peer_w, i)
    return xt.reshape(b, s, d)
```

```python
import functools
import math

import jax
import jax.numpy as jnp
from jax import lax
from jax.experimental import pallas as pl
from jax.experimental.pallas import tpu as pltpu

F32 = jnp.float32
BF16 = jnp.bfloat16

RMS_EPS = 1e-6
NEG_INF = -1e30
LOG2_E = math.log2(math.e)
CHUNK_SHIFT = 6
ROPE_THETA = 10000.0

MLA_HEADS = 16
MLA_Q_RANK = 512
MLA_KV_RANK = 512
MLA_NOPE = 128
MLA_ROPE = 64
MLA_V = 128
MLA_QK = MLA_NOPE + MLA_ROPE

FOX_HEADS = 16
FOX_HEAD_DIM = 128

PEER_HEADS = 8
PEER_N_KEYS = 128
PEER_D_HALF = 128
PEER_TOPK = 16

LANES = 128
VMEM_LIMIT = 56 * 1024 * 1024


def _cparams(sem, flags=None):
    return pltpu.CompilerParams(dimension_semantics=sem, vmem_limit_bytes=VMEM_LIMIT, flags=flags)


def _nm_body(*refs, norm, has_res):
    it = iter(refs)
    a_ref = next(it)
    g_ref = next(it) if norm else None
    w_ref = next(it)
    r_ref = next(it) if has_res else None
    o_ref = next(it)
    an_ref = next(it) if norm else None

    if norm:
        @pl.when(pl.program_id(1) == 0)
        def _():
            a = a_ref[...].astype(F32)
            y = a * lax.rsqrt(jnp.mean(a * a, axis=-1, keepdims=True) + RMS_EPS)
            an_ref[...] = (y * g_ref[...]).astype(BF16)

        a_bf = an_ref[...]
    else:
        a_bf = a_ref[...]
    acc = jnp.dot(a_bf, w_ref[...], preferred_element_type=F32)
    if has_res:
        acc = acc + r_ref[...]
    o_ref[...] = acc.astype(o_ref.dtype)


def norm_matmul(a, w, *, gain=None, residual=None, a_col_block=0, n_cols=None, out_dtype=F32, tm=512, tn=512, name):
    m = a.shape[0]
    k = w.shape[0]
    n = w.shape[1] if n_cols is None else n_cols
    tn = min(tn, n)
    assert m % tm == 0 and n % tn == 0
    norm = gain is not None
    in_specs = [pl.BlockSpec((tm, k), lambda i, j: (i, a_col_block))]
    args = [a]
    if norm:
        in_specs.append(pl.BlockSpec((1, k), lambda i, j: (0, 0)))
        args.append(gain.reshape(1, k).astype(F32))
    in_specs.append(pl.BlockSpec((k, tn), lambda i, j: (0, j)))
    args.append(w)
    if residual is not None:
        in_specs.append(pl.BlockSpec((tm, tn), lambda i, j: (i, j)))
        args.append(residual)
    return pl.pallas_call(
        functools.partial(_nm_body, norm=norm, has_res=residual is not None),
        grid=(m // tm, n // tn),
        in_specs=in_specs,
        out_specs=pl.BlockSpec((tm, tn), lambda i, j: (i, j)),
        out_shape=jax.ShapeDtypeStruct((m, n), out_dtype),
        scratch_shapes=[pltpu.VMEM((tm, k), BF16)] if norm else [],
        compiler_params=_cparams(("parallel", "arbitrary")),
        name=name,
    )(*args)


def _swap_halves(x, first_half):
    return jnp.where(first_half, pltpu.roll(x, 96, 1), pltpu.roll(x, 32, 1))


def _mla_prep_body(q_ref, kv_ref, kr_ref, pos_ref, invf_ref, sign_ref, gqn_ref, gqr_ref, gkn_ref, gkr_ref,
                   qf_ref, kf_ref, vf_ref):
    ts = q_ref.shape[0]
    lane = lax.broadcasted_iota(jnp.int32, (ts, LANES), 1)
    first_half = (lane & 63) < 32
    low64 = lane < 64
    ang = pos_ref[...] * invf_ref[...]
    cosv = jnp.cos(ang)
    sinv = jnp.sin(ang) * sign_ref[...]

    def rot(x):
        return x * cosv + _swap_halves(x, first_half) * sinv

    inv_d = 1.0 / MLA_QK
    zeros = jnp.zeros((ts, LANES), F32)

    kr = kr_ref[...]
    ss_kr = jnp.sum(kr * kr, axis=-1, keepdims=True)
    kr_rot = rot(kr * gkr_ref[...])
    for h in range(MLA_HEADS):
        kn = kv_ref[:, h * 256:h * 256 + 128].astype(F32)
        r = lax.rsqrt((jnp.sum(kn * kn, axis=-1, keepdims=True) + ss_kr) * inv_d + RMS_EPS)
        kf_ref[0, h, :, 0:128] = (kn * r * gkn_ref[...]).astype(BF16)
        kf_ref[0, h, :, 128:256] = (kr_rot * r).astype(BF16)
        vf_ref[0, h] = kv_ref[:, h * 256 + 128:(h + 1) * 256].astype(BF16)

    for p in range(MLA_HEADS // 2):
        xr = q_ref[:, 2048 + p * 128:2048 + (p + 1) * 128].astype(F32)
        xr2 = xr * xr
        ss_e = jnp.sum(jnp.where(low64, xr2, 0.0), axis=-1, keepdims=True)
        ss_o = jnp.sum(jnp.where(low64, 0.0, xr2), axis=-1, keepdims=True)
        qn_e = q_ref[:, (2 * p) * 128:(2 * p + 1) * 128].astype(F32)
        qn_o = q_ref[:, (2 * p + 1) * 128:(2 * p + 2) * 128].astype(F32)
        r_e = lax.rsqrt((jnp.sum(qn_e * qn_e, axis=-1, keepdims=True) + ss_e) * inv_d + RMS_EPS)
        r_o = lax.rsqrt((jnp.sum(qn_o * qn_o, axis=-1, keepdims=True) + ss_o) * inv_d + RMS_EPS)
        xr_rot = rot(xr * jnp.where(low64, r_e, r_o) * gqr_ref[...])
        qf_ref[0, 2 * p, :, 0:128] = (qn_e * r_e * gqn_ref[...]).astype(BF16)
        qf_ref[0, 2 * p, :, 128:256] = jnp.where(low64, xr_rot, zeros).astype(BF16)
        qf_ref[0, 2 * p + 1, :, 0:128] = (qn_o * r_o * gqn_ref[...]).astype(BF16)
        qf_ref[0, 2 * p + 1, :, 128:256] = jnp.where(low64, pltpu.roll(xr_rot, 64, 1), zeros).astype(BF16)


def _mla_qkv_body(cq_ref, ckv_ref, gqa_ref, gkva_ref, wq_ref, wkv_ref, kr_ref, pos_ref, *rest):
    vec_refs, (qf_ref, kf_ref, vf_ref, q_s, kv_s) = rest[:6], rest[6:]

    def up_project(c_ref, g_ref, w_ref):
        c = c_ref[...]
        y = c * lax.rsqrt(jnp.mean(c * c, axis=-1, keepdims=True) + RMS_EPS)
        return jnp.dot((y * g_ref[...]).astype(BF16), w_ref[...], preferred_element_type=F32)

    q_s[...] = up_project(cq_ref, gqa_ref, wq_ref)
    kv_s[...] = up_project(ckv_ref, gkva_ref, wkv_ref)
    _mla_prep_body(q_s, kv_s, kr_ref, pos_ref, *vec_refs, qf_ref, kf_ref, vf_ref)


def mla_qkv_prep(z, wq, wkv, g_qa, g_kva, pos_col, rows, b, s, ts=256):
    ns = s // ts
    h = MLA_HEADS
    row = lambda blk: (lambda bi, si: (bi * ns + si, blk))
    vec = pl.BlockSpec((1, LANES), lambda bi, si: (0, 0))
    gain = lambda r: pl.BlockSpec((1, r), lambda bi, si: (0, 0))
    whole = lambda w: pl.BlockSpec(w.shape, lambda bi, si: (0, 0))
    head_out = lambda d: pl.BlockSpec((1, h, ts, d), lambda bi, si: (bi, 0, si, 0))
    return pl.pallas_call(
        _mla_qkv_body,
        grid=(b, ns),
        in_specs=[
            pl.BlockSpec((ts, MLA_Q_RANK), row(0)),
            pl.BlockSpec((ts, MLA_KV_RANK), row(MLA_Q_RANK // MLA_KV_RANK)),
            gain(MLA_Q_RANK), gain(MLA_KV_RANK), whole(wq), whole(wkv),
            pl.BlockSpec((ts, LANES), row((MLA_Q_RANK + MLA_KV_RANK) // LANES)),
            pl.BlockSpec((ts, 1), row(0)),
        ] + [vec] * 6,
        out_specs=[head_out(256), head_out(256), head_out(128)],
        out_shape=[
            jax.ShapeDtypeStruct((b, h, s, 256), BF16),
            jax.ShapeDtypeStruct((b, h, s, 256), BF16),
            jax.ShapeDtypeStruct((b, h, s, 128), BF16),
        ],
        scratch_shapes=[pltpu.VMEM((ts, wq.shape[1]), F32), pltpu.VMEM((ts, wkv.shape[1]), F32)],
        compiler_params=_cparams(("parallel", "parallel")),
        name="mla_qkv_prep",
    )(z, z, g_qa.reshape(1, -1).astype(F32), g_kva.reshape(1, -1).astype(F32), wq, wkv, z, pos_col, *rows)


def _split3(x):
    hi = x.astype(BF16)
    r1 = x - hi.astype(F32)
    mid = r1.astype(BF16)
    lo = (r1 - mid.astype(F32)).astype(BF16)
    return hi, mid, lo


def _fox_prep_body(z_ref, f_ref, bf_ref, gq_ref, gk_ref, qf_ref, kf_ref, vf_ref, ct_ref, carry_ref):
    ts = z_ref.shape[0]
    d = FOX_HEAD_DIM
    nh = FOX_HEADS

    @pl.when(pl.program_id(1) == 0)
    def _():
        carry_ref[...] = jnp.zeros_like(carry_ref)

    x = f_ref[...] + bf_ref[...]
    logf = jnp.minimum(x, 0.0) - jnp.log1p(jnp.exp(-jnp.abs(x)))
    tri = (lax.broadcasted_iota(jnp.int32, (ts, ts), 0) >= lax.broadcasted_iota(jnp.int32, (ts, ts), 1)).astype(BF16)
    hi, mid, lo = _split3(logf)
    local = (jnp.dot(tri, hi, preferred_element_type=F32) + jnp.dot(tri, mid, preferred_element_type=F32)
             + jnp.dot(tri, lo, preferred_element_type=F32))
    c = carry_ref[0:1, :] + local
    carry_ref[0:1, :] = c[ts - 1:ts, :]
    ct_ref[0] = (c * LOG2_E).T

    inv_d = 1.0 / d
    for h in range(nh):
        q = z_ref[:, h * d:(h + 1) * d].astype(F32)
        k = z_ref[:, (nh + h) * d:(nh + h + 1) * d].astype(F32)
        rq = lax.rsqrt(jnp.sum(q * q, axis=-1, keepdims=True) * inv_d + RMS_EPS)
        rk = lax.rsqrt(jnp.sum(k * k, axis=-1, keepdims=True) * inv_d + RMS_EPS)
        qf_ref[0, h] = (q * rq * gq_ref[...]).astype(BF16)
        kf_ref[0, h] = (k * rk * gk_ref[...]).astype(BF16)
        vf_ref[0, h] = z_ref[:, (2 * nh + h) * d:(2 * nh + h + 1) * d]


def fox_prep(zmain, flog, rows, b, s, ts=256):
    ns = s // ts
    h = FOX_HEADS
    row = lambda bi, si: (bi * ns + si, 0)
    vec = pl.BlockSpec((1, LANES), lambda bi, si: (0, 0))
    head_out = pl.BlockSpec((1, h, ts, 128), lambda bi, si: (bi, 0, si, 0))
    return pl.pallas_call(
        _fox_prep_body,
        grid=(b, ns),
        in_specs=[pl.BlockSpec((ts, zmain.shape[1]), row), pl.BlockSpec((ts, LANES), row), vec, vec, vec],
        out_specs=[head_out, head_out, head_out, pl.BlockSpec((1, LANES, ts), lambda bi, si: (bi, 0, si))],
        out_shape=[jax.ShapeDtypeStruct((b, h, s, 128), BF16)] * 3 + [jax.ShapeDtypeStruct((b, LANES, s), F32)],
        scratch_shapes=[pltpu.VMEM((8, LANES), F32)],
        compiler_params=_cparams(("parallel", "arbitrary")),
        name="fox_prep",
    )(zmain, flog, *rows)


_ATTN_HEADS = 2


def _attn_body(*refs, fox, tq, nq):
    if fox:
        q_ref, k_ref, v_ref, ccol_ref, crow_ref, gate_ref, o_ref, s_ref, p_ref = refs
    else:
        q_ref, k_ref, v_ref, o_ref, s_ref, p_ref = refs
    nt = (((1,), (1,)), ((), ()))
    half = tq // 2
    dv = v_ref.shape[-1]
    r = lax.broadcasted_iota(jnp.int32, (tq, tq), 0)
    c = lax.broadcasted_iota(jnp.int32, (tq, tq), 1)
    allowed = (c <= r) if fox else ((c >> CHUNK_SHIFT) <= (r >> CHUNK_SHIFT))
    for i in range(nq):
        for hh in range(_ATTN_HEADS):
            buf = 2 * hh + i % 2
            rows = slice(i * tq, (i + 1) * tq)
            q = q_ref[0, hh, rows, :]
            mx = jnp.full((tq, half), NEG_INF, F32)
            for j in range(i + 1):
                cols = slice(j * tq, (j + 1) * tq)
                s = lax.dot_general(q, k_ref[0, hh, cols, :], nt, preferred_element_type=F32)
                if fox:
                    s = s + (ccol_ref[0, hh, rows, :] - crow_ref[0, hh, :, cols])
                if j == i:
                    s = jnp.where(allowed, s, NEG_INF)
                s_ref[buf, :, cols] = s
                mx = jnp.maximum(mx, jnp.maximum(s[:, :half], s[:, half:]))
            m = jnp.max(mx, axis=-1, keepdims=True)
            ps = jnp.zeros((tq, half), F32)
            for j in range(i + 1):
                cols = slice(j * tq, (j + 1) * tq)
                p = jnp.exp2(s_ref[buf, :, cols] - m)
                ps = ps + (p[:, :half] + p[:, half:])
                p_ref[buf, :, cols] = p.astype(BF16)
            l = jnp.sum(ps, axis=-1, keepdims=True)
            n_keys = (i + 1) * tq
            o = jnp.dot(p_ref[buf, :, :n_keys], v_ref[0, hh, :n_keys, :], preferred_element_type=F32) / l
            out_cols = slice(hh * dv, (hh + 1) * dv)
            if fox:
                o = o * (1.0 / (1.0 + jnp.exp(-gate_ref[rows, out_cols].astype(F32))))
            o_ref[rows, out_cols] = o.astype(o_ref.dtype)


def attention(qf, kf, vf, *, fox, ccol=None, crow=None, gate=None, gate_col0=0, tq=256):
    b, h, s, dk = qf.shape
    dv = vf.shape[-1]
    nq = s // tq
    hps = _ATTN_HEADS
    head = lambda d: pl.BlockSpec((1, hps, s, d), lambda bi, hi: (bi, hi, 0, 0))
    in_specs = [head(dk), head(dk), head(dv)]
    args = [qf, kf, vf]
    if fox:
        in_specs += [
            head(1),
            pl.BlockSpec((1, hps, 1, s), lambda bi, hi: (bi, hi, 0, 0)),
            pl.BlockSpec((s, hps * dv), lambda bi, hi: (bi, gate_col0 // hps + hi)),
        ]
        args += [ccol, crow.reshape(b, h, 1, s), gate]
    return pl.pallas_call(
        functools.partial(_attn_body, fox=fox, tq=tq, nq=nq),
        grid=(b, h // hps),
        in_specs=in_specs,
        out_specs=pl.BlockSpec((s, hps * dv), lambda bi, hi: (bi, hi)),
        out_shape=jax.ShapeDtypeStruct((b * s, h * dv), BF16),
        scratch_shapes=[pltpu.VMEM((2 * hps, tq, s), F32), pltpu.VMEM((2 * hps, tq, s), BF16)],
        compiler_params=_cparams(("parallel", "parallel")),
        name="fox_attn" if fox else "mla_attn",
    )(*args)


_CAND_PAIRS = [(p, q) for p in range(PEER_TOPK) for q in range(PEER_TOPK) if (p + 1) * (q + 1) <= PEER_TOPK]
_CAND_ROWS = 56


def _top_rows(v, n, want_rank=False, exact=False):
    row_id = lax.broadcasted_iota(jnp.int32, v.shape, 0).astype(F32) if exact else None
    rank = jnp.full(v.shape, float(n), F32)
    out = []
    for r in range(n):
        m = jnp.max(v, axis=0, keepdims=True)
        out.append(m)
        hit = v == m
        if exact:
            hit = row_id == jnp.min(jnp.where(hit, row_id, float(v.shape[0])), axis=0, keepdims=True)
        if want_rank:
            rank = jnp.where(hit, float(r), rank)
        if r + 1 < n or exact:
            v = jnp.where(hit, -jnp.inf, v)
    return (out,) + ((rank,) if want_rank else ()) + ((v,) if exact else ())


def _count_ge(v, row):
    return jnp.sum(jnp.where(v >= row, 1.0, 0.0), axis=0, keepdims=True)


def _select_lane_group(s1, s2, cand_ref, exact):
    k = PEER_TOPK
    if exact:
        a1, rank1 = _top_rows(s1, k, want_rank=True, exact=True)[:2]
        a2, rank2 = _top_rows(s2, k, want_rank=True, exact=True)[:2]
    else:
        a1, = _top_rows(s1, k)
        a2, rank2 = _top_rows(s2, k, want_rank=True)
    for r, (p, q) in enumerate(_CAND_PAIRS):
        cand_ref[r:r + 1, :] = a1[p] + a2[q]
    cand = cand_ref[...]
    if exact:
        best, left = _top_rows(cand, k, exact=True)
        cand_ref[...] = jnp.where(left != cand, 1.0, 0.0)
    else:
        best, = _top_rows(cand, k)
    tau = best[k - 1]
    z = jnp.ones_like(tau)
    for r in range(1, k):
        z = z + jnp.exp(best[r] - best[0])
    n1 = jnp.zeros(s1.shape, F32)
    r = 0
    for p in range(k):
        n_p = jnp.zeros_like(tau)
        for q in range(k // (p + 1)):
            n_p = n_p + (cand_ref[r:r + 1, :] if exact else jnp.where(a1[p] + a2[q] >= tau, 1.0, 0.0))
            r += 1
        n1 = jnp.where((rank1 == float(p)) if exact else (s1 == a1[p]), n_p, n1)
    stats = (rank2.astype(BF16), jnp.exp(s2 - a2[0]).astype(BF16), n1, 0.5 * jnp.exp(s1 - a1[0]) / z)
    if exact:
        return stats
    ties = jnp.maximum(jnp.maximum(_count_ge(s1, a1[k - 1]), _count_ge(s2, a2[k - 1])), _count_ge(cand, tau)) > k
    return stats + (ties,)


def _peer_prep_body(x_ref, g_ref, wq_ref, sk_ref, xn_ref, rk2_ref, e2_ref, n1_ref, c1_ref, s_ref, cand_ref):
    @pl.when(pl.program_id(1) == 0)
    def _():
        a = x_ref[...]
        y = a * lax.rsqrt(jnp.mean(a * a, axis=-1, keepdims=True) + RMS_EPS)
        xn_ref[...] = (y * g_ref[...]).T.astype(BF16)

    qt = jnp.dot(wq_ref[...], xn_ref[...], preferred_element_type=F32)
    s_ref[0] = jnp.dot(sk_ref[0], qt[0:128].astype(BF16), preferred_element_type=F32)
    s_ref[1] = jnp.dot(sk_ref[1], qt[128:256].astype(BF16), preferred_element_type=F32)
    cand_ref[...] = jnp.full(cand_ref.shape, -jnp.inf, F32)

    def lane_groups(c, carry):
        def column(g):
            lanes = pl.ds(pl.multiple_of((2 * c + g) * LANES, LANES), LANES)
            slot = cand_ref.at[g]

            def store(rank2, e2, n1, c1):
                rk2_ref[0, :, lanes] = rank2
                e2_ref[0, :, lanes] = e2
                n1_ref[0, :, lanes] = n1
                c1_ref[0, :, lanes] = c1

            def select(exact):
                return _select_lane_group(s_ref[0, :, lanes], s_ref[1, :, lanes], slot, exact)

            *stats, ties = select(exact=False)
            store(*stats)

            def redo():
                store(*select(exact=True))
                slot[...] = jnp.full(slot.shape, -jnp.inf, F32)

            return jnp.max(jnp.where(ties, 1.0, 0.0)) > 0.0, redo

        fixes = [column(g) for g in range(2)]
        for tied, redo in fixes:
            pl.when(tied)(redo)
        return carry

    lax.fori_loop(0, s_ref.shape[-1] // (2 * LANES), lane_groups, 0)


def peer_prep(x, gain, wq_t, sk, layer, tm=512):
    t, d = x.shape
    nh = PEER_HEADS
    stat = pl.BlockSpec((1, PEER_N_KEYS, tm), lambda i, h: (h, 0, i))
    stat_f32 = jax.ShapeDtypeStruct((nh, PEER_N_KEYS, t), F32)
    stat_bf16 = jax.ShapeDtypeStruct((nh, PEER_N_KEYS, t), BF16)
    return pl.pallas_call(
        _peer_prep_body,
        grid=(t // tm, nh),
        in_specs=[
            pl.BlockSpec((tm, d), lambda i, h: (i, 0)),
            pl.BlockSpec((1, d), lambda i, h: (0, 0)),
            pl.BlockSpec((None, 2 * PEER_D_HALF, d), lambda i, h: (layer, h, 0)),
            pl.BlockSpec((None, 2, PEER_N_KEYS, PEER_D_HALF), lambda i, h: (layer, 0, 0, 0)),
        ],
        out_specs=[pl.BlockSpec((d, tm), lambda i, h: (0, i)), stat, stat, stat, stat],
        out_shape=[jax.ShapeDtypeStruct((d, t), BF16), stat_bf16, stat_bf16, stat_f32, stat_f32],
        scratch_shapes=[pltpu.VMEM((2, PEER_N_KEYS, tm), F32), pltpu.VMEM((2, _CAND_ROWS, LANES), F32)],
        compiler_params=_cparams(("parallel", "arbitrary")),
        name="peer_prep",
    )(x, gain.reshape(1, d).astype(F32), wq_t, sk)


_SLAB = 16
_GATE_LANES = 256


def _peer_main_body(xn_ref, u_ref, vta_ref, vtb_ref, rk2_ref, e2_ref, n1_ref, c1_ref, res_ref, o_ref,
                    acc_ref, h_ref, a_ref):
    k = pl.program_id(1)
    nk = pl.num_programs(1) - 1
    te, tm = h_ref.shape[1:]
    n_i = te // PEER_N_KEYS
    n_slab = PEER_N_KEYS // _SLAB
    zero = jnp.zeros((_SLAB, _GATE_LANES), BF16)

    def gate_stage(slot, blk):
        for lc in range(tm // _GATE_LANES):
            lanes = slice(lc * _GATE_LANES, (lc + 1) * _GATE_LANES)
            for ii in range(n_i):
                i = blk * n_i + ii
                g = [None] * n_slab
                for h in range(PEER_HEADS):
                    nb = jnp.broadcast_to(n1_ref[h, pl.ds(i, 1), lanes], (_SLAB, _GATE_LANES)).astype(BF16)
                    cb = jnp.broadcast_to(c1_ref[h, pl.ds(i, 1), lanes], (_SLAB, _GATE_LANES)).astype(BF16)
                    for sl in range(n_slab):
                        rows = slice(sl * _SLAB, (sl + 1) * _SLAB)
                        w = jnp.where(rk2_ref[h, rows, lanes] < nb, e2_ref[h, rows, lanes] * cb, zero)
                        g[sl] = w if g[sl] is None else g[sl] + w
                for sl in range(n_slab):
                    rows = slice(ii * PEER_N_KEYS + sl * _SLAB, ii * PEER_N_KEYS + (sl + 1) * _SLAB)
                    hh = h_ref[slot, rows, lanes]
                    act = hh * (1.0 + lax.erf(hh * (1.0 / math.sqrt(2.0))))
                    a_ref[slot, rows, lanes] = act.astype(BF16) * g[sl]

    def pre_activations(slot):
        h_ref[slot] = jnp.dot(u_ref[slot * te:(slot + 1) * te, :].astype(BF16), xn_ref[...], preferred_element_type=F32)

    def block_output(slot, vt_ref):
        return jnp.dot(vt_ref[...], a_ref[slot], preferred_element_type=F32)

    @pl.when(k == 0)
    def _():
        pre_activations(0)
        gate_stage(0, 0)
        pre_activations(1)
        acc_ref[...] = block_output(0, vtb_ref)

    @pl.when((k > 0) & (k < nk))
    def _():
        gate_stage(1, 2 * k - 1)
        pre_activations(0)
        acc_ref[...] += block_output(1, vta_ref)
        gate_stage(0, 2 * k)
        pre_activations(1)
        acc_ref[...] += block_output(0, vtb_ref)

    @pl.when(k == nk)
    def _():
        gate_stage(1, 2 * k - 1)
        o_ref[...] = res_ref[...] + (acc_ref[...] + block_output(1, vta_ref)).T


def peer_main(xn_t, u, vt, rk2, e2, n1, c1, res, layer, tm=512):
    d, t = xn_t.shape
    te = vt.shape[-1]
    nk = u.shape[1] // (2 * te)
    nh = PEER_HEADS
    once = pl.Buffered(1)
    stat = pl.BlockSpec((nh, PEER_N_KEYS, tm), lambda i, k: (0, 0, i), pipeline_mode=once)
    return pl.pallas_call(
        _peer_main_body,
        grid=(t // tm, nk + 1),
        in_specs=[
            pl.BlockSpec((d, tm), lambda i, k: (0, i), pipeline_mode=once),
            pl.BlockSpec((None, 2 * te, d), lambda i, k: (layer, jnp.minimum(k, nk - 1), 0)),
            pl.BlockSpec((None, None, d, te), lambda i, k: (layer, jnp.clip(2 * k - 1, 0, 2 * nk - 1), 0, 0)),
            pl.BlockSpec((None, None, d, te), lambda i, k: (layer, jnp.minimum(2 * k, 2 * nk - 1), 0, 0)),
            stat, stat, stat, stat,
            pl.BlockSpec((tm, d), lambda i, k: (i, 0), pipeline_mode=once),
        ],
        out_specs=pl.BlockSpec((tm, d), lambda i, k: (i, 0)),
        out_shape=jax.ShapeDtypeStruct((t, d), F32),
        scratch_shapes=[pltpu.VMEM((d, tm), F32), pltpu.VMEM((2, te, tm), F32), pltpu.VMEM((2, te, tm), BF16)],
        compiler_params=_cparams(("parallel", "arbitrary")),
        name="peer_main",
    )(xn_t, u, vt, vt, rk2, e2, n1, c1, res)


PEER_BLOCK = 512


def _transpose_block_body(v_ref, o_ref):
    o_ref[...] = v_ref[...].T.astype(BF16)


def peer_v_blocks(v):
    n_layers, n_e, d = v.shape
    te = PEER_BLOCK
    return pl.pallas_call(
        _transpose_block_body,
        grid=(n_layers, n_e // te),
        in_specs=[pl.BlockSpec((None, te, d), lambda l, e: (l, e, 0))],
        out_specs=pl.BlockSpec((None, None, d, te), lambda l, e: (l, e, 0, 0)),
        out_shape=jax.ShapeDtypeStruct((n_layers, n_e // te, d, te), BF16),
        compiler_params=_cparams(("parallel", "parallel")),
        name="peer_v_blocks",
    )(v)


def peer_weights(w_q, sub_keys, u, v):
    wq_t = jnp.swapaxes(w_q, 1, 2).astype(BF16)
    return wq_t, sub_keys.astype(BF16), u, peer_v_blocks(v)


def peer_ffn(x, gain, weights, layer):
    wq_t, sk, u, vt = weights
    xn_t, rk2, e2, n1, c1 = peer_prep(x, gain, wq_t, sk, layer)
    return peer_main(xn_t, u, vt, rk2, e2, n1, c1, x, layer)


def _row128(v):
    return v.reshape(1, LANES).astype(F32)


def mla_mixer(x, gain, positions, w_in, g_qa, w_qb, g_kva, w_kvb, g_q, g_k, w_o, b, s):
    t = b * s
    nh = MLA_HEADS
    w_in_p = jnp.pad(w_in, ((0, 0), (0, 64))).astype(BF16)
    z = norm_matmul(x, w_in_p, gain=gain, tn=w_in_p.shape[1], name="mla_in")
    wq = w_qb.reshape(MLA_Q_RANK, nh, MLA_QK)
    wq = jnp.concatenate([wq[:, :, :MLA_NOPE].reshape(MLA_Q_RANK, -1), wq[:, :, MLA_NOPE:].reshape(MLA_Q_RANK, -1)], 1)

    scale = LOG2_E / math.sqrt(MLA_QK)
    inv_freq = ROPE_THETA ** (-jnp.arange(0, MLA_ROPE, 2, dtype=F32) / MLA_ROPE)
    sign = jnp.where((jnp.arange(LANES) % 64) < 32, -1.0, 1.0)
    rows = [
        _row128(jnp.tile(inv_freq, 4)),
        _row128(sign),
        _row128(g_q[:MLA_NOPE] * scale),
        _row128(jnp.tile(g_q[MLA_NOPE:], 2) * scale),
        _row128(g_k[:MLA_NOPE]),
        _row128(jnp.pad(g_k[MLA_NOPE:], (0, 64))),
    ]
    pos_col = positions.astype(F32).reshape(t, 1)
    qf, kf, vf = mla_qkv_prep(z, wq.astype(BF16), w_kvb.astype(BF16), g_qa, g_kva, pos_col, rows, b, s)
    o = attention(qf, kf, vf, fox=False)
    return norm_matmul(o, w_o.astype(BF16), residual=x, tn=w_o.shape[1], name="mla_out")


def fox_mixer(x, gain, w_in, b_f, g_q, g_k, w_o, b, s):
    nh = FOX_HEADS
    d = x.shape[1]
    zmain = norm_matmul(x, w_in.astype(BF16), gain=gain, n_cols=4 * d, tm=1024, out_dtype=BF16, name="fox_in")
    w_f = jnp.pad(w_in[:, 4 * d:], ((0, 0), (0, LANES - nh))).astype(BF16)
    flog = norm_matmul(x, w_f, gain=gain, tm=1024, name="fox_f")
    scale = LOG2_E / math.sqrt(FOX_HEAD_DIM)
    rows = [_row128(jnp.pad(b_f, (0, LANES - nh))), _row128(g_q * scale), _row128(g_k)]
    qf, kf, vf, ct = fox_prep(zmain, flog, rows, b, s)
    crow = ct[:, :nh, :]
    o = attention(qf, kf, vf, fox=True, ccol=crow[..., None], crow=crow, gate=zmain, gate_col0=3 * nh)
    return norm_matmul(o, w_o.astype(BF16), residual=x, tn=w_o.shape[1], name="fox_out")


def kernel(x, positions, norm_mix_g, norm_ffn_g, mla_w_in, mla_g_qa, mla_w_qb, mla_g_kva, mla_w_kvb, mla_g_q, mla_g_k, mla_w_o, fox_w_in, fox_b_f, fox_g_q, fox_g_k, fox_w_o, peer_w_q, peer_sub_keys, peer_u, peer_v):
    b, s, d = x.shape
    depth = norm_mix_g.shape[0]
    xt = x.reshape(b * s, d)
    peer_w = peer_weights(peer_w_q, peer_sub_keys, peer_u, peer_v)
    for i in range(depth):
        j = i // 2
        if i % 2 == 0:
            xt = mla_mixer(xt, norm_mix_g[i], positions, mla_w_in[j], mla_g_qa[j], mla_w_qb[j], mla_g_kva[j],
                           mla_w_kvb[j], mla_g_q[j], mla_g_k[j], mla_w_o[j], b, s)
        else:
            xt = fox_mixer(xt, norm_mix_g[i], fox_w_in[j], fox_b_f[j], fox_g_q[j], fox_g_k[j], fox_w_o[j], b, s)
        xt = peer_ffn(xt, norm_ffn_g[i], peer_w, i)
    return xt.reshape(b, s, d)
```

```python
import functools
import math

import jax
import jax.numpy as jnp
from jax import lax
from jax.experimental import pallas as pl
from jax.experimental.pallas import tpu as pltpu

F32 = jnp.float32
BF16 = jnp.bfloat16

RMS_EPS = 1e-6
NEG_INF = -1e30
LOG2_E = math.log2(math.e)
CHUNK_SHIFT = 6
ROPE_THETA = 10000.0

MLA_HEADS = 16
MLA_Q_RANK = 512
MLA_KV_RANK = 512
MLA_NOPE = 128
MLA_ROPE = 64
MLA_V = 128
MLA_QK = MLA_NOPE + MLA_ROPE

FOX_HEADS = 16
FOX_HEAD_DIM = 128

PEER_HEADS = 8
PEER_N_KEYS = 128
PEER_D_HALF = 128
PEER_TOPK = 16

LANES = 128
VMEM_LIMIT = 56 * 1024 * 1024


def _cparams(sem, flags=None):
    return pltpu.CompilerParams(dimension_semantics=sem, vmem_limit_bytes=VMEM_LIMIT, flags=flags)


def _nm_body(*refs, norm, has_res):
    it = iter(refs)
    a_ref = next(it)
    g_ref = next(it) if norm else None
    w_ref = next(it)
    r_ref = next(it) if has_res else None
    o_ref = next(it)
    an_ref = next(it) if norm else None

    if norm:
        @pl.when(pl.program_id(1) == 0)
        def _():
            a = a_ref[...].astype(F32)
            y = a * lax.rsqrt(jnp.mean(a * a, axis=-1, keepdims=True) + RMS_EPS)
            an_ref[...] = (y * g_ref[...]).astype(BF16)

        a_bf = an_ref[...]
    else:
        a_bf = a_ref[...]
    acc = jnp.dot(a_bf, w_ref[...], preferred_element_type=F32)
    if has_res:
        acc = acc + r_ref[...]
    o_ref[...] = acc.astype(o_ref.dtype)


def norm_matmul(a, w, *, gain=None, residual=None, a_col_block=0, n_cols=None, out_dtype=F32, tm=512, tn=512, name):
    m = a.shape[0]
    k = w.shape[0]
    n = w.shape[1] if n_cols is None else n_cols
    tn = min(tn, n)
    assert m % tm == 0 and n % tn == 0
    norm = gain is not None
    in_specs = [pl.BlockSpec((tm, k), lambda i, j: (i, a_col_block))]
    args = [a]
    if norm:
        in_specs.append(pl.BlockSpec((1, k), lambda i, j: (0, 0)))
        args.append(gain.reshape(1, k).astype(F32))
    in_specs.append(pl.BlockSpec((k, tn), lambda i, j: (0, j)))
    args.append(w)
    if residual is not None:
        in_specs.append(pl.BlockSpec((tm, tn), lambda i, j: (i, j)))
        args.append(residual)
    return pl.pallas_call(
        functools.partial(_nm_body, norm=norm, has_res=residual is not None),
        grid=(m // tm, n // tn),
        in_specs=in_specs,
        out_specs=pl.BlockSpec((tm, tn), lambda i, j: (i, j)),
        out_shape=jax.ShapeDtypeStruct((m, n), out_dtype),
        scratch_shapes=[pltpu.VMEM((tm, k), BF16)] if norm else [],
        compiler_params=_cparams(("parallel", "arbitrary")),
        name=name,
    )(*args)


def _swap_halves(x, first_half):
    return jnp.where(first_half, pltpu.roll(x, 96, 1), pltpu.roll(x, 32, 1))


def _mla_prep_body(q_ref, kv_ref, kr_ref, pos_ref, invf_ref, sign_ref, gqn_ref, gqr_ref, gkn_ref, gkr_ref,
                   qf_ref, kf_ref, vf_ref):
    ts = q_ref.shape[0]
    lane = lax.broadcasted_iota(jnp.int32, (ts, LANES), 1)
    first_half = (lane & 63) < 32
    low64 = lane < 64
    ang = pos_ref[...] * invf_ref[...]
    cosv = jnp.cos(ang)
    sinv = jnp.sin(ang) * sign_ref[...]

    def rot(x):
        return x * cosv + _swap_halves(x, first_half) * sinv

    inv_d = 1.0 / MLA_QK
    zeros = jnp.zeros((ts, LANES), F32)

    kr = kr_ref[...]
    ss_kr = jnp.sum(kr * kr, axis=-1, keepdims=True)
    kr_rot = rot(kr * gkr_ref[...])
    for h in range(MLA_HEADS):
        kn = kv_ref[:, h * 256:h * 256 + 128].astype(F32)
        r = lax.rsqrt((jnp.sum(kn * kn, axis=-1, keepdims=True) + ss_kr) * inv_d + RMS_EPS)
        kf_ref[0, h, :, 0:128] = (kn * r * gkn_ref[...]).astype(BF16)
        kf_ref[0, h, :, 128:256] = (kr_rot * r).astype(BF16)
        vf_ref[0, h] = kv_ref[:, h * 256 + 128:(h + 1) * 256].astype(BF16)

    for p in range(MLA_HEADS // 2):
        xr = q_ref[:, 2048 + p * 128:2048 + (p + 1) * 128].astype(F32)
        xr2 = xr * xr
        ss_e = jnp.sum(jnp.where(low64, xr2, 0.0), axis=-1, keepdims=True)
        ss_o = jnp.sum(jnp.where(low64, 0.0, xr2), axis=-1, keepdims=True)
        qn_e = q_ref[:, (2 * p) * 128:(2 * p + 1) * 128].astype(F32)
        qn_o = q_ref[:, (2 * p + 1) * 128:(2 * p + 2) * 128].astype(F32)
        r_e = lax.rsqrt((jnp.sum(qn_e * qn_e, axis=-1, keepdims=True) + ss_e) * inv_d + RMS_EPS)
        r_o = lax.rsqrt((jnp.sum(qn_o * qn_o, axis=-1, keepdims=True) + ss_o) * inv_d + RMS_EPS)
        xr_rot = rot(xr * jnp.where(low64, r_e, r_o) * gqr_ref[...])
        qf_ref[0, 2 * p, :, 0:128] = (qn_e * r_e * gqn_ref[...]).astype(BF16)
        qf_ref[0, 2 * p, :, 128:256] = jnp.where(low64, xr_rot, zeros).astype(BF16)
        qf_ref[0, 2 * p + 1, :, 0:128] = (qn_o * r_o * gqn_ref[...]).astype(BF16)
        qf_ref[0, 2 * p + 1, :, 128:256] = jnp.where(low64, pltpu.roll(xr_rot, 64, 1), zeros).astype(BF16)


def _mla_qkv_body(cq_ref, ckv_ref, gqa_ref, gkva_ref, wq_ref, wkv_ref, kr_ref, pos_ref, *rest):
    vec_refs, (qf_ref, kf_ref, vf_ref, q_s, kv_s) = rest[:6], rest[6:]

    def up_project(c_ref, g_ref, w_ref):
        c = c_ref[...]
        y = c * lax.rsqrt(jnp.mean(c * c, axis=-1, keepdims=True) + RMS_EPS)
        return jnp.dot((y * g_ref[...]).astype(BF16), w_ref[...], preferred_element_type=F32)

    q_s[...] = up_project(cq_ref, gqa_ref, wq_ref)
    kv_s[...] = up_project(ckv_ref, gkva_ref, wkv_ref)
    _mla_prep_body(q_s, kv_s, kr_ref, pos_ref, *vec_refs, qf_ref, kf_ref, vf_ref)


def mla_qkv_prep(z, wq, wkv, g_qa, g_kva, pos_col, rows, b, s, ts=256):
    ns = s // ts
    h = MLA_HEADS
    row = lambda blk: (lambda bi, si: (bi * ns + si, blk))
    vec = pl.BlockSpec((1, LANES), lambda bi, si: (0, 0))
    gain = lambda r: pl.BlockSpec((1, r), lambda bi, si: (0, 0))
    whole = lambda w: pl.BlockSpec(w.shape, lambda bi, si: (0, 0))
    head_out = lambda d: pl.BlockSpec((1, h, ts, d), lambda bi, si: (bi, 0, si, 0))
    return pl.pallas_call(
        _mla_qkv_body,
        grid=(b, ns),
        in_specs=[
            pl.BlockSpec((ts, MLA_Q_RANK), row(0)),
            pl.BlockSpec((ts, MLA_KV_RANK), row(MLA_Q_RANK // MLA_KV_RANK)),
            gain(MLA_Q_RANK), gain(MLA_KV_RANK), whole(wq), whole(wkv),
            pl.BlockSpec((ts, LANES), row((MLA_Q_RANK + MLA_KV_RANK) // LANES)),
            pl.BlockSpec((ts, 1), row(0)),
        ] + [vec] * 6,
        out_specs=[head_out(256), head_out(256), head_out(128)],
        out_shape=[
            jax.ShapeDtypeStruct((b, h, s, 256), BF16),
            jax.ShapeDtypeStruct((b, h, s, 256), BF16),
            jax.ShapeDtypeStruct((b, h, s, 128), BF16),
        ],
        scratch_shapes=[pltpu.VMEM((ts, wq.shape[1]), F32), pltpu.VMEM((ts, wkv.shape[1]), F32)],
        compiler_params=_cparams(("parallel", "parallel")),
        name="mla_qkv_prep",
    )(z, z, g_qa.reshape(1, -1).astype(F32), g_kva.reshape(1, -1).astype(F32), wq, wkv, z, pos_col, *rows)


def _split3(x):
    hi = x.astype(BF16)
    r1 = x - hi.astype(F32)
    mid = r1.astype(BF16)
    lo = (r1 - mid.astype(F32)).astype(BF16)
    return hi, mid, lo


def _fox_prep_body(z_ref, f_ref, bf_ref, gq_ref, gk_ref, qf_ref, kf_ref, vf_ref, ct_ref, carry_ref):
    ts = z_ref.shape[0]
    d = FOX_HEAD_DIM
    nh = FOX_HEADS

    @pl.when(pl.program_id(1) == 0)
    def _():
        carry_ref[...] = jnp.zeros_like(carry_ref)

    x = f_ref[...] + bf_ref[...]
    logf = jnp.minimum(x, 0.0) - jnp.log1p(jnp.exp(-jnp.abs(x)))
    tri = (lax.broadcasted_iota(jnp.int32, (ts, ts), 0) >= lax.broadcasted_iota(jnp.int32, (ts, ts), 1)).astype(BF16)
    hi, mid, lo = _split3(logf)
    local = (jnp.dot(tri, hi, preferred_element_type=F32) + jnp.dot(tri, mid, preferred_element_type=F32)
             + jnp.dot(tri, lo, preferred_element_type=F32))
    c = carry_ref[0:1, :] + local
    carry_ref[0:1, :] = c[ts - 1:ts, :]
    ct_ref[0] = (c * LOG2_E).T

    inv_d = 1.0 / d
    for h in range(nh):
        q = z_ref[:, h * d:(h + 1) * d].astype(F32)
        k = z_ref[:, (nh + h) * d:(nh + h + 1) * d].astype(F32)
        rq = lax.rsqrt(jnp.sum(q * q, axis=-1, keepdims=True) * inv_d + RMS_EPS)
        rk = lax.rsqrt(jnp.sum(k * k, axis=-1, keepdims=True) * inv_d + RMS_EPS)
        qf_ref[0, h] = (q * rq * gq_ref[...]).astype(BF16)
        kf_ref[0, h] = (k * rk * gk_ref[...]).astype(BF16)
        vf_ref[0, h] = z_ref[:, (2 * nh + h) * d:(2 * nh + h + 1) * d]


def fox_prep(zmain, flog, rows, b, s, ts=256):
    ns = s // ts
    h = FOX_HEADS
    row = lambda bi, si: (bi * ns + si, 0)
    vec = pl.BlockSpec((1, LANES), lambda bi, si: (0, 0))
    head_out = pl.BlockSpec((1, h, ts, 128), lambda bi, si: (bi, 0, si, 0))
    return pl.pallas_call(
        _fox_prep_body,
        grid=(b, ns),
        in_specs=[pl.BlockSpec((ts, zmain.shape[1]), row), pl.BlockSpec((ts, LANES), row), vec, vec, vec],
        out_specs=[head_out, head_out, head_out, pl.BlockSpec((1, LANES, ts), lambda bi, si: (bi, 0, si))],
        out_shape=[jax.ShapeDtypeStruct((b, h, s, 128), BF16)] * 3 + [jax.ShapeDtypeStruct((b, LANES, s), F32)],
        scratch_shapes=[pltpu.VMEM((8, LANES), F32)],
        compiler_params=_cparams(("parallel", "arbitrary")),
        name="fox_prep",
    )(zmain, flog, *rows)


_ATTN_HEADS = 2


def _attn_body(*refs, fox, tq, nq):
    if fox:
        q_ref, k_ref, v_ref, ccol_ref, crow_ref, gate_ref, o_ref, s_ref, p_ref = refs
    else:
        q_ref, k_ref, v_ref, o_ref, s_ref, p_ref = refs
    nt = (((1,), (1,)), ((), ()))
    half = tq // 2
    dv = v_ref.shape[-1]
    r = lax.broadcasted_iota(jnp.int32, (tq, tq), 0)
    c = lax.broadcasted_iota(jnp.int32, (tq, tq), 1)
    allowed = (c <= r) if fox else ((c >> CHUNK_SHIFT) <= (r >> CHUNK_SHIFT))
    for i in range(nq):
        for hh in range(_ATTN_HEADS):
            buf = 2 * hh + i % 2
            rows = slice(i * tq, (i + 1) * tq)
            q = q_ref[0, hh, rows, :]
            mx = jnp.full((tq, half), NEG_INF, F32)
            for j in range(i + 1):
                cols = slice(j * tq, (j + 1) * tq)
                s = lax.dot_general(q, k_ref[0, hh, cols, :], nt, preferred_element_type=F32)
                if fox:
                    s = s + (ccol_ref[0, hh, rows, :] - crow_ref[0, hh, :, cols])
                if j == i:
                    s = jnp.where(allowed, s, NEG_INF)
                s_ref[buf, :, cols] = s
                mx = jnp.maximum(mx, jnp.maximum(s[:, :half], s[:, half:]))
            m = jnp.max(mx, axis=-1, keepdims=True)
            ps = jnp.zeros((tq, half), F32)
            for j in range(i + 1):
                cols = slice(j * tq, (j + 1) * tq)
                p = jnp.exp2(s_ref[buf, :, cols] - m)
                ps = ps + (p[:, :half] + p[:, half:])
                p_ref[buf, :, cols] = p.astype(BF16)
            l = jnp.sum(ps, axis=-1, keepdims=True)
            n_keys = (i + 1) * tq
            o = jnp.dot(p_ref[buf, :, :n_keys], v_ref[0, hh, :n_keys, :], preferred_element_type=F32) / l
            out_cols = slice(hh * dv, (hh + 1) * dv)
            if fox:
                o = o * (1.0 / (1.0 + jnp.exp(-gate_ref[rows, out_cols].astype(F32))))
            o_ref[rows, out_cols] = o.astype(o_ref.dtype)


def attention(qf, kf, vf, *, fox, ccol=None, crow=None, gate=None, gate_col0=0, tq=256):
    b, h, s, dk = qf.shape
    dv = vf.shape[-1]
    nq = s // tq
    hps = _ATTN_HEADS
    head = lambda d: pl.BlockSpec((1, hps, s, d), lambda bi, hi: (bi, hi, 0, 0))
    in_specs = [head(dk), head(dk), head(dv)]
    args = [qf, kf, vf]
    if fox:
        in_specs += [
            head(1),
            pl.BlockSpec((1, hps, 1, s), lambda bi, hi: (bi, hi, 0, 0)),
            pl.BlockSpec((s, hps * dv), lambda bi, hi: (bi, gate_col0 // hps + hi)),
        ]
        args += [ccol, crow.reshape(b, h, 1, s), gate]
    return pl.pallas_call(
        functools.partial(_attn_body, fox=fox, tq=tq, nq=nq),
        grid=(b, h // hps),
        in_specs=in_specs,
        out_specs=pl.BlockSpec((s, hps * dv), lambda bi, hi: (bi, hi)),
        out_shape=jax.ShapeDtypeStruct((b * s, h * dv), BF16),
        scratch_shapes=[pltpu.VMEM((2 * hps, tq, s), F32), pltpu.VMEM((2 * hps, tq, s), BF16)],
        compiler_params=_cparams(("parallel", "parallel")),
        name="fox_attn" if fox else "mla_attn",
    )(*args)


_CAND_PAIRS = [(p, q) for p in range(PEER_TOPK) for q in range(PEER_TOPK) if (p + 1) * (q + 1) <= PEER_TOPK]
_CAND_ROWS = 56


def _top_rows(v, n, want_rank=False, exact=False):
    row_id = lax.broadcasted_iota(jnp.int32, v.shape, 0).astype(F32) if exact else None
    rank = jnp.full(v.shape, float(n), F32)
    out = []
    for r in range(n):
        m = jnp.max(v, axis=0, keepdims=True)
        out.append(m)
        hit = v == m
        if exact:
            hit = row_id == jnp.min(jnp.where(hit, row_id, float(v.shape[0])), axis=0, keepdims=True)
        if want_rank:
            rank = jnp.where(hit, float(r), rank)
        if r + 1 < n or exact:
            v = jnp.where(hit, -jnp.inf, v)
    return (out,) + ((rank,) if want_rank else ()) + ((v,) if exact else ())


def _count_ge(v, row):
    return jnp.sum(jnp.where(v >= row, 1.0, 0.0), axis=0, keepdims=True)


def _select_lane_group(s1, s2, cand_ref, exact):
    k = PEER_TOPK
    if exact:
        a1, rank1 = _top_rows(s1, k, want_rank=True, exact=True)[:2]
        a2, rank2 = _top_rows(s2, k, want_rank=True, exact=True)[:2]
    else:
        a1, = _top_rows(s1, k)
        a2, rank2 = _top_rows(s2, k, want_rank=True)
    for r, (p, q) in enumerate(_CAND_PAIRS):
        cand_ref[r:r + 1, :] = a1[p] + a2[q]
    cand = cand_ref[...]
    if exact:
        best, left = _top_rows(cand, k, exact=True)
        cand_ref[...] = jnp.where(left != cand, 1.0, 0.0)
    else:
        best, = _top_rows(cand, k)
    tau = best[k - 1]
    z = jnp.ones_like(tau)
    for r in range(1, k):
        z = z + jnp.exp(best[r] - best[0])
    n1 = jnp.zeros(s1.shape, F32)
    r = 0
    for p in range(k):
        n_p = jnp.zeros_like(tau)
        for q in range(k // (p + 1)):
            n_p = n_p + (cand_ref[r:r + 1, :] if exact else jnp.where(a1[p] + a2[q] >= tau, 1.0, 0.0))
            r += 1
        n1 = jnp.where((rank1 == float(p)) if exact else (s1 == a1[p]), n_p, n1)
    stats = (rank2.astype(BF16), jnp.exp(s2 - a2[0]).astype(BF16), n1, 0.5 * jnp.exp(s1 - a1[0]) / z)
    if exact:
        return stats
    ties = jnp.maximum(jnp.maximum(_count_ge(s1, a1[k - 1]), _count_ge(s2, a2[k - 1])), _count_ge(cand, tau)) > k
    return stats + (ties,)


def _peer_prep_body(x_ref, g_ref, wq_ref, sk_ref, xn_ref, rk2_ref, e2_ref, n1_ref, c1_ref, s_ref, cand_ref):
    @pl.when(pl.program_id(1) == 0)
    def _():
        a = x_ref[...]
        y = a * lax.rsqrt(jnp.mean(a * a, axis=-1, keepdims=True) + RMS_EPS)
        xn_ref[...] = (y * g_ref[...]).T.astype(BF16)

    qt = jnp.dot(wq_ref[...], xn_ref[...], preferred_element_type=F32)
    s_ref[0] = jnp.dot(sk_ref[0], qt[0:128].astype(BF16), preferred_element_type=F32)
    s_ref[1] = jnp.dot(sk_ref[1], qt[128:256].astype(BF16), preferred_element_type=F32)
    cand_ref[...] = jnp.full(cand_ref.shape, -jnp.inf, F32)

    def lane_groups(c, carry):
        def column(g):
            lanes = pl.ds(pl.multiple_of((2 * c + g) * LANES, LANES), LANES)
            slot = cand_ref.at[g]

            def store(rank2, e2, n1, c1):
                rk2_ref[0, :, lanes] = rank2
                e2_ref[0, :, lanes] = e2
                n1_ref[0, :, lanes] = n1
                c1_ref[0, :, lanes] = c1

            def select(exact):
                return _select_lane_group(s_ref[0, :, lanes], s_ref[1, :, lanes], slot, exact)

            *stats, ties = select(exact=False)
            store(*stats)

            def redo():
                store(*select(exact=True))
                slot[...] = jnp.full(slot.shape, -jnp.inf, F32)

            return jnp.max(jnp.where(ties, 1.0, 0.0)) > 0.0, redo

        fixes = [column(g) for g in range(2)]
        for tied, redo in fixes:
            pl.when(tied)(redo)
        return carry

    for c in range(s_ref.shape[-1] // (2 * LANES)):
        lane_groups(c, 0)


def peer_prep(x, gain, wq_t, sk, layer, tm=512):
    t, d = x.shape
    nh = PEER_HEADS
    stat = pl.BlockSpec((1, PEER_N_KEYS, tm), lambda i, h: (h, 0, i))
    stat_f32 = jax.ShapeDtypeStruct((nh, PEER_N_KEYS, t), F32)
    stat_bf16 = jax.ShapeDtypeStruct((nh, PEER_N_KEYS, t), BF16)
    return pl.pallas_call(
        _peer_prep_body,
        grid=(t // tm, nh),
        in_specs=[
            pl.BlockSpec((tm, d), lambda i, h: (i, 0)),
            pl.BlockSpec((1, d), lambda i, h: (0, 0)),
            pl.BlockSpec((None, 2 * PEER_D_HALF, d), lambda i, h: (layer, h, 0)),
            pl.BlockSpec((None, 2, PEER_N_KEYS, PEER_D_HALF), lambda i, h: (layer, 0, 0, 0)),
        ],
        out_specs=[pl.BlockSpec((d, tm), lambda i, h: (0, i)), stat, stat, stat, stat],
        out_shape=[jax.ShapeDtypeStruct((d, t), BF16), stat_bf16, stat_bf16, stat_f32, stat_f32],
        scratch_shapes=[pltpu.VMEM((2, PEER_N_KEYS, tm), F32), pltpu.VMEM((2, _CAND_ROWS, LANES), F32)],
        compiler_params=_cparams(("parallel", "arbitrary")),
        name="peer_prep",
    )(x, gain.reshape(1, d).astype(F32), wq_t, sk)


_SLAB = 16
_GATE_LANES = 256


def _peer_main_body(xn_ref, u_ref, vta_ref, vtb_ref, rk2_ref, e2_ref, n1_ref, c1_ref, res_ref, o_ref,
                    acc_ref, h_ref, a_ref):
    k = pl.program_id(1)
    nk = pl.num_programs(1) - 1
    te, tm = h_ref.shape[1:]
    n_i = te // PEER_N_KEYS
    n_slab = PEER_N_KEYS // _SLAB
    zero = jnp.zeros((_SLAB, _GATE_LANES), BF16)

    def gate_stage(slot, blk):
        for lc in range(tm // _GATE_LANES):
            lanes = slice(lc * _GATE_LANES, (lc + 1) * _GATE_LANES)
            for ii in range(n_i):
                i = blk * n_i + ii
                g = [None] * n_slab
                for h in range(PEER_HEADS):
                    nb = jnp.broadcast_to(n1_ref[h, pl.ds(i, 1), lanes], (_SLAB, _GATE_LANES)).astype(BF16)
                    cb = jnp.broadcast_to(c1_ref[h, pl.ds(i, 1), lanes], (_SLAB, _GATE_LANES)).astype(BF16)
                    for sl in range(n_slab):
                        rows = slice(sl * _SLAB, (sl + 1) * _SLAB)
                        w = jnp.where(rk2_ref[h, rows, lanes] < nb, e2_ref[h, rows, lanes] * cb, zero)
                        g[sl] = w if g[sl] is None else g[sl] + w
                for sl in range(n_slab):
                    rows = slice(ii * PEER_N_KEYS + sl * _SLAB, ii * PEER_N_KEYS + (sl + 1) * _SLAB)
                    hh = h_ref[slot, rows, lanes]
                    act = hh * (1.0 + lax.erf(hh * (1.0 / math.sqrt(2.0))))
                    a_ref[slot, rows, lanes] = act.astype(BF16) * g[sl]

    def pre_activations(slot):
        h_ref[slot] = jnp.dot(u_ref[slot * te:(slot + 1) * te, :].astype(BF16), xn_ref[...], preferred_element_type=F32)

    def block_output(slot, vt_ref):
        return jnp.dot(vt_ref[...], a_ref[slot], preferred_element_type=F32)

    @pl.when(k == 0)
    def _():
        pre_activations(0)
        gate_stage(0, 0)
        pre_activations(1)
        acc_ref[...] = block_output(0, vtb_ref)

    @pl.when((k > 0) & (k < nk))
    def _():
        gate_stage(1, 2 * k - 1)
        pre_activations(0)
        acc_ref[...] += block_output(1, vta_ref)
        gate_stage(0, 2 * k)
        pre_activations(1)
        acc_ref[...] += block_output(0, vtb_ref)

    @pl.when(k == nk)
    def _():
        gate_stage(1, 2 * k - 1)
        o_ref[...] = res_ref[...] + (acc_ref[...] + block_output(1, vta_ref)).T


def peer_main(xn_t, u, vt, rk2, e2, n1, c1, res, layer, tm=512):
    d, t = xn_t.shape
    te = vt.shape[-1]
    nk = u.shape[1] // (2 * te)
    nh = PEER_HEADS
    once = pl.Buffered(1)
    stat = pl.BlockSpec((nh, PEER_N_KEYS, tm), lambda i, k: (0, 0, i), pipeline_mode=once)
    return pl.pallas_call(
        _peer_main_body,
        grid=(t // tm, nk + 1),
        in_specs=[
            pl.BlockSpec((d, tm), lambda i, k: (0, i), pipeline_mode=once),
            pl.BlockSpec((None, 2 * te, d), lambda i, k: (layer, jnp.minimum(k, nk - 1), 0)),
            pl.BlockSpec((None, None, d, te), lambda i, k: (layer, jnp.clip(2 * k - 1, 0, 2 * nk - 1), 0, 0)),
            pl.BlockSpec((None, None, d, te), lambda i, k: (layer, jnp.minimum(2 * k, 2 * nk - 1), 0, 0)),
            stat, stat, stat, stat,
            pl.BlockSpec((tm, d), lambda i, k: (i, 0), pipeline_mode=once),
        ],
        out_specs=pl.BlockSpec((tm, d), lambda i, k: (i, 0)),
        out_shape=jax.ShapeDtypeStruct((t, d), F32),
        scratch_shapes=[pltpu.VMEM((d, tm), F32), pltpu.VMEM((2, te, tm), F32), pltpu.VMEM((2, te, tm), BF16)],
        compiler_params=_cparams(("parallel", "arbitrary")),
        name="peer_main",
    )(xn_t, u, vt, vt, rk2, e2, n1, c1, res)


PEER_BLOCK = 512


def _transpose_block_body(v_ref, o_ref):
    o_ref[...] = v_ref[...].T.astype(BF16)


def peer_v_blocks(v):
    n_layers, n_e, d = v.shape
    te = PEER_BLOCK
    return pl.pallas_call(
        _transpose_block_body,
        grid=(n_layers, n_e // te),
        in_specs=[pl.BlockSpec((None, te, d), lambda l, e: (l, e, 0))],
        out_specs=pl.BlockSpec((None, None, d, te), lambda l, e: (l, e, 0, 0)),
        out_shape=jax.ShapeDtypeStruct((n_layers, n_e // te, d, te), BF16),
        compiler_params=_cparams(("parallel", "parallel")),
        name="peer_v_blocks",
    )(v)


def peer_weights(w_q, sub_keys, u, v):
    wq_t = jnp.swapaxes(w_q, 1, 2).astype(BF16)
    return wq_t, sub_keys.astype(BF16), u, peer_v_blocks(v)


def peer_ffn(x, gain, weights, layer):
    wq_t, sk, u, vt = weights
    xn_t, rk2, e2, n1, c1 = peer_prep(x, gain, wq_t, sk, layer)
    return peer_main(xn_t, u, vt, rk2, e2, n1, c1, x, layer)


def _row128(v):
    return v.reshape(1, LANES).astype(F32)


def mla_mixer(x, gain, positions, w_in, g_qa, w_qb, g_kva, w_kvb, g_q, g_k, w_o, b, s):
    t = b * s
    nh = MLA_HEADS
    w_in_p = jnp.pad(w_in, ((0, 0), (0, 64))).astype(BF16)
    z = norm_matmul(x, w_in_p, gain=gain, tn=w_in_p.shape[1], name="mla_in")
    wq = w_qb.reshape(MLA_Q_RANK, nh, MLA_QK)
    wq = jnp.concatenate([wq[:, :, :MLA_NOPE].reshape(MLA_Q_RANK, -1), wq[:, :, MLA_NOPE:].reshape(MLA_Q_RANK, -1)], 1)

    scale = LOG2_E / math.sqrt(MLA_QK)
    inv_freq = ROPE_THETA ** (-jnp.arange(0, MLA_ROPE, 2, dtype=F32) / MLA_ROPE)
    sign = jnp.where((jnp.arange(LANES) % 64) < 32, -1.0, 1.0)
    rows = [
        _row128(jnp.tile(inv_freq, 4)),
        _row128(sign),
        _row128(g_q[:MLA_NOPE] * scale),
        _row128(jnp.tile(g_q[MLA_NOPE:], 2) * scale),
        _row128(g_k[:MLA_NOPE]),
        _row128(jnp.pad(g_k[MLA_NOPE:], (0, 64))),
    ]
    pos_col = positions.astype(F32).reshape(t, 1)
    qf, kf, vf = mla_qkv_prep(z, wq.astype(BF16), w_kvb.astype(BF16), g_qa, g_kva, pos_col, rows, b, s)
    o = attention(qf, kf, vf, fox=False)
    return norm_matmul(o, w_o.astype(BF16), residual=x, tn=w_o.shape[1], name="mla_out")


def fox_mixer(x, gain, w_in, b_f, g_q, g_k, w_o, b, s):
    nh = FOX_HEADS
    d = x.shape[1]
    zmain = norm_matmul(x, w_in.astype(BF16), gain=gain, n_cols=4 * d, tm=1024, tn=1024, out_dtype=BF16,
                        name="fox_in")
    w_f = jnp.pad(w_in[:, 4 * d:], ((0, 0), (0, LANES - nh))).astype(BF16)
    flog = norm_matmul(x, w_f, gain=gain, tm=1024, name="fox_f")
    scale = LOG2_E / math.sqrt(FOX_HEAD_DIM)
    rows = [_row128(jnp.pad(b_f, (0, LANES - nh))), _row128(g_q * scale), _row128(g_k)]
    qf, kf, vf, ct = fox_prep(zmain, flog, rows, b, s)
    crow = ct[:, :nh, :]
    o = attention(qf, kf, vf, fox=True, ccol=crow[..., None], crow=crow, gate=zmain, gate_col0=3 * nh)
    return norm_matmul(o, w_o.astype(BF16), residual=x, tn=w_o.shape[1], name="fox_out")


def kernel(x, positions, norm_mix_g, norm_ffn_g, mla_w_in, mla_g_qa, mla_w_qb, mla_g_kva, mla_w_kvb, mla_g_q, mla_g_k, mla_w_o, fox_w_in, fox_b_f, fox_g_q, fox_g_k, fox_w_o, peer_w_q, peer_sub_keys, peer_u, peer_v):
    b, s, d = x.shape
    depth = norm_mix_g.shape[0]
    xt = x.reshape(b * s, d)
    peer_w = peer_weights(peer_w_q, peer_sub_keys, peer_u, peer_v)
    for i in range(depth):
        j = i // 2
        if i % 2 == 0:
            xt = mla_mixer(xt, norm_mix_g[i], positions, mla_w_in[j], mla_g_qa[j], mla_w_qb[j], mla_g_kva[j],
                           mla_w_kvb[j], mla_g_q[j], mla_g_k[j], mla_w_o[j], b, s)
        else:
            xt = fox_mixer(xt, norm_mix_g[i], fox_w_in[j], fox_b_f[j], fox_g_q[j], fox_g_k[j], fox_w_o[j], b, s)
        xt = peer_ffn(xt, norm_ffn_g[i], peer_w, i)
    return xt.reshape(b, s, d)
```

```python
import functools
import math

import jax
import jax.numpy as jnp
from jax import lax
from jax.experimental import pallas as pl
from jax.experimental.pallas import tpu as pltpu

F32 = jnp.float32
BF16 = jnp.bfloat16

RMS_EPS = 1e-6
NEG_INF = -1e30
LOG2_E = math.log2(math.e)
CHUNK_SHIFT = 6
ROPE_THETA = 10000.0

MLA_HEADS = 16
MLA_Q_RANK = 512
MLA_KV_RANK = 512
MLA_NOPE = 128
MLA_ROPE = 64
MLA_V = 128
MLA_QK = MLA_NOPE + MLA_ROPE

FOX_HEADS = 16
FOX_HEAD_DIM = 128

PEER_HEADS = 8
PEER_N_KEYS = 128
PEER_D_HALF = 128
PEER_TOPK = 16

LANES = 128
VMEM_LIMIT = 60 * 1024 * 1024


def _cparams(sem, flags=None):
    return pltpu.CompilerParams(dimension_semantics=sem, vmem_limit_bytes=VMEM_LIMIT, flags=flags)


def _nm_body(*refs, norm, has_res):
    it = iter(refs)
    a_ref = next(it)
    g_ref = next(it) if norm else None
    w_ref = next(it)
    r_ref = next(it) if has_res else None
    o_ref = next(it)
    an_ref = next(it) if norm else None

    if norm:
        @pl.when(pl.program_id(1) == 0)
        def _():
            a = a_ref[...].astype(F32)
            y = a * lax.rsqrt(jnp.mean(a * a, axis=-1, keepdims=True) + RMS_EPS)
            an_ref[...] = (y * g_ref[...]).astype(BF16)

        a_bf = an_ref[...]
    else:
        a_bf = a_ref[...]
    acc = jnp.dot(a_bf, w_ref[...], preferred_element_type=F32)
    if has_res:
        acc = acc + r_ref[...]
    o_ref[...] = acc.astype(o_ref.dtype)


def norm_matmul(a, w, *, gain=None, residual=None, a_col_block=0, n_cols=None, out_dtype=F32, tm=512, tn=512, name):
    m = a.shape[0]
    k = w.shape[0]
    n = w.shape[1] if n_cols is None else n_cols
    tn = min(tn, n)
    assert m % tm == 0 and n % tn == 0
    norm = gain is not None
    in_specs = [pl.BlockSpec((tm, k), lambda i, j: (i, a_col_block))]
    args = [a]
    if norm:
        in_specs.append(pl.BlockSpec((1, k), lambda i, j: (0, 0)))
        args.append(gain.reshape(1, k).astype(F32))
    in_specs.append(pl.BlockSpec((k, tn), lambda i, j: (0, j)))
    args.append(w)
    if residual is not None:
        in_specs.append(pl.BlockSpec((tm, tn), lambda i, j: (i, j)))
        args.append(residual)
    return pl.pallas_call(
        functools.partial(_nm_body, norm=norm, has_res=residual is not None),
        grid=(m // tm, n // tn),
        in_specs=in_specs,
        out_specs=pl.BlockSpec((tm, tn), lambda i, j: (i, j)),
        out_shape=jax.ShapeDtypeStruct((m, n), out_dtype),
        scratch_shapes=[pltpu.VMEM((tm, k), BF16)] if norm else [],
        compiler_params=_cparams(("parallel", "arbitrary")),
        name=name,
    )(*args)


def _swap_halves(x, first_half):
    return jnp.where(first_half, pltpu.roll(x, 96, 1), pltpu.roll(x, 32, 1))


def _mla_prep_body(q_ref, kv_ref, kr_ref, pos_ref, invf_ref, sign_ref, gqn_ref, gqr_ref, gkn_ref, gkr_ref,
                   qf_ref, kf_ref, vf_ref):
    ts = q_ref.shape[0]
    lane = lax.broadcasted_iota(jnp.int32, (ts, LANES), 1)
    first_half = (lane & 63) < 32
    low64 = lane < 64
    ang = pos_ref[...] * invf_ref[...]
    cosv = jnp.cos(ang)
    sinv = jnp.sin(ang) * sign_ref[...]

    def rot(x):
        return x * cosv + _swap_halves(x, first_half) * sinv

    inv_d = 1.0 / MLA_QK
    zeros = jnp.zeros((ts, LANES), F32)

    kr = kr_ref[...]
    ss_kr = jnp.sum(kr * kr, axis=-1, keepdims=True)
    kr_rot = rot(kr * gkr_ref[...])
    for h in range(MLA_HEADS):
        kn = kv_ref[:, h * 256:h * 256 + 128].astype(F32)
        r = lax.rsqrt((jnp.sum(kn * kn, axis=-1, keepdims=True) + ss_kr) * inv_d + RMS_EPS)
        kf_ref[0, h, :, 0:128] = (kn * r * gkn_ref[...]).astype(BF16)
        kf_ref[0, h, :, 128:256] = (kr_rot * r).astype(BF16)
        vf_ref[0, h] = kv_ref[:, h * 256 + 128:(h + 1) * 256].astype(BF16)

    for p in range(MLA_HEADS // 2):
        xr = q_ref[:, 2048 + p * 128:2048 + (p + 1) * 128].astype(F32)
        xr2 = xr * xr
        ss_e = jnp.sum(jnp.where(low64, xr2, 0.0), axis=-1, keepdims=True)
        ss_o = jnp.sum(jnp.where(low64, 0.0, xr2), axis=-1, keepdims=True)
        qn_e = q_ref[:, (2 * p) * 128:(2 * p + 1) * 128].astype(F32)
        qn_o = q_ref[:, (2 * p + 1) * 128:(2 * p + 2) * 128].astype(F32)
        r_e = lax.rsqrt((jnp.sum(qn_e * qn_e, axis=-1, keepdims=True) + ss_e) * inv_d + RMS_EPS)
        r_o = lax.rsqrt((jnp.sum(qn_o * qn_o, axis=-1, keepdims=True) + ss_o) * inv_d + RMS_EPS)
        xr_rot = rot(xr * jnp.where(low64, r_e, r_o) * gqr_ref[...])
        qf_ref[0, 2 * p, :, 0:128] = (qn_e * r_e * gqn_ref[...]).astype(BF16)
        qf_ref[0, 2 * p, :, 128:256] = jnp.where(low64, xr_rot, zeros).astype(BF16)
        qf_ref[0, 2 * p + 1, :, 0:128] = (qn_o * r_o * gqn_ref[...]).astype(BF16)
        qf_ref[0, 2 * p + 1, :, 128:256] = jnp.where(low64, pltpu.roll(xr_rot, 64, 1), zeros).astype(BF16)


def _mla_qkv_body(cq_ref, ckv_ref, gqa_ref, gkva_ref, wq_ref, wkv_ref, kr_ref, pos_ref, *rest):
    vec_refs, (qf_ref, kf_ref, vf_ref, q_s, kv_s) = rest[:6], rest[6:]

    def up_project(c_ref, g_ref, w_ref):
        c = c_ref[...]
        y = c * lax.rsqrt(jnp.mean(c * c, axis=-1, keepdims=True) + RMS_EPS)
        return jnp.dot((y * g_ref[...]).astype(BF16), w_ref[...], preferred_element_type=F32)

    q_s[...] = up_project(cq_ref, gqa_ref, wq_ref)
    kv_s[...] = up_project(ckv_ref, gkva_ref, wkv_ref)
    _mla_prep_body(q_s, kv_s, kr_ref, pos_ref, *vec_refs, qf_ref, kf_ref, vf_ref)


def mla_qkv_prep(z, wq, wkv, g_qa, g_kva, pos_col, rows, b, s, ts=256):
    ns = s // ts
    h = MLA_HEADS
    row = lambda blk: (lambda bi, si: (bi * ns + si, blk))
    vec = pl.BlockSpec((1, LANES), lambda bi, si: (0, 0))
    gain = lambda r: pl.BlockSpec((1, r), lambda bi, si: (0, 0))
    whole = lambda w: pl.BlockSpec(w.shape, lambda bi, si: (0, 0))
    head_out = lambda d: pl.BlockSpec((1, h, ts, d), lambda bi, si: (bi, 0, si, 0))
    return pl.pallas_call(
        _mla_qkv_body,
        grid=(b, ns),
        in_specs=[
            pl.BlockSpec((ts, MLA_Q_RANK), row(0)),
            pl.BlockSpec((ts, MLA_KV_RANK), row(MLA_Q_RANK // MLA_KV_RANK)),
            gain(MLA_Q_RANK), gain(MLA_KV_RANK), whole(wq), whole(wkv),
            pl.BlockSpec((ts, LANES), row((MLA_Q_RANK + MLA_KV_RANK) // LANES)),
            pl.BlockSpec((ts, 1), row(0)),
        ] + [vec] * 6,
        out_specs=[head_out(256), head_out(256), head_out(128)],
        out_shape=[
            jax.ShapeDtypeStruct((b, h, s, 256), BF16),
            jax.ShapeDtypeStruct((b, h, s, 256), BF16),
            jax.ShapeDtypeStruct((b, h, s, 128), BF16),
        ],
        scratch_shapes=[pltpu.VMEM((ts, wq.shape[1]), F32), pltpu.VMEM((ts, wkv.shape[1]), F32)],
        compiler_params=_cparams(("parallel", "parallel")),
        name="mla_qkv_prep",
    )(z, z, g_qa.reshape(1, -1).astype(F32), g_kva.reshape(1, -1).astype(F32), wq, wkv, z, pos_col, *rows)


def _split3(x):
    hi = x.astype(BF16)
    r1 = x - hi.astype(F32)
    mid = r1.astype(BF16)
    lo = (r1 - mid.astype(F32)).astype(BF16)
    return hi, mid, lo


def _fox_prep_body(z_ref, f_ref, bf_ref, gq_ref, gk_ref, qf_ref, kf_ref, vf_ref, ct_ref, carry_ref):
    ts = z_ref.shape[0]
    d = FOX_HEAD_DIM
    nh = FOX_HEADS

    @pl.when(pl.program_id(1) == 0)
    def _():
        carry_ref[...] = jnp.zeros_like(carry_ref)

    x = f_ref[...] + bf_ref[...]
    logf = jnp.minimum(x, 0.0) - jnp.log1p(jnp.exp(-jnp.abs(x)))
    tri = (lax.broadcasted_iota(jnp.int32, (ts, ts), 0) >= lax.broadcasted_iota(jnp.int32, (ts, ts), 1)).astype(BF16)
    hi, mid, lo = _split3(logf)
    local = (jnp.dot(tri, hi, preferred_element_type=F32) + jnp.dot(tri, mid, preferred_element_type=F32)
             + jnp.dot(tri, lo, preferred_element_type=F32))
    c = carry_ref[0:1, :] + local
    carry_ref[0:1, :] = c[ts - 1:ts, :]
    ct_ref[0] = (c * LOG2_E).T

    inv_d = 1.0 / d
    for h in range(nh):
        q = z_ref[:, h * d:(h + 1) * d].astype(F32)
        k = z_ref[:, (nh + h) * d:(nh + h + 1) * d].astype(F32)
        rq = lax.rsqrt(jnp.sum(q * q, axis=-1, keepdims=True) * inv_d + RMS_EPS)
        rk = lax.rsqrt(jnp.sum(k * k, axis=-1, keepdims=True) * inv_d + RMS_EPS)
        qf_ref[0, h] = (q * rq * gq_ref[...]).astype(BF16)
        kf_ref[0, h] = (k * rk * gk_ref[...]).astype(BF16)
        vf_ref[0, h] = z_ref[:, (2 * nh + h) * d:(2 * nh + h + 1) * d]


def fox_prep(zmain, flog, rows, b, s, ts=256):
    ns = s // ts
    h = FOX_HEADS
    row = lambda bi, si: (bi * ns + si, 0)
    vec = pl.BlockSpec((1, LANES), lambda bi, si: (0, 0))
    head_out = pl.BlockSpec((1, h, ts, 128), lambda bi, si: (bi, 0, si, 0))
    return pl.pallas_call(
        _fox_prep_body,
        grid=(b, ns),
        in_specs=[pl.BlockSpec((ts, zmain.shape[1]), row), pl.BlockSpec((ts, LANES), row), vec, vec, vec],
        out_specs=[head_out, head_out, head_out, pl.BlockSpec((1, LANES, ts), lambda bi, si: (bi, 0, si))],
        out_shape=[jax.ShapeDtypeStruct((b, h, s, 128), BF16)] * 3 + [jax.ShapeDtypeStruct((b, LANES, s), F32)],
        scratch_shapes=[pltpu.VMEM((8, LANES), F32)],
        compiler_params=_cparams(("parallel", "arbitrary")),
        name="fox_prep",
    )(zmain, flog, *rows)


_ATTN_HEADS = 2


def _attn_body(*refs, fox, tq, nq):
    if fox:
        q_ref, k_ref, v_ref, ccol_ref, crow_ref, gate_ref, o_ref, s_ref, p_ref = refs
    else:
        q_ref, k_ref, v_ref, o_ref, s_ref, p_ref = refs
    nt = (((1,), (1,)), ((), ()))
    half = tq // 2
    dv = v_ref.shape[-1]
    r = lax.broadcasted_iota(jnp.int32, (tq, tq), 0)
    c = lax.broadcasted_iota(jnp.int32, (tq, tq), 1)
    allowed = (c <= r) if fox else ((c >> CHUNK_SHIFT) <= (r >> CHUNK_SHIFT))
    for i in range(nq):
        for hh in range(_ATTN_HEADS):
            buf = 2 * hh + i % 2
            rows = slice(i * tq, (i + 1) * tq)
            q = q_ref[0, hh, rows, :]
            mx = jnp.full((tq, half), NEG_INF, F32)
            for j in range(i + 1):
                cols = slice(j * tq, (j + 1) * tq)
                s = lax.dot_general(q, k_ref[0, hh, cols, :], nt, preferred_element_type=F32)
                if fox:
                    s = s + (ccol_ref[0, hh, rows, :] - crow_ref[0, hh, :, cols])
                if j == i:
                    s = jnp.where(allowed, s, NEG_INF)
                s_ref[buf, :, cols] = s
                mx = jnp.maximum(mx, jnp.maximum(s[:, :half], s[:, half:]))
            m = jnp.max(mx, axis=-1, keepdims=True)
            ps = jnp.zeros((tq, half), F32)
            for j in range(i + 1):
                cols = slice(j * tq, (j + 1) * tq)
                p = jnp.exp2(s_ref[buf, :, cols] - m)
                ps = ps + (p[:, :half] + p[:, half:])
                p_ref[buf, :, cols] = p.astype(BF16)
            l = jnp.sum(ps, axis=-1, keepdims=True)
            n_keys = (i + 1) * tq
            o = jnp.dot(p_ref[buf, :, :n_keys], v_ref[0, hh, :n_keys, :], preferred_element_type=F32) / l
            out_cols = slice(hh * dv, (hh + 1) * dv)
            if fox:
                o = o * (1.0 / (1.0 + jnp.exp(-gate_ref[rows, out_cols].astype(F32))))
            o_ref[rows, out_cols] = o.astype(o_ref.dtype)


def attention(qf, kf, vf, *, fox, ccol=None, crow=None, gate=None, gate_col0=0, tq=256):
    b, h, s, dk = qf.shape
    dv = vf.shape[-1]
    nq = s // tq
    hps = _ATTN_HEADS
    head = lambda d: pl.BlockSpec((1, hps, s, d), lambda bi, hi: (bi, hi, 0, 0))
    in_specs = [head(dk), head(dk), head(dv)]
    args = [qf, kf, vf]
    if fox:
        in_specs += [
            head(1),
            pl.BlockSpec((1, hps, 1, s), lambda bi, hi: (bi, hi, 0, 0)),
            pl.BlockSpec((s, hps * dv), lambda bi, hi: (bi, gate_col0 // hps + hi)),
        ]
        args += [ccol, crow.reshape(b, h, 1, s), gate]
    return pl.pallas_call(
        functools.partial(_attn_body, fox=fox, tq=tq, nq=nq),
        grid=(b, h // hps),
        in_specs=in_specs,
        out_specs=pl.BlockSpec((s, hps * dv), lambda bi, hi: (bi, hi)),
        out_shape=jax.ShapeDtypeStruct((b * s, h * dv), BF16),
        scratch_shapes=[pltpu.VMEM((2 * hps, tq, s), F32), pltpu.VMEM((2 * hps, tq, s), BF16)],
        compiler_params=_cparams(("parallel", "parallel")),
        name="fox_attn" if fox else "mla_attn",
    )(*args)


_CAND_PAIRS = [(p, q) for p in range(PEER_TOPK) for q in range(PEER_TOPK) if (p + 1) * (q + 1) <= PEER_TOPK]
_CAND_ROWS = 56


def _top_rows(v, n, want_rank=False, exact=False):
    row_id = lax.broadcasted_iota(jnp.int32, v.shape, 0).astype(F32) if exact else None
    rank = jnp.full(v.shape, float(n), F32)
    out = []
    for r in range(n):
        m = jnp.max(v, axis=0, keepdims=True)
        out.append(m)
        hit = v == m
        if exact:
            hit = row_id == jnp.min(jnp.where(hit, row_id, float(v.shape[0])), axis=0, keepdims=True)
        if want_rank:
            rank = jnp.where(hit, float(r), rank)
        if r + 1 < n or exact:
            v = jnp.where(hit, -jnp.inf, v)
    return (out,) + ((rank,) if want_rank else ()) + ((v,) if exact else ())


def _count_ge(v, row):
    return jnp.sum(jnp.where(v >= row, 1.0, 0.0), axis=0, keepdims=True)


def _select_lane_group(s1, s2, cand_ref, exact):
    k = PEER_TOPK
    if exact:
        a1, rank1 = _top_rows(s1, k, want_rank=True, exact=True)[:2]
        a2, rank2 = _top_rows(s2, k, want_rank=True, exact=True)[:2]
    else:
        a1, = _top_rows(s1, k)
        a2, rank2 = _top_rows(s2, k, want_rank=True)
    for r, (p, q) in enumerate(_CAND_PAIRS):
        cand_ref[r:r + 1, :] = a1[p] + a2[q]
    cand = cand_ref[...]
    if exact:
        best, left = _top_rows(cand, k, exact=True)
        cand_ref[...] = jnp.where(left != cand, 1.0, 0.0)
    else:
        best, = _top_rows(cand, k)
    tau = best[k - 1]
    z = jnp.ones_like(tau)
    for r in range(1, k):
        z = z + jnp.exp(best[r] - best[0])
    n1 = jnp.zeros(s1.shape, F32)
    r = 0
    for p in range(k):
        n_p = jnp.zeros_like(tau)
        for q in range(k // (p + 1)):
            n_p = n_p + (cand_ref[r:r + 1, :] if exact else jnp.where(a1[p] + a2[q] >= tau, 1.0, 0.0))
            r += 1
        n1 = jnp.where((rank1 == float(p)) if exact else (s1 == a1[p]), n_p, n1)
    stats = (rank2.astype(BF16), jnp.exp(s2 - a2[0]).astype(BF16), n1, 0.5 * jnp.exp(s1 - a1[0]) / z)
    if exact:
        return stats
    ties = jnp.maximum(jnp.maximum(_count_ge(s1, a1[k - 1]), _count_ge(s2, a2[k - 1])), _count_ge(cand, tau)) > k
    return stats + (ties,)


def _peer_prep_body(x_ref, g_ref, wq_ref, sk_ref, xn_ref, rk2_ref, e2_ref, n1_ref, c1_ref, s_ref, cand_ref):
    @pl.when(pl.program_id(1) == 0)
    def _():
        a = x_ref[...]
        y = a * lax.rsqrt(jnp.mean(a * a, axis=-1, keepdims=True) + RMS_EPS)
        xn_ref[...] = (y * g_ref[...]).T.astype(BF16)

    qt = jnp.dot(wq_ref[...], xn_ref[...], preferred_element_type=F32)
    s_ref[0] = jnp.dot(sk_ref[0], qt[0:128].astype(BF16), preferred_element_type=F32)
    s_ref[1] = jnp.dot(sk_ref[1], qt[128:256].astype(BF16), preferred_element_type=F32)
    cand_ref[...] = jnp.full(cand_ref.shape, -jnp.inf, F32)

    def lane_groups(c, carry):
        def column(g):
            lanes = pl.ds(pl.multiple_of((2 * c + g) * LANES, LANES), LANES)
            slot = cand_ref.at[g]

            def store(rank2, e2, n1, c1):
                rk2_ref[0, :, lanes] = rank2
                e2_ref[0, :, lanes] = e2
                n1_ref[0, :, lanes] = n1
                c1_ref[0, :, lanes] = c1

            def select(exact):
                return _select_lane_group(s_ref[0, :, lanes], s_ref[1, :, lanes], slot, exact)

            *stats, ties = select(exact=False)
            store(*stats)

            def redo():
                store(*select(exact=True))
                slot[...] = jnp.full(slot.shape, -jnp.inf, F32)

            return jnp.max(jnp.where(ties, 1.0, 0.0)) > 0.0, redo

        fixes = [column(g) for g in range(2)]
        for tied, redo in fixes:
            pl.when(tied)(redo)
        return carry

    for c in range(s_ref.shape[-1] // (2 * LANES)):
        lane_groups(c, 0)


def peer_prep(x, gain, wq_t, sk, layer, tm=512):
    t, d = x.shape
    nh = PEER_HEADS
    stat = pl.BlockSpec((1, PEER_N_KEYS, tm), lambda i, h: (h, 0, i))
    stat_f32 = jax.ShapeDtypeStruct((nh, PEER_N_KEYS, t), F32)
    stat_bf16 = jax.ShapeDtypeStruct((nh, PEER_N_KEYS, t), BF16)
    return pl.pallas_call(
        _peer_prep_body,
        grid=(t // tm, nh),
        in_specs=[
            pl.BlockSpec((tm, d), lambda i, h: (i, 0)),
            pl.BlockSpec((1, d), lambda i, h: (0, 0)),
            pl.BlockSpec((None, 2 * PEER_D_HALF, d), lambda i, h: (layer, h, 0)),
            pl.BlockSpec((None, 2, PEER_N_KEYS, PEER_D_HALF), lambda i, h: (layer, 0, 0, 0)),
        ],
        out_specs=[pl.BlockSpec((d, tm), lambda i, h: (0, i)), stat, stat, stat, stat],
        out_shape=[jax.ShapeDtypeStruct((d, t), BF16), stat_bf16, stat_bf16, stat_f32, stat_f32],
        scratch_shapes=[pltpu.VMEM((2, PEER_N_KEYS, tm), F32), pltpu.VMEM((2, _CAND_ROWS, LANES), F32)],
        compiler_params=_cparams(("parallel", "arbitrary")),
        name="peer_prep",
    )(x, gain.reshape(1, d).astype(F32), wq_t, sk)


_SLAB = 16
_GATE_LANES = 256


def _peer_main_body(xn_ref, u_ref, vta_ref, vtb_ref, rk2_ref, e2_ref, n1_ref, c1_ref, res_ref, o_ref,
                    acc_ref, h_ref, a_ref):
    k = pl.program_id(1)
    nk = pl.num_programs(1) - 1
    te, tm = h_ref.shape[1:]
    n_i = te // PEER_N_KEYS
    n_slab = PEER_N_KEYS // _SLAB
    zero = jnp.zeros((_SLAB, _GATE_LANES), BF16)

    def gate_stage(slot, blk):
        for lc in range(tm // _GATE_LANES):
            lanes = slice(lc * _GATE_LANES, (lc + 1) * _GATE_LANES)
            for ii in range(n_i):
                i = blk * n_i + ii
                g = [None] * n_slab
                for h in range(PEER_HEADS):
                    nb = jnp.broadcast_to(n1_ref[h, pl.ds(i, 1), lanes], (_SLAB, _GATE_LANES)).astype(BF16)
                    cb = jnp.broadcast_to(c1_ref[h, pl.ds(i, 1), lanes], (_SLAB, _GATE_LANES)).astype(BF16)
                    for sl in range(n_slab):
                        rows = slice(sl * _SLAB, (sl + 1) * _SLAB)
                        w = jnp.where(rk2_ref[h, rows, lanes] < nb, e2_ref[h, rows, lanes] * cb, zero)
                        g[sl] = w if g[sl] is None else g[sl] + w
                for sl in range(n_slab):
                    rows = slice(ii * PEER_N_KEYS + sl * _SLAB, ii * PEER_N_KEYS + (sl + 1) * _SLAB)
                    hh = h_ref[slot, rows, lanes]
                    act = hh * (1.0 + lax.erf(hh * (1.0 / math.sqrt(2.0))))
                    a_ref[slot, rows, lanes] = act.astype(BF16) * g[sl]

    def pre_activations(slot):
        h_ref[slot] = jnp.dot(u_ref[slot * te:(slot + 1) * te, :].astype(BF16), xn_ref[...], preferred_element_type=F32)

    def block_output(slot, vt_ref):
        return jnp.dot(vt_ref[...], a_ref[slot], preferred_element_type=F32)

    @pl.when(k == 0)
    def _():
        pre_activations(0)
        gate_stage(0, 0)
        pre_activations(1)
        acc_ref[...] = block_output(0, vtb_ref)

    @pl.when((k > 0) & (k < nk))
    def _():
        gate_stage(1, 2 * k - 1)
        pre_activations(0)
        acc_ref[...] += block_output(1, vta_ref)
        gate_stage(0, 2 * k)
        pre_activations(1)
        acc_ref[...] += block_output(0, vtb_ref)

    @pl.when(k == nk)
    def _():
        gate_stage(1, 2 * k - 1)
        o_ref[...] = res_ref[...] + (acc_ref[...] + block_output(1, vta_ref)).T


def peer_main(xn_t, u, vt, rk2, e2, n1, c1, res, layer, tm=512):
    d, t = xn_t.shape
    te = vt.shape[-1]
    nk = u.shape[1] // (2 * te)
    nh = PEER_HEADS
    once = pl.Buffered(1)
    stat = pl.BlockSpec((nh, PEER_N_KEYS, tm), lambda i, k: (0, 0, i))
    return pl.pallas_call(
        _peer_main_body,
        grid=(t // tm, nk + 1),
        in_specs=[
            pl.BlockSpec((d, tm), lambda i, k: (0, i), pipeline_mode=once),
            pl.BlockSpec((None, 2 * te, d), lambda i, k: (layer, jnp.minimum(k, nk - 1), 0)),
            pl.BlockSpec((None, None, d, te), lambda i, k: (layer, jnp.clip(2 * k - 1, 0, 2 * nk - 1), 0, 0)),
            pl.BlockSpec((None, None, d, te), lambda i, k: (layer, jnp.minimum(2 * k, 2 * nk - 1), 0, 0)),
            stat, stat, stat, stat,
            pl.BlockSpec((tm, d), lambda i, k: (i, 0), pipeline_mode=once),
        ],
        out_specs=pl.BlockSpec((tm, d), lambda i, k: (i, 0)),
        out_shape=jax.ShapeDtypeStruct((t, d), F32),
        scratch_shapes=[pltpu.VMEM((d, tm), F32), pltpu.VMEM((2, te, tm), F32), pltpu.VMEM((2, te, tm), BF16)],
        compiler_params=_cparams(("parallel", "arbitrary")),
        name="peer_main",
    )(xn_t, u, vt, vt, rk2, e2, n1, c1, res)


PEER_BLOCK = 512


def _transpose_block_body(v_ref, o_ref):
    o_ref[...] = v_ref[...].T.astype(BF16)


def peer_v_blocks(v):
    n_layers, n_e, d = v.shape
    te = PEER_BLOCK
    return pl.pallas_call(
        _transpose_block_body,
        grid=(n_layers, n_e // te),
        in_specs=[pl.BlockSpec((None, te, d), lambda l, e: (l, e, 0))],
        out_specs=pl.BlockSpec((None, None, d, te), lambda l, e: (l, e, 0, 0)),
        out_shape=jax.ShapeDtypeStruct((n_layers, n_e // te, d, te), BF16),
        compiler_params=_cparams(("parallel", "parallel")),
        name="peer_v_blocks",
    )(v)


def peer_weights(w_q, sub_keys, u, v):
    wq_t = jnp.swapaxes(w_q, 1, 2).astype(BF16)
    return wq_t, sub_keys.astype(BF16), u, peer_v_blocks(v)


def peer_ffn(x, gain, weights, layer):
    wq_t, sk, u, vt = weights
    xn_t, rk2, e2, n1, c1 = peer_prep(x, gain, wq_t, sk, layer)
    return peer_main(xn_t, u, vt, rk2, e2, n1, c1, x, layer)


def _row128(v):
    return v.reshape(1, LANES).astype(F32)


def mla_mixer(x, gain, positions, w_in, g_qa, w_qb, g_kva, w_kvb, g_q, g_k, w_o, b, s):
    t = b * s
    nh = MLA_HEADS
    w_in_p = jnp.pad(w_in, ((0, 0), (0, 64))).astype(BF16)
    z = norm_matmul(x, w_in_p, gain=gain, tn=w_in_p.shape[1], name="mla_in")
    wq = w_qb.reshape(MLA_Q_RANK, nh, MLA_QK)
    wq = jnp.concatenate([wq[:, :, :MLA_NOPE].reshape(MLA_Q_RANK, -1), wq[:, :, MLA_NOPE:].reshape(MLA_Q_RANK, -1)], 1)

    scale = LOG2_E / math.sqrt(MLA_QK)
    inv_freq = ROPE_THETA ** (-jnp.arange(0, MLA_ROPE, 2, dtype=F32) / MLA_ROPE)
    sign = jnp.where((jnp.arange(LANES) % 64) < 32, -1.0, 1.0)
    rows = [
        _row128(jnp.tile(inv_freq, 4)),
        _row128(sign),
        _row128(g_q[:MLA_NOPE] * scale),
        _row128(jnp.tile(g_q[MLA_NOPE:], 2) * scale),
        _row128(g_k[:MLA_NOPE]),
        _row128(jnp.pad(g_k[MLA_NOPE:], (0, 64))),
    ]
    pos_col = positions.astype(F32).reshape(t, 1)
    qf, kf, vf = mla_qkv_prep(z, wq.astype(BF16), w_kvb.astype(BF16), g_qa, g_kva, pos_col, rows, b, s)
    o = attention(qf, kf, vf, fox=False)
    return norm_matmul(o, w_o.astype(BF16), residual=x, tn=w_o.shape[1], name="mla_out")


def fox_mixer(x, gain, w_in, b_f, g_q, g_k, w_o, b, s):
    nh = FOX_HEADS
    d = x.shape[1]
    zmain = norm_matmul(x, w_in.astype(BF16), gain=gain, n_cols=4 * d, tm=1024, tn=1024, out_dtype=BF16,
                        name="fox_in")
    w_f = jnp.pad(w_in[:, 4 * d:], ((0, 0), (0, LANES - nh))).astype(BF16)
    flog = norm_matmul(x, w_f, gain=gain, tm=1024, name="fox_f")
    scale = LOG2_E / math.sqrt(FOX_HEAD_DIM)
    rows = [_row128(jnp.pad(b_f, (0, LANES - nh))), _row128(g_q * scale), _row128(g_k)]
    qf, kf, vf, ct = fox_prep(zmain, flog, rows, b, s)
    crow = ct[:, :nh, :]
    o = attention(qf, kf, vf, fox=True, ccol=crow[..., None], crow=crow, gate=zmain, gate_col0=3 * nh)
    return norm_matmul(o, w_o.astype(BF16), residual=x, tn=w_o.shape[1], name="fox_out")


def kernel(x, positions, norm_mix_g, norm_ffn_g, mla_w_in, mla_g_qa, mla_w_qb, mla_g_kva, mla_w_kvb, mla_g_q, mla_g_k, mla_w_o, fox_w_in, fox_b_f, fox_g_q, fox_g_k, fox_w_o, peer_w_q, peer_sub_keys, peer_u, peer_v):
    b, s, d = x.shape
    depth = norm_mix_g.shape[0]
    xt = x.reshape(b * s, d)
    peer_w = peer_weights(peer_w_q, peer_sub_keys, peer_u, peer_v)
    for i in range(depth):
        j = i // 2
        if i % 2 == 0:
            xt = mla_mixer(xt, norm_mix_g[i], positions, mla_w_in[j], mla_g_qa[j], mla_w_qb[j], mla_g_kva[j],
                           mla_w_kvb[j], mla_g_q[j], mla_g_k[j], mla_w_o[j], b, s)
        else:
            xt = fox_mixer(xt, norm_mix_g[i], fox_w_in[j], fox_b_f[j], fox_g_q[j], fox_g_k[j], fox_w_o[j], b, s)
        xt = peer_ffn(xt, norm_ffn_g[i], peer_w, i)
    return xt.reshape(b, s, d)
```

```python
import functools
import math

import jax
import jax.numpy as jnp
from jax import lax
from jax.experimental import pallas as pl
from jax.experimental.pallas import tpu as pltpu

F32 = jnp.float32
BF16 = jnp.bfloat16

RMS_EPS = 1e-6
NEG_INF = -1e30
LOG2_E = math.log2(math.e)
CHUNK_SHIFT = 6
ROPE_THETA = 10000.0

MLA_HEADS = 16
MLA_Q_RANK = 512
MLA_KV_RANK = 512
MLA_NOPE = 128
MLA_ROPE = 64
MLA_V = 128
MLA_QK = MLA_NOPE + MLA_ROPE

FOX_HEADS = 16
FOX_HEAD_DIM = 128

PEER_HEADS = 8
PEER_N_KEYS = 128
PEER_D_HALF = 128
PEER_TOPK = 16

LANES = 128
VMEM_LIMIT = 56 * 1024 * 1024


def _cparams(sem, flags=None):
    return pltpu.CompilerParams(dimension_semantics=sem, vmem_limit_bytes=VMEM_LIMIT, flags=flags)


def _nm_body(*refs, norm, has_res):
    it = iter(refs)
    a_ref = next(it)
    g_ref = next(it) if norm else None
    w_ref = next(it)
    r_ref = next(it) if has_res else None
    o_ref = next(it)
    an_ref = next(it) if norm else None

    if norm:
        @pl.when(pl.program_id(1) == 0)
        def _():
            a = a_ref[...].astype(F32)
            y = a * lax.rsqrt(jnp.mean(a * a, axis=-1, keepdims=True) + RMS_EPS)
            an_ref[...] = (y * g_ref[...]).astype(BF16)

        a_bf = an_ref[...]
    else:
        a_bf = a_ref[...]
    acc = jnp.dot(a_bf, w_ref[...], preferred_element_type=F32)
    if has_res:
        acc = acc + r_ref[...]
    o_ref[...] = acc.astype(o_ref.dtype)


def norm_matmul(a, w, *, gain=None, residual=None, a_col_block=0, n_cols=None, out_dtype=F32, tm=512, tn=512, name):
    m = a.shape[0]
    k = w.shape[0]
    n = w.shape[1] if n_cols is None else n_cols
    tn = min(tn, n)
    assert m % tm == 0 and n % tn == 0
    norm = gain is not None
    in_specs = [pl.BlockSpec((tm, k), lambda i, j: (i, a_col_block))]
    args = [a]
    if norm:
        in_specs.append(pl.BlockSpec((1, k), lambda i, j: (0, 0)))
        args.append(gain.reshape(1, k).astype(F32))
    in_specs.append(pl.BlockSpec((k, tn), lambda i, j: (0, j)))
    args.append(w)
    if residual is not None:
        in_specs.append(pl.BlockSpec((tm, tn), lambda i, j: (i, j)))
        args.append(residual)
    return pl.pallas_call(
        functools.partial(_nm_body, norm=norm, has_res=residual is not None),
        grid=(m // tm, n // tn),
        in_specs=in_specs,
        out_specs=pl.BlockSpec((tm, tn), lambda i, j: (i, j)),
        out_shape=jax.ShapeDtypeStruct((m, n), out_dtype),
        scratch_shapes=[pltpu.VMEM((tm, k), BF16)] if norm else [],
        compiler_params=_cparams(("parallel", "arbitrary")),
        name=name,
    )(*args)


def _swap_halves(x, first_half):
    return jnp.where(first_half, pltpu.roll(x, 96, 1), pltpu.roll(x, 32, 1))


def _mla_prep_body(q_ref, kv_ref, kr_ref, pos_ref, invf_ref, sign_ref, gqn_ref, gqr_ref, gkn_ref, gkr_ref,
                   qf_ref, kf_ref, vf_ref):
    ts = q_ref.shape[0]
    lane = lax.broadcasted_iota(jnp.int32, (ts, LANES), 1)
    first_half = (lane & 63) < 32
    low64 = lane < 64
    ang = pos_ref[...] * invf_ref[...]
    cosv = jnp.cos(ang)
    sinv = jnp.sin(ang) * sign_ref[...]

    def rot(x):
        return x * cosv + _swap_halves(x, first_half) * sinv

    inv_d = 1.0 / MLA_QK
    zeros = jnp.zeros((ts, LANES), F32)

    kr = kr_ref[...]
    ss_kr = jnp.sum(kr * kr, axis=-1, keepdims=True)
    kr_rot = rot(kr * gkr_ref[...])
    for h in range(MLA_HEADS):
        kn = kv_ref[:, h * 256:h * 256 + 128].astype(F32)
        r = lax.rsqrt((jnp.sum(kn * kn, axis=-1, keepdims=True) + ss_kr) * inv_d + RMS_EPS)
        kf_ref[0, h, :, 0:128] = (kn * r * gkn_ref[...]).astype(BF16)
        kf_ref[0, h, :, 128:256] = (kr_rot * r).astype(BF16)
        vf_ref[0, h] = kv_ref[:, h * 256 + 128:(h + 1) * 256].astype(BF16)

    for p in range(MLA_HEADS // 2):
        xr = q_ref[:, 2048 + p * 128:2048 + (p + 1) * 128].astype(F32)
        xr2 = xr * xr
        ss_e = jnp.sum(jnp.where(low64, xr2, 0.0), axis=-1, keepdims=True)
        ss_o = jnp.sum(jnp.where(low64, 0.0, xr2), axis=-1, keepdims=True)
        qn_e = q_ref[:, (2 * p) * 128:(2 * p + 1) * 128].astype(F32)
        qn_o = q_ref[:, (2 * p + 1) * 128:(2 * p + 2) * 128].astype(F32)
        r_e = lax.rsqrt((jnp.sum(qn_e * qn_e, axis=-1, keepdims=True) + ss_e) * inv_d + RMS_EPS)
        r_o = lax.rsqrt((jnp.sum(qn_o * qn_o, axis=-1, keepdims=True) + ss_o) * inv_d + RMS_EPS)
        xr_rot = rot(xr * jnp.where(low64, r_e, r_o) * gqr_ref[...])
        qf_ref[0, 2 * p, :, 0:128] = (qn_e * r_e * gqn_ref[...]).astype(BF16)
        qf_ref[0, 2 * p, :, 128:256] = jnp.where(low64, xr_rot, zeros).astype(BF16)
        qf_ref[0, 2 * p + 1, :, 0:128] = (qn_o * r_o * gqn_ref[...]).astype(BF16)
        qf_ref[0, 2 * p + 1, :, 128:256] = jnp.where(low64, pltpu.roll(xr_rot, 64, 1), zeros).astype(BF16)


def _mla_qkv_body(cq_ref, ckv_ref, gqa_ref, gkva_ref, wq_ref, wkv_ref, kr_ref, pos_ref, *rest):
    vec_refs, (qf_ref, kf_ref, vf_ref, q_s, kv_s) = rest[:6], rest[6:]

    def up_project(c_ref, g_ref, w_ref):
        c = c_ref[...]
        y = c * lax.rsqrt(jnp.mean(c * c, axis=-1, keepdims=True) + RMS_EPS)
        return jnp.dot((y * g_ref[...]).astype(BF16), w_ref[...], preferred_element_type=F32)

    q_s[...] = up_project(cq_ref, gqa_ref, wq_ref)
    kv_s[...] = up_project(ckv_ref, gkva_ref, wkv_ref)
    _mla_prep_body(q_s, kv_s, kr_ref, pos_ref, *vec_refs, qf_ref, kf_ref, vf_ref)


def mla_qkv_prep(z, wq, wkv, g_qa, g_kva, pos_col, rows, b, s, ts=256):
    ns = s // ts
    h = MLA_HEADS
    row = lambda blk: (lambda bi, si: (bi * ns + si, blk))
    vec = pl.BlockSpec((1, LANES), lambda bi, si: (0, 0))
    gain = lambda r: pl.BlockSpec((1, r), lambda bi, si: (0, 0))
    whole = lambda w: pl.BlockSpec(w.shape, lambda bi, si: (0, 0))
    head_out = lambda d: pl.BlockSpec((1, h, ts, d), lambda bi, si: (bi, 0, si, 0))
    return pl.pallas_call(
        _mla_qkv_body,
        grid=(b, ns),
        in_specs=[
            pl.BlockSpec((ts, MLA_Q_RANK), row(0)),
            pl.BlockSpec((ts, MLA_KV_RANK), row(MLA_Q_RANK // MLA_KV_RANK)),
            gain(MLA_Q_RANK), gain(MLA_KV_RANK), whole(wq), whole(wkv),
            pl.BlockSpec((ts, LANES), row((MLA_Q_RANK + MLA_KV_RANK) // LANES)),
            pl.BlockSpec((ts, 1), row(0)),
        ] + [vec] * 6,
        out_specs=[head_out(256), head_out(256), head_out(128)],
        out_shape=[
            jax.ShapeDtypeStruct((b, h, s, 256), BF16),
            jax.ShapeDtypeStruct((b, h, s, 256), BF16),
            jax.ShapeDtypeStruct((b, h, s, 128), BF16),
        ],
        scratch_shapes=[pltpu.VMEM((ts, wq.shape[1]), F32), pltpu.VMEM((ts, wkv.shape[1]), F32)],
        compiler_params=_cparams(("parallel", "parallel")),
        name="mla_qkv_prep",
    )(z, z, g_qa.reshape(1, -1).astype(F32), g_kva.reshape(1, -1).astype(F32), wq, wkv, z, pos_col, *rows)


def _split3(x):
    hi = x.astype(BF16)
    r1 = x - hi.astype(F32)
    mid = r1.astype(BF16)
    lo = (r1 - mid.astype(F32)).astype(BF16)
    return hi, mid, lo


def _fox_prep_body(z_ref, f_ref, bf_ref, gq_ref, gk_ref, qf_ref, kf_ref, vf_ref, ct_ref, carry_ref):
    ts = z_ref.shape[0]
    d = FOX_HEAD_DIM
    nh = FOX_HEADS

    @pl.when(pl.program_id(1) == 0)
    def _():
        carry_ref[...] = jnp.zeros_like(carry_ref)

    x = f_ref[...] + bf_ref[...]
    logf = jnp.minimum(x, 0.0) - jnp.log1p(jnp.exp(-jnp.abs(x)))
    tri = (lax.broadcasted_iota(jnp.int32, (ts, ts), 0) >= lax.broadcasted_iota(jnp.int32, (ts, ts), 1)).astype(BF16)
    hi, mid, lo = _split3(logf)
    local = (jnp.dot(tri, hi, preferred_element_type=F32) + jnp.dot(tri, mid, preferred_element_type=F32)
             + jnp.dot(tri, lo, preferred_element_type=F32))
    c = carry_ref[0:1, :] + local
    carry_ref[0:1, :] = c[ts - 1:ts, :]
    ct_ref[0] = (c * LOG2_E).T

    inv_d = 1.0 / d
    for h in range(nh):
        q = z_ref[:, h * d:(h + 1) * d].astype(F32)
        k = z_ref[:, (nh + h) * d:(nh + h + 1) * d].astype(F32)
        rq = lax.rsqrt(jnp.sum(q * q, axis=-1, keepdims=True) * inv_d + RMS_EPS)
        rk = lax.rsqrt(jnp.sum(k * k, axis=-1, keepdims=True) * inv_d + RMS_EPS)
        qf_ref[0, h] = (q * rq * gq_ref[...]).astype(BF16)
        kf_ref[0, h] = (k * rk * gk_ref[...]).astype(BF16)
        vf_ref[0, h] = z_ref[:, (2 * nh + h) * d:(2 * nh + h + 1) * d]


def fox_prep(zmain, flog, rows, b, s, ts=256):
    ns = s // ts
    h = FOX_HEADS
    row = lambda bi, si: (bi * ns + si, 0)
    vec = pl.BlockSpec((1, LANES), lambda bi, si: (0, 0))
    head_out = pl.BlockSpec((1, h, ts, 128), lambda bi, si: (bi, 0, si, 0))
    return pl.pallas_call(
        _fox_prep_body,
        grid=(b, ns),
        in_specs=[pl.BlockSpec((ts, zmain.shape[1]), row), pl.BlockSpec((ts, LANES), row), vec, vec, vec],
        out_specs=[head_out, head_out, head_out, pl.BlockSpec((1, LANES, ts), lambda bi, si: (bi, 0, si))],
        out_shape=[jax.ShapeDtypeStruct((b, h, s, 128), BF16)] * 3 + [jax.ShapeDtypeStruct((b, LANES, s), F32)],
        scratch_shapes=[pltpu.VMEM((8, LANES), F32)],
        compiler_params=_cparams(("parallel", "arbitrary")),
        name="fox_prep",
    )(zmain, flog, *rows)


_ATTN_HEADS = 2


def _attn_body(*refs, fox, tq, nq):
    if fox:
        q_ref, k_ref, v_ref, ccol_ref, crow_ref, gate_ref, o_ref, s_ref, p_ref = refs
    else:
        q_ref, k_ref, v_ref, o_ref, s_ref, p_ref = refs
    nt = (((1,), (1,)), ((), ()))
    half = tq // 2
    dv = v_ref.shape[-1]
    r = lax.broadcasted_iota(jnp.int32, (tq, tq), 0)
    c = lax.broadcasted_iota(jnp.int32, (tq, tq), 1)
    allowed = (c <= r) if fox else ((c >> CHUNK_SHIFT) <= (r >> CHUNK_SHIFT))
    for i in range(nq):
        for hh in range(_ATTN_HEADS):
            buf = 2 * hh + i % 2
            rows = slice(i * tq, (i + 1) * tq)
            q = q_ref[0, hh, rows, :]
            mx = jnp.full((tq, half), NEG_INF, F32)
            for j in range(i + 1):
                cols = slice(j * tq, (j + 1) * tq)
                s = lax.dot_general(q, k_ref[0, hh, cols, :], nt, preferred_element_type=F32)
                if fox:
                    s = s + (ccol_ref[0, hh, rows, :] - crow_ref[0, hh, :, cols])
                if j == i:
                    s = jnp.where(allowed, s, NEG_INF)
                s_ref[buf, :, cols] = s
                mx = jnp.maximum(mx, jnp.maximum(s[:, :half], s[:, half:]))
            m = jnp.max(mx, axis=-1, keepdims=True)
            ps = jnp.zeros((tq, half), F32)
            for j in range(i + 1):
                cols = slice(j * tq, (j + 1) * tq)
                p = jnp.exp2(s_ref[buf, :, cols] - m)
                ps = ps + (p[:, :half] + p[:, half:])
                p_ref[buf, :, cols] = p.astype(BF16)
            l = jnp.sum(ps, axis=-1, keepdims=True)
            n_keys = (i + 1) * tq
            o = jnp.dot(p_ref[buf, :, :n_keys], v_ref[0, hh, :n_keys, :], preferred_element_type=F32) / l
            out_cols = slice(hh * dv, (hh + 1) * dv)
            if fox:
                o = o * (1.0 / (1.0 + jnp.exp(-gate_ref[rows, out_cols].astype(F32))))
            o_ref[rows, out_cols] = o.astype(o_ref.dtype)


def attention(qf, kf, vf, *, fox, ccol=None, crow=None, gate=None, gate_col0=0, tq=256):
    b, h, s, dk = qf.shape
    dv = vf.shape[-1]
    nq = s // tq
    hps = _ATTN_HEADS
    head = lambda d: pl.BlockSpec((1, hps, s, d), lambda bi, hi: (bi, hi, 0, 0))
    in_specs = [head(dk), head(dk), head(dv)]
    args = [qf, kf, vf]
    if fox:
        in_specs += [
            head(1),
            pl.BlockSpec((1, hps, 1, s), lambda bi, hi: (bi, hi, 0, 0)),
            pl.BlockSpec((s, hps * dv), lambda bi, hi: (bi, gate_col0 // hps + hi)),
        ]
        args += [ccol, crow.reshape(b, h, 1, s), gate]
    return pl.pallas_call(
        functools.partial(_attn_body, fox=fox, tq=tq, nq=nq),
        grid=(b, h // hps),
        in_specs=in_specs,
        out_specs=pl.BlockSpec((s, hps * dv), lambda bi, hi: (bi, hi)),
        out_shape=jax.ShapeDtypeStruct((b * s, h * dv), BF16),
        scratch_shapes=[pltpu.VMEM((2 * hps, tq, s), F32), pltpu.VMEM((2 * hps, tq, s), BF16)],
        compiler_params=_cparams(("parallel", "parallel")),
        name="fox_attn" if fox else "mla_attn",
    )(*args)


_CAND_PAIRS = [(p, q) for p in range(PEER_TOPK) for q in range(PEER_TOPK) if (p + 1) * (q + 1) <= PEER_TOPK]
_CAND_ROWS = 56


def _top_rows(v, n, want_rank=False, exact=False):
    row_id = lax.broadcasted_iota(jnp.int32, v.shape, 0).astype(F32) if exact else None
    rank = jnp.full(v.shape, float(n), F32)
    out = []
    for r in range(n):
        m = jnp.max(v, axis=0, keepdims=True)
        out.append(m)
        hit = v == m
        if exact:
            hit = row_id == jnp.min(jnp.where(hit, row_id, float(v.shape[0])), axis=0, keepdims=True)
        if want_rank:
            rank = jnp.where(hit, float(r), rank)
        if r + 1 < n or exact:
            v = jnp.where(hit, -jnp.inf, v)
    return (out,) + ((rank,) if want_rank else ()) + ((v,) if exact else ())


def _count_ge(v, row):
    return jnp.sum(jnp.where(v >= row, 1.0, 0.0), axis=0, keepdims=True)


def _select_lane_group(s1, s2, cand_ref, exact):
    k = PEER_TOPK
    if exact:
        a1, rank1 = _top_rows(s1, k, want_rank=True, exact=True)[:2]
        a2, rank2 = _top_rows(s2, k, want_rank=True, exact=True)[:2]
    else:
        a1, = _top_rows(s1, k)
        a2, rank2 = _top_rows(s2, k, want_rank=True)
    for r, (p, q) in enumerate(_CAND_PAIRS):
        cand_ref[r:r + 1, :] = a1[p] + a2[q]
    cand = cand_ref[...]
    if exact:
        best, left = _top_rows(cand, k, exact=True)
        cand_ref[...] = jnp.where(left != cand, 1.0, 0.0)
    else:
        best, = _top_rows(cand, k)
    tau = best[k - 1]
    z = jnp.ones_like(tau)
    for r in range(1, k):
        z = z + jnp.exp(best[r] - best[0])
    n1 = jnp.zeros(s1.shape, F32)
    r = 0
    for p in range(k):
        n_p = jnp.zeros_like(tau)
        for q in range(k // (p + 1)):
            n_p = n_p + (cand_ref[r:r + 1, :] if exact else jnp.where(a1[p] + a2[q] >= tau, 1.0, 0.0))
            r += 1
        n1 = jnp.where((rank1 == float(p)) if exact else (s1 == a1[p]), n_p, n1)
    stats = (rank2.astype(BF16), jnp.exp(s2 - a2[0]).astype(BF16), n1, 0.5 * jnp.exp(s1 - a1[0]) / z)
    if exact:
        return stats
    ties = jnp.maximum(jnp.maximum(_count_ge(s1, a1[k - 1]), _count_ge(s2, a2[k - 1])), _count_ge(cand, tau)) > k
    return stats + (ties,)


def _peer_prep_body(x_ref, g_ref, wq_ref, sk_ref, xn_ref, rk2_ref, e2_ref, n1_ref, c1_ref, s_ref, cand_ref):
    @pl.when(pl.program_id(1) == 0)
    def _():
        a = x_ref[...]
        y = a * lax.rsqrt(jnp.mean(a * a, axis=-1, keepdims=True) + RMS_EPS)
        xn_ref[...] = (y * g_ref[...]).T.astype(BF16)

    qt = jnp.dot(wq_ref[...], xn_ref[...], preferred_element_type=F32)
    s_ref[0] = jnp.dot(sk_ref[0], qt[0:128].astype(BF16), preferred_element_type=F32)
    s_ref[1] = jnp.dot(sk_ref[1], qt[128:256].astype(BF16), preferred_element_type=F32)
    cand_ref[...] = jnp.full(cand_ref.shape, -jnp.inf, F32)

    def lane_groups(c, carry):
        def column(g):
            lanes = pl.ds(pl.multiple_of((2 * c + g) * LANES, LANES), LANES)
            slot = cand_ref.at[g]

            def store(rank2, e2, n1, c1):
                rk2_ref[0, :, lanes] = rank2
                e2_ref[0, :, lanes] = e2
                n1_ref[0, :, lanes] = n1
                c1_ref[0, :, lanes] = c1

            def select(exact):
                return _select_lane_group(s_ref[0, :, lanes], s_ref[1, :, lanes], slot, exact)

            *stats, ties = select(exact=False)
            store(*stats)

            def redo():
                store(*select(exact=True))
                slot[...] = jnp.full(slot.shape, -jnp.inf, F32)

            return jnp.max(jnp.where(ties, 1.0, 0.0)) > 0.0, redo

        fixes = [column(g) for g in range(2)]
        for tied, redo in fixes:
            pl.when(tied)(redo)
        return carry

    for c in range(s_ref.shape[-1] // (2 * LANES)):
        lane_groups(c, 0)


def peer_prep(x, gain, wq_t, sk, layer, tm=1024):
    t, d = x.shape
    nh = PEER_HEADS
    stat = pl.BlockSpec((1, PEER_N_KEYS, tm), lambda i, h: (h, 0, i))
    stat_f32 = jax.ShapeDtypeStruct((nh, PEER_N_KEYS, t), F32)
    stat_bf16 = jax.ShapeDtypeStruct((nh, PEER_N_KEYS, t), BF16)
    return pl.pallas_call(
        _peer_prep_body,
        grid=(t // tm, nh),
        in_specs=[
            pl.BlockSpec((tm, d), lambda i, h: (i, 0)),
            pl.BlockSpec((1, d), lambda i, h: (0, 0)),
            pl.BlockSpec((None, 2 * PEER_D_HALF, d), lambda i, h: (layer, h, 0)),
            pl.BlockSpec((None, 2, PEER_N_KEYS, PEER_D_HALF), lambda i, h: (layer, 0, 0, 0)),
        ],
        out_specs=[pl.BlockSpec((d, tm), lambda i, h: (0, i)), stat, stat, stat, stat],
        out_shape=[jax.ShapeDtypeStruct((d, t), BF16), stat_bf16, stat_bf16, stat_f32, stat_f32],
        scratch_shapes=[pltpu.VMEM((2, PEER_N_KEYS, tm), F32), pltpu.VMEM((2, _CAND_ROWS, LANES), F32)],
        compiler_params=_cparams(("parallel", "arbitrary")),
        name="peer_prep",
    )(x, gain.reshape(1, d).astype(F32), wq_t, sk)


_SLAB = 16
_GATE_LANES = 256


def _peer_main_body(xn_ref, u_ref, vta_ref, vtb_ref, rk2_ref, e2_ref, n1_ref, c1_ref, res_ref, o_ref,
                    acc_ref, h_ref, a_ref):
    k = pl.program_id(1)
    nk = pl.num_programs(1) - 1
    te, tm = h_ref.shape[1:]
    n_i = te // PEER_N_KEYS
    n_slab = PEER_N_KEYS // _SLAB
    zero = jnp.zeros((_SLAB, _GATE_LANES), BF16)

    def gate_stage(slot, blk):
        for lc in range(tm // _GATE_LANES):
            lanes = slice(lc * _GATE_LANES, (lc + 1) * _GATE_LANES)
            for ii in range(n_i):
                i = blk * n_i + ii
                g = [None] * n_slab
                for h in range(PEER_HEADS):
                    nb = jnp.broadcast_to(n1_ref[h, pl.ds(i, 1), lanes], (_SLAB, _GATE_LANES)).astype(BF16)
                    cb = jnp.broadcast_to(c1_ref[h, pl.ds(i, 1), lanes], (_SLAB, _GATE_LANES)).astype(BF16)
                    for sl in range(n_slab):
                        rows = slice(sl * _SLAB, (sl + 1) * _SLAB)
                        w = jnp.where(rk2_ref[h, rows, lanes] < nb, e2_ref[h, rows, lanes] * cb, zero)
                        g[sl] = w if g[sl] is None else g[sl] + w
                for sl in range(n_slab):
                    rows = slice(ii * PEER_N_KEYS + sl * _SLAB, ii * PEER_N_KEYS + (sl + 1) * _SLAB)
                    hh = h_ref[slot, rows, lanes]
                    act = hh * (1.0 + lax.erf(hh * (1.0 / math.sqrt(2.0))))
                    a_ref[slot, rows, lanes] = act.astype(BF16) * g[sl]

    def pre_activations(slot):
        h_ref[slot] = jnp.dot(u_ref[slot * te:(slot + 1) * te, :].astype(BF16), xn_ref[...], preferred_element_type=F32)

    def block_output(slot, vt_ref):
        return jnp.dot(vt_ref[...], a_ref[slot], preferred_element_type=F32)

    @pl.when(k == 0)
    def _():
        pre_activations(0)
        gate_stage(0, 0)
        pre_activations(1)
        acc_ref[...] = block_output(0, vtb_ref)

    @pl.when((k > 0) & (k < nk))
    def _():
        gate_stage(1, 2 * k - 1)
        pre_activations(0)
        acc_ref[...] += block_output(1, vta_ref)
        gate_stage(0, 2 * k)
        pre_activations(1)
        acc_ref[...] += block_output(0, vtb_ref)

    @pl.when(k == nk)
    def _():
        gate_stage(1, 2 * k - 1)
        o_ref[...] = res_ref[...] + (acc_ref[...] + block_output(1, vta_ref)).T


def peer_main(xn_t, u, vt, rk2, e2, n1, c1, res, layer, tm=512):
    d, t = xn_t.shape
    te = vt.shape[-1]
    nk = u.shape[1] // (2 * te)
    nh = PEER_HEADS
    once = pl.Buffered(1)
    stat = pl.BlockSpec((nh, PEER_N_KEYS, tm), lambda i, k: (0, 0, i), pipeline_mode=once)
    return pl.pallas_call(
        _peer_main_body,
        grid=(t // tm, nk + 1),
        in_specs=[
            pl.BlockSpec((d, tm), lambda i, k: (0, i), pipeline_mode=once),
            pl.BlockSpec((None, 2 * te, d), lambda i, k: (layer, jnp.minimum(k, nk - 1), 0)),
            pl.BlockSpec((None, None, d, te), lambda i, k: (layer, jnp.clip(2 * k - 1, 0, 2 * nk - 1), 0, 0)),
            pl.BlockSpec((None, None, d, te), lambda i, k: (layer, jnp.minimum(2 * k, 2 * nk - 1), 0, 0)),
            stat, stat, stat, stat,
            pl.BlockSpec((tm, d), lambda i, k: (i, 0), pipeline_mode=once),
        ],
        out_specs=pl.BlockSpec((tm, d), lambda i, k: (i, 0)),
        out_shape=jax.ShapeDtypeStruct((t, d), F32),
        scratch_shapes=[pltpu.VMEM((d, tm), F32), pltpu.VMEM((2, te, tm), F32), pltpu.VMEM((2, te, tm), BF16)],
        compiler_params=_cparams(("parallel", "arbitrary")),
        name="peer_main",
    )(xn_t, u, vt, vt, rk2, e2, n1, c1, res)


PEER_BLOCK = 512


def _transpose_block_body(v_ref, o_ref):
    o_ref[...] = v_ref[...].T.astype(BF16)


def peer_v_blocks(v):
    n_layers, n_e, d = v.shape
    te = PEER_BLOCK
    return pl.pallas_call(
        _transpose_block_body,
        grid=(n_layers, n_e // te),
        in_specs=[pl.BlockSpec((None, te, d), lambda l, e: (l, e, 0))],
        out_specs=pl.BlockSpec((None, None, d, te), lambda l, e: (l, e, 0, 0)),
        out_shape=jax.ShapeDtypeStruct((n_layers, n_e // te, d, te), BF16),
        compiler_params=_cparams(("parallel", "parallel")),
        name="peer_v_blocks",
    )(v)


def peer_weights(w_q, sub_keys, u, v):
    wq_t = jnp.swapaxes(w_q, 1, 2).astype(BF16)
    return wq_t, sub_keys.astype(BF16), u, peer_v_blocks(v)


def peer_ffn(x, gain, weights, layer):
    wq_t, sk, u, vt = weights
    xn_t, rk2, e2, n1, c1 = peer_prep(x, gain, wq_t, sk, layer)
    return peer_main(xn_t, u, vt, rk2, e2, n1, c1, x, layer)


def _row128(v):
    return v.reshape(1, LANES).astype(F32)


def mla_mixer(x, gain, positions, w_in, g_qa, w_qb, g_kva, w_kvb, g_q, g_k, w_o, b, s):
    t = b * s
    nh = MLA_HEADS
    w_in_p = jnp.pad(w_in, ((0, 0), (0, 64))).astype(BF16)
    z = norm_matmul(x, w_in_p, gain=gain, tn=w_in_p.shape[1], name="mla_in")
    wq = w_qb.reshape(MLA_Q_RANK, nh, MLA_QK)
    wq = jnp.concatenate([wq[:, :, :MLA_NOPE].reshape(MLA_Q_RANK, -1), wq[:, :, MLA_NOPE:].reshape(MLA_Q_RANK, -1)], 1)

    scale = LOG2_E / math.sqrt(MLA_QK)
    inv_freq = ROPE_THETA ** (-jnp.arange(0, MLA_ROPE, 2, dtype=F32) / MLA_ROPE)
    sign = jnp.where((jnp.arange(LANES) % 64) < 32, -1.0, 1.0)
    rows = [
        _row128(jnp.tile(inv_freq, 4)),
        _row128(sign),
        _row128(g_q[:MLA_NOPE] * scale),
        _row128(jnp.tile(g_q[MLA_NOPE:], 2) * scale),
        _row128(g_k[:MLA_NOPE]),
        _row128(jnp.pad(g_k[MLA_NOPE:], (0, 64))),
    ]
    pos_col = positions.astype(F32).reshape(t, 1)
    qf, kf, vf = mla_qkv_prep(z, wq.astype(BF16), w_kvb.astype(BF16), g_qa, g_kva, pos_col, rows, b, s)
    o = attention(qf, kf, vf, fox=False)
    return norm_matmul(o, w_o.astype(BF16), residual=x, tn=w_o.shape[1], name="mla_out")


def fox_mixer(x, gain, w_in, b_f, g_q, g_k, w_o, b, s):
    nh = FOX_HEADS
    d = x.shape[1]
    zmain = norm_matmul(x, w_in.astype(BF16), gain=gain, n_cols=4 * d, tm=1024, tn=1024, out_dtype=BF16,
                        name="fox_in")
    w_f = jnp.pad(w_in[:, 4 * d:], ((0, 0), (0, LANES - nh))).astype(BF16)
    flog = norm_matmul(x, w_f, gain=gain, tm=1024, name="fox_f")
    scale = LOG2_E / math.sqrt(FOX_HEAD_DIM)
    rows = [_row128(jnp.pad(b_f, (0, LANES - nh))), _row128(g_q * scale), _row128(g_k)]
    qf, kf, vf, ct = fox_prep(zmain, flog, rows, b, s)
    crow = ct[:, :nh, :]
    o = attention(qf, kf, vf, fox=True, ccol=crow[..., None], crow=crow, gate=zmain, gate_col0=3 * nh)
    return norm_matmul(o, w_o.astype(BF16), residual=x, tn=w_o.shape[1], name="fox_out")


def kernel(x, positions, norm_mix_g, norm_ffn_g, mla_w_in, mla_g_qa, mla_w_qb, mla_g_kva, mla_w_kvb, mla_g_q, mla_g_k, mla_w_o, fox_w_in, fox_b_f, fox_g_q, fox_g_k, fox_w_o, peer_w_q, peer_sub_keys, peer_u, peer_v):
    b, s, d = x.shape
    depth = norm_mix_g.shape[0]
    xt = x.reshape(b * s, d)
    peer_w = peer_weights(peer_w_q, peer_sub_keys, peer_u, peer_v)
    for i in range(depth):
        j = i // 2
        if i % 2 == 0:
            xt = mla_mixer(xt, norm_mix_g[i], positions, mla_w_in[j], mla_g_qa[j], mla_w_qb[j], mla_g_kva[j],
                           mla_w_kvb[j], mla_g_q[j], mla_g_k[j], mla_w_o[j], b, s)
        else:
            xt = fox_mixer(xt, norm_mix_g[i], fox_w_in[j], fox_b_f[j], fox_g_q[j], fox_g_k[j], fox_w_o[j], b, s)
        xt = peer_ffn(xt, norm_ffn_g[i], peer_w, i)
    return xt.reshape(b, s, d)
```

```python
import functools
import math

import jax
import jax.numpy as jnp
from jax import lax
from jax.experimental import pallas as pl
from jax.experimental.pallas import tpu as pltpu

F32 = jnp.float32
BF16 = jnp.bfloat16

RMS_EPS = 1e-6
NEG_INF = -1e30
LOG2_E = math.log2(math.e)
CHUNK_SHIFT = 6
ROPE_THETA = 10000.0

MLA_HEADS = 16
MLA_Q_RANK = 512
MLA_KV_RANK = 512
MLA_NOPE = 128
MLA_ROPE = 64
MLA_V = 128
MLA_QK = MLA_NOPE + MLA_ROPE

FOX_HEADS = 16
FOX_HEAD_DIM = 128

PEER_HEADS = 8
PEER_N_KEYS = 128
PEER_D_HALF = 128
PEER_TOPK = 16

LANES = 128
VMEM_LIMIT = 56 * 1024 * 1024


def _cparams(sem, flags=None):
    return pltpu.CompilerParams(dimension_semantics=sem, vmem_limit_bytes=VMEM_LIMIT, flags=flags)


def _nm_body(*refs, norm, has_res):
    it = iter(refs)
    a_ref = next(it)
    g_ref = next(it) if norm else None
    w_ref = next(it)
    r_ref = next(it) if has_res else None
    o_ref = next(it)
    an_ref = next(it) if norm else None

    if norm:
        @pl.when(pl.program_id(1) == 0)
        def _():
            a = a_ref[...].astype(F32)
            y = a * lax.rsqrt(jnp.mean(a * a, axis=-1, keepdims=True) + RMS_EPS)
            an_ref[...] = (y * g_ref[...]).astype(BF16)

        a_bf = an_ref[...]
    else:
        a_bf = a_ref[...]
    acc = jnp.dot(a_bf, w_ref[...], preferred_element_type=F32)
    if has_res:
        acc = acc + r_ref[...]
    o_ref[...] = acc.astype(o_ref.dtype)


def norm_matmul(a, w, *, gain=None, residual=None, a_col_block=0, n_cols=None, out_dtype=F32, tm=512, tn=512, name):
    m = a.shape[0]
    k = w.shape[0]
    n = w.shape[1] if n_cols is None else n_cols
    tn = min(tn, n)
    assert m % tm == 0 and n % tn == 0
    norm = gain is not None
    in_specs = [pl.BlockSpec((tm, k), lambda i, j: (i, a_col_block))]
    args = [a]
    if norm:
        in_specs.append(pl.BlockSpec((1, k), lambda i, j: (0, 0)))
        args.append(gain.reshape(1, k).astype(F32))
    in_specs.append(pl.BlockSpec((k, tn), lambda i, j: (0, j)))
    args.append(w)
    if residual is not None:
        in_specs.append(pl.BlockSpec((tm, tn), lambda i, j: (i, j)))
        args.append(residual)
    return pl.pallas_call(
        functools.partial(_nm_body, norm=norm, has_res=residual is not None),
        grid=(m // tm, n // tn),
        in_specs=in_specs,
        out_specs=pl.BlockSpec((tm, tn), lambda i, j: (i, j)),
        out_shape=jax.ShapeDtypeStruct((m, n), out_dtype),
        scratch_shapes=[pltpu.VMEM((tm, k), BF16)] if norm else [],
        compiler_params=_cparams(("parallel", "arbitrary")),
        name=name,
    )(*args)


def _swap_halves(x, first_half):
    return jnp.where(first_half, pltpu.roll(x, 96, 1), pltpu.roll(x, 32, 1))


def _mla_prep_body(q_ref, kv_ref, kr_ref, pos_ref, invf_ref, sign_ref, gqn_ref, gqr_ref, gkn_ref, gkr_ref,
                   qf_ref, kf_ref, vf_ref):
    ts = q_ref.shape[0]
    lane = lax.broadcasted_iota(jnp.int32, (ts, LANES), 1)
    first_half = (lane & 63) < 32
    low64 = lane < 64
    ang = pos_ref[...] * invf_ref[...]
    cosv = jnp.cos(ang)
    sinv = jnp.sin(ang) * sign_ref[...]

    def rot(x):
        return x * cosv + _swap_halves(x, first_half) * sinv

    inv_d = 1.0 / MLA_QK
    zeros = jnp.zeros((ts, LANES), F32)

    kr = kr_ref[...]
    ss_kr = jnp.sum(kr * kr, axis=-1, keepdims=True)
    kr_rot = rot(kr * gkr_ref[...])
    for h in range(MLA_HEADS):
        kn = kv_ref[:, h * 256:h * 256 + 128].astype(F32)
        r = lax.rsqrt((jnp.sum(kn * kn, axis=-1, keepdims=True) + ss_kr) * inv_d + RMS_EPS)
        kf_ref[0, h, :, 0:128] = (kn * r * gkn_ref[...]).astype(BF16)
        kf_ref[0, h, :, 128:256] = (kr_rot * r).astype(BF16)
        vf_ref[0, h] = kv_ref[:, h * 256 + 128:(h + 1) * 256].astype(BF16)

    for p in range(MLA_HEADS // 2):
        xr = q_ref[:, 2048 + p * 128:2048 + (p + 1) * 128].astype(F32)
        xr2 = xr * xr
        ss_e = jnp.sum(jnp.where(low64, xr2, 0.0), axis=-1, keepdims=True)
        ss_o = jnp.sum(jnp.where(low64, 0.0, xr2), axis=-1, keepdims=True)
        qn_e = q_ref[:, (2 * p) * 128:(2 * p + 1) * 128].astype(F32)
        qn_o = q_ref[:, (2 * p + 1) * 128:(2 * p + 2) * 128].astype(F32)
        r_e = lax.rsqrt((jnp.sum(qn_e * qn_e, axis=-1, keepdims=True) + ss_e) * inv_d + RMS_EPS)
        r_o = lax.rsqrt((jnp.sum(qn_o * qn_o, axis=-1, keepdims=True) + ss_o) * inv_d + RMS_EPS)
        xr_rot = rot(xr * jnp.where(low64, r_e, r_o) * gqr_ref[...])
        qf_ref[0, 2 * p, :, 0:128] = (qn_e * r_e * gqn_ref[...]).astype(BF16)
        qf_ref[0, 2 * p, :, 128:256] = jnp.where(low64, xr_rot, zeros).astype(BF16)
        qf_ref[0, 2 * p + 1, :, 0:128] = (qn_o * r_o * gqn_ref[...]).astype(BF16)
        qf_ref[0, 2 * p + 1, :, 128:256] = jnp.where(low64, pltpu.roll(xr_rot, 64, 1), zeros).astype(BF16)


def _mla_qkv_body(cq_ref, ckv_ref, gqa_ref, gkva_ref, wq_ref, wkv_ref, kr_ref, pos_ref, *rest):
    vec_refs, (qf_ref, kf_ref, vf_ref, q_s, kv_s) = rest[:6], rest[6:]

    def up_project(c_ref, g_ref, w_ref):
        c = c_ref[...]
        y = c * lax.rsqrt(jnp.mean(c * c, axis=-1, keepdims=True) + RMS_EPS)
        return jnp.dot((y * g_ref[...]).astype(BF16), w_ref[...], preferred_element_type=F32)

    q_s[...] = up_project(cq_ref, gqa_ref, wq_ref)
    kv_s[...] = up_project(ckv_ref, gkva_ref, wkv_ref)
    _mla_prep_body(q_s, kv_s, kr_ref, pos_ref, *vec_refs, qf_ref, kf_ref, vf_ref)


def mla_qkv_prep(z, wq, wkv, g_qa, g_kva, pos_col, rows, b, s, ts=256):
    ns = s // ts
    h = MLA_HEADS
    row = lambda blk: (lambda bi, si: (bi * ns + si, blk))
    vec = pl.BlockSpec((1, LANES), lambda bi, si: (0, 0))
    gain = lambda r: pl.BlockSpec((1, r), lambda bi, si: (0, 0))
    whole = lambda w: pl.BlockSpec(w.shape, lambda bi, si: (0, 0))
    head_out = lambda d: pl.BlockSpec((1, h, ts, d), lambda bi, si: (bi, 0, si, 0))
    return pl.pallas_call(
        _mla_qkv_body,
        grid=(b, ns),
        in_specs=[
            pl.BlockSpec((ts, MLA_Q_RANK), row(0)),
            pl.BlockSpec((ts, MLA_KV_RANK), row(MLA_Q_RANK // MLA_KV_RANK)),
            gain(MLA_Q_RANK), gain(MLA_KV_RANK), whole(wq), whole(wkv),
            pl.BlockSpec((ts, LANES), row((MLA_Q_RANK + MLA_KV_RANK) // LANES)),
            pl.BlockSpec((ts, 1), row(0)),
        ] + [vec] * 6,
        out_specs=[head_out(256), head_out(256), head_out(128)],
        out_shape=[
            jax.ShapeDtypeStruct((b, h, s, 256), BF16),
            jax.ShapeDtypeStruct((b, h, s, 256), BF16),
            jax.ShapeDtypeStruct((b, h, s, 128), BF16),
        ],
        scratch_shapes=[pltpu.VMEM((ts, wq.shape[1]), F32), pltpu.VMEM((ts, wkv.shape[1]), F32)],
        compiler_params=_cparams(("parallel", "parallel")),
        name="mla_qkv_prep",
    )(z, z, g_qa.reshape(1, -1).astype(F32), g_kva.reshape(1, -1).astype(F32), wq, wkv, z, pos_col, *rows)


def _split3(x):
    hi = x.astype(BF16)
    r1 = x - hi.astype(F32)
    mid = r1.astype(BF16)
    lo = (r1 - mid.astype(F32)).astype(BF16)
    return hi, mid, lo


def _fox_prep_body(z_ref, f_ref, bf_ref, gq_ref, gk_ref, qf_ref, kf_ref, vf_ref, ct_ref, carry_ref):
    ts = z_ref.shape[0]
    d = FOX_HEAD_DIM
    nh = FOX_HEADS

    @pl.when(pl.program_id(1) == 0)
    def _():
        carry_ref[...] = jnp.zeros_like(carry_ref)

    x = f_ref[...] + bf_ref[...]
    logf = jnp.minimum(x, 0.0) - jnp.log1p(jnp.exp(-jnp.abs(x)))
    tri = (lax.broadcasted_iota(jnp.int32, (ts, ts), 0) >= lax.broadcasted_iota(jnp.int32, (ts, ts), 1)).astype(BF16)
    hi, mid, lo = _split3(logf)
    local = (jnp.dot(tri, hi, preferred_element_type=F32) + jnp.dot(tri, mid, preferred_element_type=F32)
             + jnp.dot(tri, lo, preferred_element_type=F32))
    c = carry_ref[0:1, :] + local
    carry_ref[0:1, :] = c[ts - 1:ts, :]
    ct_ref[0] = (c * LOG2_E).T

    inv_d = 1.0 / d
    for h in range(nh):
        q = z_ref[:, h * d:(h + 1) * d].astype(F32)
        k = z_ref[:, (nh + h) * d:(nh + h + 1) * d].astype(F32)
        rq = lax.rsqrt(jnp.sum(q * q, axis=-1, keepdims=True) * inv_d + RMS_EPS)
        rk = lax.rsqrt(jnp.sum(k * k, axis=-1, keepdims=True) * inv_d + RMS_EPS)
        qf_ref[0, h] = (q * rq * gq_ref[...]).astype(BF16)
        kf_ref[0, h] = (k * rk * gk_ref[...]).astype(BF16)
        vf_ref[0, h] = z_ref[:, (2 * nh + h) * d:(2 * nh + h + 1) * d]


def fox_prep(zmain, flog, rows, b, s, ts=256):
    ns = s // ts
    h = FOX_HEADS
    row = lambda bi, si: (bi * ns + si, 0)
    vec = pl.BlockSpec((1, LANES), lambda bi, si: (0, 0))
    head_out = pl.BlockSpec((1, h, ts, 128), lambda bi, si: (bi, 0, si, 0))
    return pl.pallas_call(
        _fox_prep_body,
        grid=(b, ns),
        in_specs=[pl.BlockSpec((ts, zmain.shape[1]), row), pl.BlockSpec((ts, LANES), row), vec, vec, vec],
        out_specs=[head_out, head_out, head_out, pl.BlockSpec((1, LANES, ts), lambda bi, si: (bi, 0, si))],
        out_shape=[jax.ShapeDtypeStruct((b, h, s, 128), BF16)] * 3 + [jax.ShapeDtypeStruct((b, LANES, s), F32)],
        scratch_shapes=[pltpu.VMEM((8, LANES), F32)],
        compiler_params=_cparams(("parallel", "arbitrary")),
        name="fox_prep",
    )(zmain, flog, *rows)


_ATTN_HEADS = 2


def _attn_body(*refs, fox, tq, nq):
    if fox:
        q_ref, k_ref, v_ref, ccol_ref, crow_ref, gate_ref, o_ref, s_ref, p_ref = refs
    else:
        q_ref, k_ref, v_ref, o_ref, s_ref, p_ref = refs
    nt = (((1,), (1,)), ((), ()))
    half = tq // 2
    dv = v_ref.shape[-1]
    r = lax.broadcasted_iota(jnp.int32, (tq, tq), 0)
    c = lax.broadcasted_iota(jnp.int32, (tq, tq), 1)
    allowed = (c <= r) if fox else ((c >> CHUNK_SHIFT) <= (r >> CHUNK_SHIFT))
    for i in range(nq):
        for hh in range(_ATTN_HEADS):
            buf = 2 * hh + i % 2
            rows = slice(i * tq, (i + 1) * tq)
            q = q_ref[0, hh, rows, :]
            mx = jnp.full((tq, half), NEG_INF, F32)
            for j in range(i + 1):
                cols = slice(j * tq, (j + 1) * tq)
                s = lax.dot_general(q, k_ref[0, hh, cols, :], nt, preferred_element_type=F32)
                if fox:
                    s = s + (ccol_ref[0, hh, rows, :] - crow_ref[0, hh, :, cols])
                if j == i:
                    s = jnp.where(allowed, s, NEG_INF)
                s_ref[buf, :, cols] = s
                mx = jnp.maximum(mx, jnp.maximum(s[:, :half], s[:, half:]))
            m = jnp.max(mx, axis=-1, keepdims=True)
            ps = jnp.zeros((tq, half), F32)
            for j in range(i + 1):
                cols = slice(j * tq, (j + 1) * tq)
                p = jnp.exp2(s_ref[buf, :, cols] - m)
                ps = ps + (p[:, :half] + p[:, half:])
                p_ref[buf, :, cols] = p.astype(BF16)
            l = jnp.sum(ps, axis=-1, keepdims=True)
            n_keys = (i + 1) * tq
            o = jnp.dot(p_ref[buf, :, :n_keys], v_ref[0, hh, :n_keys, :], preferred_element_type=F32) / l
            out_cols = slice(hh * dv, (hh + 1) * dv)
            if fox:
                o = o * (1.0 / (1.0 + jnp.exp(-gate_ref[rows, out_cols].astype(F32))))
            o_ref[rows, out_cols] = o.astype(o_ref.dtype)


def attention(qf, kf, vf, *, fox, ccol=None, crow=None, gate=None, gate_col0=0, tq=256):
    b, h, s, dk = qf.shape
    dv = vf.shape[-1]
    nq = s // tq
    hps = _ATTN_HEADS
    head = lambda d: pl.BlockSpec((1, hps, s, d), lambda bi, hi: (bi, hi, 0, 0))
    in_specs = [head(dk), head(dk), head(dv)]
    args = [qf, kf, vf]
    if fox:
        in_specs += [
            head(1),
            pl.BlockSpec((1, hps, 1, s), lambda bi, hi: (bi, hi, 0, 0)),
            pl.BlockSpec((s, hps * dv), lambda bi, hi: (bi, gate_col0 // hps + hi)),
        ]
        args += [ccol, crow.reshape(b, h, 1, s), gate]
    return pl.pallas_call(
        functools.partial(_attn_body, fox=fox, tq=tq, nq=nq),
        grid=(b, h // hps),
        in_specs=in_specs,
        out_specs=pl.BlockSpec((s, hps * dv), lambda bi, hi: (bi, hi)),
        out_shape=jax.ShapeDtypeStruct((b * s, h * dv), BF16),
        scratch_shapes=[pltpu.VMEM((2 * hps, tq, s), F32), pltpu.VMEM((2 * hps, tq, s), BF16)],
        compiler_params=_cparams(("parallel", "parallel")),
        name="fox_attn" if fox else "mla_attn",
    )(*args)


_CAND_PAIRS = [(p, q) for p in range(PEER_TOPK) for q in range(PEER_TOPK) if (p + 1) * (q + 1) <= PEER_TOPK]
_CAND_ROWS = 56


def _top_rows(v, n, want_rank=False, exact=False):
    row_id = lax.broadcasted_iota(jnp.int32, v.shape, 0).astype(F32) if exact else None
    rank = jnp.full(v.shape, float(n), F32)
    out = []
    for r in range(n):
        m = jnp.max(v, axis=0, keepdims=True)
        out.append(m)
        hit = v == m
        if exact:
            hit = row_id == jnp.min(jnp.where(hit, row_id, float(v.shape[0])), axis=0, keepdims=True)
        if want_rank:
            rank = jnp.where(hit, float(r), rank)
        if r + 1 < n or exact:
            v = jnp.where(hit, -jnp.inf, v)
    return (out,) + ((rank,) if want_rank else ()) + ((v,) if exact else ())


def _count_ge(v, row):
    return jnp.sum(jnp.where(v >= row, 1.0, 0.0), axis=0, keepdims=True)


def _select_lane_group(s1, s2, cand_ref, exact):
    k = PEER_TOPK
    if exact:
        a1, rank1 = _top_rows(s1, k, want_rank=True, exact=True)[:2]
        a2, rank2 = _top_rows(s2, k, want_rank=True, exact=True)[:2]
    else:
        a1, = _top_rows(s1, k)
        a2, rank2 = _top_rows(s2, k, want_rank=True)
    for r, (p, q) in enumerate(_CAND_PAIRS):
        cand_ref[r:r + 1, :] = a1[p] + a2[q]
    cand = cand_ref[...]
    if exact:
        best, left = _top_rows(cand, k, exact=True)
        cand_ref[...] = jnp.where(left != cand, 1.0, 0.0)
    else:
        best, = _top_rows(cand, k)
    tau = best[k - 1]
    z = jnp.ones_like(tau)
    for r in range(1, k):
        z = z + jnp.exp(best[r] - best[0])
    n1 = jnp.zeros(s1.shape, F32)
    r = 0
    for p in range(k):
        n_p = jnp.zeros_like(tau)
        for q in range(k // (p + 1)):
            n_p = n_p + (cand_ref[r:r + 1, :] if exact else jnp.where(a1[p] + a2[q] >= tau, 1.0, 0.0))
            r += 1
        n1 = jnp.where((rank1 == float(p)) if exact else (s1 == a1[p]), n_p, n1)
    stats = (rank2.astype(BF16), jnp.exp(s2 - a2[0]).astype(BF16), n1, 0.5 * jnp.exp(s1 - a1[0]) / z)
    if exact:
        return stats
    ties = jnp.maximum(jnp.maximum(_count_ge(s1, a1[k - 1]), _count_ge(s2, a2[k - 1])), _count_ge(cand, tau)) > k
    return stats + (ties,)


def _peer_prep_body(x_ref, g_ref, wq_ref, sk_ref, xn_ref, rk2_ref, e2_ref, n1_ref, c1_ref, s_ref, cand_ref):
    @pl.when(pl.program_id(1) == 0)
    def _():
        a = x_ref[...]
        y = a * lax.rsqrt(jnp.mean(a * a, axis=-1, keepdims=True) + RMS_EPS)
        xn_ref[...] = (y * g_ref[...]).T.astype(BF16)

    qt = jnp.dot(wq_ref[...], xn_ref[...], preferred_element_type=F32)
    s_ref[0] = jnp.dot(sk_ref[0], qt[0:128].astype(BF16), preferred_element_type=F32)
    s_ref[1] = jnp.dot(sk_ref[1], qt[128:256].astype(BF16), preferred_element_type=F32)
    cand_ref[...] = jnp.full(cand_ref.shape, -jnp.inf, F32)

    def lane_groups(c, carry):
        def column(g):
            lanes = pl.ds(pl.multiple_of((2 * c + g) * LANES, LANES), LANES)
            slot = cand_ref.at[g]

            def store(rank2, e2, n1, c1):
                rk2_ref[0, :, lanes] = rank2
                e2_ref[0, :, lanes] = e2
                n1_ref[0, :, lanes] = n1
                c1_ref[0, :, lanes] = c1

            def select(exact):
                return _select_lane_group(s_ref[0, :, lanes], s_ref[1, :, lanes], slot, exact)

            *stats, ties = select(exact=False)
            store(*stats)

            def redo():
                store(*select(exact=True))
                slot[...] = jnp.full(slot.shape, -jnp.inf, F32)

            return jnp.max(jnp.where(ties, 1.0, 0.0)) > 0.0, redo

        fixes = [column(g) for g in range(2)]
        for tied, redo in fixes:
            pl.when(tied)(redo)
        return carry

    for c in range(s_ref.shape[-1] // (2 * LANES)):
        lane_groups(c, 0)


def peer_prep(x, gain, wq_t, sk, layer, tm=2048):
    t, d = x.shape
    nh = PEER_HEADS
    stat = pl.BlockSpec((1, PEER_N_KEYS, tm), lambda i, h: (h, 0, i))
    stat_f32 = jax.ShapeDtypeStruct((nh, PEER_N_KEYS, t), F32)
    stat_bf16 = jax.ShapeDtypeStruct((nh, PEER_N_KEYS, t), BF16)
    return pl.pallas_call(
        _peer_prep_body,
        grid=(t // tm, nh),
        in_specs=[
            pl.BlockSpec((tm, d), lambda i, h: (i, 0), pipeline_mode=pl.Buffered(1)),
            pl.BlockSpec((1, d), lambda i, h: (0, 0)),
            pl.BlockSpec((None, 2 * PEER_D_HALF, d), lambda i, h: (layer, h, 0)),
            pl.BlockSpec((None, 2, PEER_N_KEYS, PEER_D_HALF), lambda i, h: (layer, 0, 0, 0)),
        ],
        out_specs=[pl.BlockSpec((d, tm), lambda i, h: (0, i)), stat, stat, stat, stat],
        out_shape=[jax.ShapeDtypeStruct((d, t), BF16), stat_bf16, stat_bf16, stat_f32, stat_f32],
        scratch_shapes=[pltpu.VMEM((2, PEER_N_KEYS, tm), F32), pltpu.VMEM((2, _CAND_ROWS, LANES), F32)],
        compiler_params=_cparams(("parallel", "arbitrary")),
        name="peer_prep",
    )(x, gain.reshape(1, d).astype(F32), wq_t, sk)


_SLAB = 16
_GATE_LANES = 256


def _peer_main_body(xn_ref, u_ref, vta_ref, vtb_ref, rk2_ref, e2_ref, n1_ref, c1_ref, res_ref, o_ref,
                    acc_ref, h_ref, a_ref):
    k = pl.program_id(1)
    nk = pl.num_programs(1) - 1
    te, tm = h_ref.shape[1:]
    n_i = te // PEER_N_KEYS
    n_slab = PEER_N_KEYS // _SLAB
    zero = jnp.zeros((_SLAB, _GATE_LANES), BF16)

    def gate_stage(slot, blk):
        for lc in range(tm // _GATE_LANES):
            lanes = slice(lc * _GATE_LANES, (lc + 1) * _GATE_LANES)
            for ii in range(n_i):
                i = blk * n_i + ii
                g = [None] * n_slab
                for h in range(PEER_HEADS):
                    nb = jnp.broadcast_to(n1_ref[h, pl.ds(i, 1), lanes], (_SLAB, _GATE_LANES)).astype(BF16)
                    cb = jnp.broadcast_to(c1_ref[h, pl.ds(i, 1), lanes], (_SLAB, _GATE_LANES)).astype(BF16)
                    for sl in range(n_slab):
                        rows = slice(sl * _SLAB, (sl + 1) * _SLAB)
                        w = jnp.where(rk2_ref[h, rows, lanes] < nb, e2_ref[h, rows, lanes] * cb, zero)
                        g[sl] = w if g[sl] is None else g[sl] + w
                for sl in range(n_slab):
                    rows = slice(ii * PEER_N_KEYS + sl * _SLAB, ii * PEER_N_KEYS + (sl + 1) * _SLAB)
                    hh = h_ref[slot, rows, lanes]
                    act = hh * (1.0 + lax.erf(hh * (1.0 / math.sqrt(2.0))))
                    a_ref[slot, rows, lanes] = act.astype(BF16) * g[sl]

    def pre_activations(slot):
        h_ref[slot] = jnp.dot(u_ref[slot * te:(slot + 1) * te, :].astype(BF16), xn_ref[...], preferred_element_type=F32)

    def block_output(slot, vt_ref):
        return jnp.dot(vt_ref[...], a_ref[slot], preferred_element_type=F32)

    @pl.when(k == 0)
    def _():
        pre_activations(0)
        gate_stage(0, 0)
        pre_activations(1)
        acc_ref[...] = block_output(0, vtb_ref)

    @pl.when((k > 0) & (k < nk))
    def _():
        gate_stage(1, 2 * k - 1)
        pre_activations(0)
        acc_ref[...] += block_output(1, vta_ref)
        gate_stage(0, 2 * k)
        pre_activations(1)
        acc_ref[...] += block_output(0, vtb_ref)

    @pl.when(k == nk)
    def _():
        gate_stage(1, 2 * k - 1)
        o_ref[...] = res_ref[...] + (acc_ref[...] + block_output(1, vta_ref)).T


def peer_main(xn_t, u, vt, rk2, e2, n1, c1, res, layer, tm=512):
    d, t = xn_t.shape
    te = vt.shape[-1]
    nk = u.shape[1] // (2 * te)
    nh = PEER_HEADS
    once = pl.Buffered(1)
    stat = pl.BlockSpec((nh, PEER_N_KEYS, tm), lambda i, k: (0, 0, i), pipeline_mode=once)
    return pl.pallas_call(
        _peer_main_body,
        grid=(t // tm, nk + 1),
        in_specs=[
            pl.BlockSpec((d, tm), lambda i, k: (0, i), pipeline_mode=once),
            pl.BlockSpec((None, 2 * te, d), lambda i, k: (layer, jnp.minimum(k, nk - 1), 0)),
            pl.BlockSpec((None, None, d, te), lambda i, k: (layer, jnp.clip(2 * k - 1, 0, 2 * nk - 1), 0, 0)),
            pl.BlockSpec((None, None, d, te), lambda i, k: (layer, jnp.minimum(2 * k, 2 * nk - 1), 0, 0)),
            stat, stat, stat, stat,
            pl.BlockSpec((tm, d), lambda i, k: (i, 0), pipeline_mode=once),
        ],
        out_specs=pl.BlockSpec((tm, d), lambda i, k: (i, 0)),
        out_shape=jax.ShapeDtypeStruct((t, d), F32),
        scratch_shapes=[pltpu.VMEM((d, tm), F32), pltpu.VMEM((2, te, tm), F32), pltpu.VMEM((2, te, tm), BF16)],
        compiler_params=_cparams(("parallel", "arbitrary")),
        name="peer_main",
    )(xn_t, u, vt, vt, rk2, e2, n1, c1, res)


PEER_BLOCK = 512


def _transpose_block_body(v_ref, o_ref):
    o_ref[...] = v_ref[...].T.astype(BF16)


def peer_v_blocks(v):
    n_layers, n_e, d = v.shape
    te = PEER_BLOCK
    return pl.pallas_call(
        _transpose_block_body,
        grid=(n_layers, n_e // te),
        in_specs=[pl.BlockSpec((None, te, d), lambda l, e: (l, e, 0))],
        out_specs=pl.BlockSpec((None, None, d, te), lambda l, e: (l, e, 0, 0)),
        out_shape=jax.ShapeDtypeStruct((n_layers, n_e // te, d, te), BF16),
        compiler_params=_cparams(("parallel", "parallel")),
        name="peer_v_blocks",
    )(v)


def peer_weights(w_q, sub_keys, u, v):
    wq_t = jnp.swapaxes(w_q, 1, 2).astype(BF16)
    return wq_t, sub_keys.astype(BF16), u, peer_v_blocks(v)


def peer_ffn(x, gain, weights, layer):
    wq_t, sk, u, vt = weights
    xn_t, rk2, e2, n1, c1 = peer_prep(x, gain, wq_t, sk, layer)
    return peer_main(xn_t, u, vt, rk2, e2, n1, c1, x, layer)


def _row128(v):
    return v.reshape(1, LANES).astype(F32)


def mla_mixer(x, gain, positions, w_in, g_qa, w_qb, g_kva, w_kvb, g_q, g_k, w_o, b, s):
    t = b * s
    nh = MLA_HEADS
    w_in_p = jnp.pad(w_in, ((0, 0), (0, 64))).astype(BF16)
    z = norm_matmul(x, w_in_p, gain=gain, tn=w_in_p.shape[1], name="mla_in")
    wq = w_qb.reshape(MLA_Q_RANK, nh, MLA_QK)
    wq = jnp.concatenate([wq[:, :, :MLA_NOPE].reshape(MLA_Q_RANK, -1), wq[:, :, MLA_NOPE:].reshape(MLA_Q_RANK, -1)], 1)

    scale = LOG2_E / math.sqrt(MLA_QK)
    inv_freq = ROPE_THETA ** (-jnp.arange(0, MLA_ROPE, 2, dtype=F32) / MLA_ROPE)
    sign = jnp.where((jnp.arange(LANES) % 64) < 32, -1.0, 1.0)
    rows = [
        _row128(jnp.tile(inv_freq, 4)),
        _row128(sign),
        _row128(g_q[:MLA_NOPE] * scale),
        _row128(jnp.tile(g_q[MLA_NOPE:], 2) * scale),
        _row128(g_k[:MLA_NOPE]),
        _row128(jnp.pad(g_k[MLA_NOPE:], (0, 64))),
    ]
    pos_col = positions.astype(F32).reshape(t, 1)
    qf, kf, vf = mla_qkv_prep(z, wq.astype(BF16), w_kvb.astype(BF16), g_qa, g_kva, pos_col, rows, b, s)
    o = attention(qf, kf, vf, fox=False)
    return norm_matmul(o, w_o.astype(BF16), residual=x, tn=w_o.shape[1], name="mla_out")


def fox_mixer(x, gain, w_in, b_f, g_q, g_k, w_o, b, s):
    nh = FOX_HEADS
    d = x.shape[1]
    zmain = norm_matmul(x, w_in.astype(BF16), gain=gain, n_cols=4 * d, tm=1024, tn=1024, out_dtype=BF16,
                        name="fox_in")
    w_f = jnp.pad(w_in[:, 4 * d:], ((0, 0), (0, LANES - nh))).astype(BF16)
    flog = norm_matmul(x, w_f, gain=gain, tm=1024, name="fox_f")
    scale = LOG2_E / math.sqrt(FOX_HEAD_DIM)
    rows = [_row128(jnp.pad(b_f, (0, LANES - nh))), _row128(g_q * scale), _row128(g_k)]
    qf, kf, vf, ct = fox_prep(zmain, flog, rows, b, s)
    crow = ct[:, :nh, :]
    o = attention(qf, kf, vf, fox=True, ccol=crow[..., None], crow=crow, gate=zmain, gate_col0=3 * nh)
    return norm_matmul(o, w_o.astype(BF16), residual=x, tn=w_o.shape[1], name="fox_out")


def kernel(x, positions, norm_mix_g, norm_ffn_g, mla_w_in, mla_g_qa, mla_w_qb, mla_g_kva, mla_w_kvb, mla_g_q, mla_g_k, mla_w_o, fox_w_in, fox_b_f, fox_g_q, fox_g_k, fox_w_o, peer_w_q, peer_sub_keys, peer_u, peer_v):
    b, s, d = x.shape
    depth = norm_mix_g.shape[0]
    xt = x.reshape(b * s, d)
    peer_w = peer_weights(peer_w_q, peer_sub_keys, peer_u, peer_v)
    for i in range(depth):
        j = i // 2
        if i % 2 == 0:
            xt = mla_mixer(xt, norm_mix_g[i], positions, mla_w_in[j], mla_g_qa[j], mla_w_qb[j], mla_g_kva[j],
                           mla_w_kvb[j], mla_g_q[j], mla_g_k[j], mla_w_o[j], b, s)
        else:
            xt = fox_mixer(xt, norm_mix_g[i], fox_w_in[j], fox_b_f[j], fox_g_q[j], fox_g_k[j], fox_w_o[j], b, s)
        xt = peer_ffn(xt, norm_ffn_g[i], peer_w, i)
    return xt.reshape(b, s, d)
```

```python
import functools
import math

import jax
import jax.numpy as jnp
from jax import lax
from jax.experimental import pallas as pl
from jax.experimental.pallas import tpu as pltpu

F32 = jnp.float32
BF16 = jnp.bfloat16

RMS_EPS = 1e-6
NEG_INF = -1e30
LOG2_E = math.log2(math.e)
CHUNK_SHIFT = 6
ROPE_THETA = 10000.0

MLA_HEADS = 16
MLA_Q_RANK = 512
MLA_KV_RANK = 512
MLA_NOPE = 128
MLA_ROPE = 64
MLA_V = 128
MLA_QK = MLA_NOPE + MLA_ROPE

FOX_HEADS = 16
FOX_HEAD_DIM = 128

PEER_HEADS = 8
PEER_N_KEYS = 128
PEER_D_HALF = 128
PEER_TOPK = 16

LANES = 128
VMEM_LIMIT = 56 * 1024 * 1024


def _cparams(sem, flags=None):
    return pltpu.CompilerParams(dimension_semantics=sem, vmem_limit_bytes=VMEM_LIMIT, flags=flags)


def _nm_body(*refs, norm, has_res):
    it = iter(refs)
    a_ref = next(it)
    g_ref = next(it) if norm else None
    w_ref = next(it)
    r_ref = next(it) if has_res else None
    o_ref = next(it)
    an_ref = next(it) if norm else None

    if norm:
        @pl.when(pl.program_id(1) == 0)
        def _():
            a = a_ref[...].astype(F32)
            y = a * lax.rsqrt(jnp.mean(a * a, axis=-1, keepdims=True) + RMS_EPS)
            an_ref[...] = (y * g_ref[...]).astype(BF16)

        a_bf = an_ref[...]
    else:
        a_bf = a_ref[...]
    acc = jnp.dot(a_bf, w_ref[...], preferred_element_type=F32)
    if has_res:
        acc = acc + r_ref[...]
    o_ref[...] = acc.astype(o_ref.dtype)


def norm_matmul(a, w, *, gain=None, residual=None, a_col_block=0, n_cols=None, out_dtype=F32, tm=512, tn=512, name):
    m = a.shape[0]
    k = w.shape[0]
    n = w.shape[1] if n_cols is None else n_cols
    tn = min(tn, n)
    assert m % tm == 0 and n % tn == 0
    norm = gain is not None
    in_specs = [pl.BlockSpec((tm, k), lambda i, j: (i, a_col_block))]
    args = [a]
    if norm:
        in_specs.append(pl.BlockSpec((1, k), lambda i, j: (0, 0)))
        args.append(gain.reshape(1, k).astype(F32))
    in_specs.append(pl.BlockSpec((k, tn), lambda i, j: (0, j)))
    args.append(w)
    if residual is not None:
        in_specs.append(pl.BlockSpec((tm, tn), lambda i, j: (i, j)))
        args.append(residual)
    return pl.pallas_call(
        functools.partial(_nm_body, norm=norm, has_res=residual is not None),
        grid=(m // tm, n // tn),
        in_specs=in_specs,
        out_specs=pl.BlockSpec((tm, tn), lambda i, j: (i, j)),
        out_shape=jax.ShapeDtypeStruct((m, n), out_dtype),
        scratch_shapes=[pltpu.VMEM((tm, k), BF16)] if norm else [],
        compiler_params=_cparams(("parallel", "arbitrary")),
        name=name,
    )(*args)


def _swap_halves(x, first_half):
    return jnp.where(first_half, pltpu.roll(x, 96, 1), pltpu.roll(x, 32, 1))


def _mla_prep_body(q_ref, kv_ref, kr_ref, pos_ref, invf_ref, sign_ref, gqn_ref, gqr_ref, gkn_ref, gkr_ref,
                   qf_ref, kf_ref, vf_ref):
    ts = q_ref.shape[0]
    lane = lax.broadcasted_iota(jnp.int32, (ts, LANES), 1)
    first_half = (lane & 63) < 32
    low64 = lane < 64
    ang = pos_ref[...] * invf_ref[...]
    cosv = jnp.cos(ang)
    sinv = jnp.sin(ang) * sign_ref[...]

    def rot(x):
        return x * cosv + _swap_halves(x, first_half) * sinv

    inv_d = 1.0 / MLA_QK
    zeros = jnp.zeros((ts, LANES), F32)

    kr = kr_ref[...]
    ss_kr = jnp.sum(kr * kr, axis=-1, keepdims=True)
    kr_rot = rot(kr * gkr_ref[...])
    for h in range(MLA_HEADS):
        kn = kv_ref[:, h * 256:h * 256 + 128].astype(F32)
        r = lax.rsqrt((jnp.sum(kn * kn, axis=-1, keepdims=True) + ss_kr) * inv_d + RMS_EPS)
        kf_ref[0, h, :, 0:128] = (kn * r * gkn_ref[...]).astype(BF16)
        kf_ref[0, h, :, 128:256] = (kr_rot * r).astype(BF16)
        vf_ref[0, h] = kv_ref[:, h * 256 + 128:(h + 1) * 256].astype(BF16)

    for p in range(MLA_HEADS // 2):
        xr = q_ref[:, 2048 + p * 128:2048 + (p + 1) * 128].astype(F32)
        xr2 = xr * xr
        ss_e = jnp.sum(jnp.where(low64, xr2, 0.0), axis=-1, keepdims=True)
        ss_o = jnp.sum(jnp.where(low64, 0.0, xr2), axis=-1, keepdims=True)
        qn_e = q_ref[:, (2 * p) * 128:(2 * p + 1) * 128].astype(F32)
        qn_o = q_ref[:, (2 * p + 1) * 128:(2 * p + 2) * 128].astype(F32)
        r_e = lax.rsqrt((jnp.sum(qn_e * qn_e, axis=-1, keepdims=True) + ss_e) * inv_d + RMS_EPS)
        r_o = lax.rsqrt((jnp.sum(qn_o * qn_o, axis=-1, keepdims=True) + ss_o) * inv_d + RMS_EPS)
        xr_rot = rot(xr * jnp.where(low64, r_e, r_o) * gqr_ref[...])
        qf_ref[0, 2 * p, :, 0:128] = (qn_e * r_e * gqn_ref[...]).astype(BF16)
        qf_ref[0, 2 * p, :, 128:256] = jnp.where(low64, xr_rot, zeros).astype(BF16)
        qf_ref[0, 2 * p + 1, :, 0:128] = (qn_o * r_o * gqn_ref[...]).astype(BF16)
        qf_ref[0, 2 * p + 1, :, 128:256] = jnp.where(low64, pltpu.roll(xr_rot, 64, 1), zeros).astype(BF16)


def _mla_qkv_body(cq_ref, ckv_ref, gqa_ref, gkva_ref, wq_ref, wkv_ref, kr_ref, pos_ref, *rest):
    vec_refs, (qf_ref, kf_ref, vf_ref, q_s, kv_s) = rest[:6], rest[6:]

    def up_project(c_ref, g_ref, w_ref):
        c = c_ref[...]
        y = c * lax.rsqrt(jnp.mean(c * c, axis=-1, keepdims=True) + RMS_EPS)
        return jnp.dot((y * g_ref[...]).astype(BF16), w_ref[...], preferred_element_type=F32)

    q_s[...] = up_project(cq_ref, gqa_ref, wq_ref)
    kv_s[...] = up_project(ckv_ref, gkva_ref, wkv_ref)
    _mla_prep_body(q_s, kv_s, kr_ref, pos_ref, *vec_refs, qf_ref, kf_ref, vf_ref)


def mla_qkv_prep(z, wq, wkv, g_qa, g_kva, pos_col, rows, b, s, ts=256):
    ns = s // ts
    h = MLA_HEADS
    row = lambda blk: (lambda bi, si: (bi * ns + si, blk))
    vec = pl.BlockSpec((1, LANES), lambda bi, si: (0, 0))
    gain = lambda r: pl.BlockSpec((1, r), lambda bi, si: (0, 0))
    whole = lambda w: pl.BlockSpec(w.shape, lambda bi, si: (0, 0))
    head_out = lambda d: pl.BlockSpec((1, h, ts, d), lambda bi, si: (bi, 0, si, 0))
    return pl.pallas_call(
        _mla_qkv_body,
        grid=(b, ns),
        in_specs=[
            pl.BlockSpec((ts, MLA_Q_RANK), row(0)),
            pl.BlockSpec((ts, MLA_KV_RANK), row(MLA_Q_RANK // MLA_KV_RANK)),
            gain(MLA_Q_RANK), gain(MLA_KV_RANK), whole(wq), whole(wkv),
            pl.BlockSpec((ts, LANES), row((MLA_Q_RANK + MLA_KV_RANK) // LANES)),
            pl.BlockSpec((ts, 1), row(0)),
        ] + [vec] * 6,
        out_specs=[head_out(256), head_out(256), head_out(128)],
        out_shape=[
            jax.ShapeDtypeStruct((b, h, s, 256), BF16),
            jax.ShapeDtypeStruct((b, h, s, 256), BF16),
            jax.ShapeDtypeStruct((b, h, s, 128), BF16),
        ],
        scratch_shapes=[pltpu.VMEM((ts, wq.shape[1]), F32), pltpu.VMEM((ts, wkv.shape[1]), F32)],
        compiler_params=_cparams(("parallel", "parallel")),
        name="mla_qkv_prep",
    )(z, z, g_qa.reshape(1, -1).astype(F32), g_kva.reshape(1, -1).astype(F32), wq, wkv, z, pos_col, *rows)


def _split3(x):
    hi = x.astype(BF16)
    r1 = x - hi.astype(F32)
    mid = r1.astype(BF16)
    lo = (r1 - mid.astype(F32)).astype(BF16)
    return hi, mid, lo


def _fox_prep_body(z_ref, f_ref, bf_ref, gq_ref, gk_ref, qf_ref, kf_ref, vf_ref, ct_ref, carry_ref):
    ts = z_ref.shape[0]
    d = FOX_HEAD_DIM
    nh = FOX_HEADS

    @pl.when(pl.program_id(1) == 0)
    def _():
        carry_ref[...] = jnp.zeros_like(carry_ref)

    x = f_ref[...] + bf_ref[...]
    logf = jnp.minimum(x, 0.0) - jnp.log1p(jnp.exp(-jnp.abs(x)))
    tri = (lax.broadcasted_iota(jnp.int32, (ts, ts), 0) >= lax.broadcasted_iota(jnp.int32, (ts, ts), 1)).astype(BF16)
    hi, mid, lo = _split3(logf)
    local = (jnp.dot(tri, hi, preferred_element_type=F32) + jnp.dot(tri, mid, preferred_element_type=F32)
             + jnp.dot(tri, lo, preferred_element_type=F32))
    c = carry_ref[0:1, :] + local
    carry_ref[0:1, :] = c[ts - 1:ts, :]
    ct_ref[0] = (c * LOG2_E).T

    inv_d = 1.0 / d
    for h in range(nh):
        q = z_ref[:, h * d:(h + 1) * d].astype(F32)
        k = z_ref[:, (nh + h) * d:(nh + h + 1) * d].astype(F32)
        rq = lax.rsqrt(jnp.sum(q * q, axis=-1, keepdims=True) * inv_d + RMS_EPS)
        rk = lax.rsqrt(jnp.sum(k * k, axis=-1, keepdims=True) * inv_d + RMS_EPS)
        qf_ref[0, h] = (q * rq * gq_ref[...]).astype(BF16)
        kf_ref[0, h] = (k * rk * gk_ref[...]).astype(BF16)
        vf_ref[0, h] = z_ref[:, (2 * nh + h) * d:(2 * nh + h + 1) * d]


def fox_prep(zmain, flog, rows, b, s, ts=256):
    ns = s // ts
    h = FOX_HEADS
    row = lambda bi, si: (bi * ns + si, 0)
    vec = pl.BlockSpec((1, LANES), lambda bi, si: (0, 0))
    head_out = pl.BlockSpec((1, h, ts, 128), lambda bi, si: (bi, 0, si, 0))
    return pl.pallas_call(
        _fox_prep_body,
        grid=(b, ns),
        in_specs=[pl.BlockSpec((ts, zmain.shape[1]), row), pl.BlockSpec((ts, LANES), row), vec, vec, vec],
        out_specs=[head_out, head_out, head_out, pl.BlockSpec((1, LANES, ts), lambda bi, si: (bi, 0, si))],
        out_shape=[jax.ShapeDtypeStruct((b, h, s, 128), BF16)] * 3 + [jax.ShapeDtypeStruct((b, LANES, s), F32)],
        scratch_shapes=[pltpu.VMEM((8, LANES), F32)],
        compiler_params=_cparams(("parallel", "arbitrary")),
        name="fox_prep",
    )(zmain, flog, *rows)


_ATTN_HEADS = 2


def _attn_body(*refs, fox, tq, nq):
    if fox:
        q_ref, k_ref, v_ref, ccol_ref, crow_ref, gate_ref, o_ref, s_ref, p_ref = refs
    else:
        q_ref, k_ref, v_ref, o_ref, s_ref, p_ref = refs
    nt = (((1,), (1,)), ((), ()))
    half = tq // 2
    dv = v_ref.shape[-1]
    r = lax.broadcasted_iota(jnp.int32, (tq, tq), 0)
    c = lax.broadcasted_iota(jnp.int32, (tq, tq), 1)
    allowed = (c <= r) if fox else ((c >> CHUNK_SHIFT) <= (r >> CHUNK_SHIFT))
    for i in range(nq):
        for hh in range(_ATTN_HEADS):
            buf = 2 * hh + i % 2
            rows = slice(i * tq, (i + 1) * tq)
            q = q_ref[0, hh, rows, :]
            mx = jnp.full((tq, half), NEG_INF, F32)
            for j in range(i + 1):
                cols = slice(j * tq, (j + 1) * tq)
                s = lax.dot_general(q, k_ref[0, hh, cols, :], nt, preferred_element_type=F32)
                if fox:
                    s = s + (ccol_ref[0, hh, rows, :] - crow_ref[0, hh, :, cols])
                if j == i:
                    s = jnp.where(allowed, s, NEG_INF)
                s_ref[buf, :, cols] = s
                mx = jnp.maximum(mx, jnp.maximum(s[:, :half], s[:, half:]))
            m = jnp.max(mx, axis=-1, keepdims=True)
            ps = jnp.zeros((tq, half), F32)
            for j in range(i + 1):
                cols = slice(j * tq, (j + 1) * tq)
                p = jnp.exp2(s_ref[buf, :, cols] - m)
                ps = ps + (p[:, :half] + p[:, half:])
                p_ref[buf, :, cols] = p.astype(BF16)
            l = jnp.sum(ps, axis=-1, keepdims=True)
            n_keys = (i + 1) * tq
            o = jnp.dot(p_ref[buf, :, :n_keys], v_ref[0, hh, :n_keys, :], preferred_element_type=F32) / l
            out_cols = slice(hh * dv, (hh + 1) * dv)
            if fox:
                o = o * (1.0 / (1.0 + jnp.exp(-gate_ref[rows, out_cols].astype(F32))))
            o_ref[rows, out_cols] = o.astype(o_ref.dtype)


def attention(qf, kf, vf, *, fox, ccol=None, crow=None, gate=None, gate_col0=0, tq=256):
    b, h, s, dk = qf.shape
    dv = vf.shape[-1]
    nq = s // tq
    hps = _ATTN_HEADS
    head = lambda d: pl.BlockSpec((1, hps, s, d), lambda bi, hi: (bi, hi, 0, 0))
    in_specs = [head(dk), head(dk), head(dv)]
    args = [qf, kf, vf]
    if fox:
        in_specs += [
            head(1),
            pl.BlockSpec((1, hps, 1, s), lambda bi, hi: (bi, hi, 0, 0)),
            pl.BlockSpec((s, hps * dv), lambda bi, hi: (bi, gate_col0 // hps + hi)),
        ]
        args += [ccol, crow.reshape(b, h, 1, s), gate]
    return pl.pallas_call(
        functools.partial(_attn_body, fox=fox, tq=tq, nq=nq),
        grid=(b, h // hps),
        in_specs=in_specs,
        out_specs=pl.BlockSpec((s, hps * dv), lambda bi, hi: (bi, hi)),
        out_shape=jax.ShapeDtypeStruct((b * s, h * dv), BF16),
        scratch_shapes=[pltpu.VMEM((2 * hps, tq, s), F32), pltpu.VMEM((2 * hps, tq, s), BF16)],
        compiler_params=_cparams(("parallel", "parallel")),
        name="fox_attn" if fox else "mla_attn",
    )(*args)


_CAND_PAIRS = [(p, q) for p in range(PEER_TOPK) for q in range(PEER_TOPK) if (p + 1) * (q + 1) <= PEER_TOPK]
_CAND_ROWS = 56


def _top_rows(v, n, want_rank=False, exact=False):
    row_id = lax.broadcasted_iota(jnp.int32, v.shape, 0).astype(F32) if exact else None
    rank = jnp.full(v.shape, float(n), F32)
    out = []
    for r in range(n):
        m = jnp.max(v, axis=0, keepdims=True)
        out.append(m)
        hit = v == m
        if exact:
            hit = row_id == jnp.min(jnp.where(hit, row_id, float(v.shape[0])), axis=0, keepdims=True)
        if want_rank:
            rank = jnp.where(hit, float(r), rank)
        if r + 1 < n or exact:
            v = jnp.where(hit, -jnp.inf, v)
    return (out,) + ((rank,) if want_rank else ()) + ((v,) if exact else ())


def _count_ge(v, row):
    return jnp.sum(jnp.where(v >= row, 1.0, 0.0), axis=0, keepdims=True)


def _select_lane_group(s1, s2, cand_ref, exact):
    k = PEER_TOPK
    if exact:
        a1, rank1 = _top_rows(s1, k, want_rank=True, exact=True)[:2]
        a2, rank2 = _top_rows(s2, k, want_rank=True, exact=True)[:2]
    else:
        a1, = _top_rows(s1, k)
        a2, rank2 = _top_rows(s2, k, want_rank=True)
    for r, (p, q) in enumerate(_CAND_PAIRS):
        cand_ref[r:r + 1, :] = a1[p] + a2[q]
    cand = cand_ref[...]
    if exact:
        best, left = _top_rows(cand, k, exact=True)
        cand_ref[...] = jnp.where(left != cand, 1.0, 0.0)
    else:
        best, = _top_rows(cand, k)
    tau = best[k - 1]
    z = jnp.ones_like(tau)
    for r in range(1, k):
        z = z + jnp.exp(best[r] - best[0])
    n1 = jnp.zeros(s1.shape, F32)
    r = 0
    for p in range(k):
        n_p = jnp.zeros_like(tau)
        for q in range(k // (p + 1)):
            n_p = n_p + (cand_ref[r:r + 1, :] if exact else jnp.where(a1[p] + a2[q] >= tau, 1.0, 0.0))
            r += 1
        n1 = jnp.where((rank1 == float(p)) if exact else (s1 == a1[p]), n_p, n1)
    stats = (rank2.astype(BF16), jnp.exp(s2 - a2[0]).astype(BF16), n1, 0.5 * jnp.exp(s1 - a1[0]) / z)
    if exact:
        return stats
    ties = jnp.maximum(jnp.maximum(_count_ge(s1, a1[k - 1]), _count_ge(s2, a2[k - 1])), _count_ge(cand, tau)) > k
    return stats + (ties,)


def _peer_prep_body(x_ref, g_ref, wq_ref, sk_ref, xn_ref, rk2_ref, e2_ref, n1_ref, c1_ref, s_ref, cand_ref):
    @pl.when(pl.program_id(1) == 0)
    def _():
        a = x_ref[...]
        y = a * lax.rsqrt(jnp.mean(a * a, axis=-1, keepdims=True) + RMS_EPS)
        xn_ref[...] = (y * g_ref[...]).T.astype(BF16)

    qt = jnp.dot(wq_ref[...], xn_ref[...], preferred_element_type=F32)
    s_ref[0] = jnp.dot(sk_ref[0], qt[0:128].astype(BF16), preferred_element_type=F32)
    s_ref[1] = jnp.dot(sk_ref[1], qt[128:256].astype(BF16), preferred_element_type=F32)
    cand_ref[...] = jnp.full(cand_ref.shape, -jnp.inf, F32)

    def lane_groups(c, carry):
        def column(g):
            lanes = pl.ds(pl.multiple_of((2 * c + g) * LANES, LANES), LANES)
            slot = cand_ref.at[g]

            def store(rank2, e2, n1, c1):
                rk2_ref[0, :, lanes] = rank2
                e2_ref[0, :, lanes] = e2
                n1_ref[0, :, lanes] = n1
                c1_ref[0, :, lanes] = c1

            def select(exact):
                return _select_lane_group(s_ref[0, :, lanes], s_ref[1, :, lanes], slot, exact)

            *stats, ties = select(exact=False)
            store(*stats)

            def redo():
                store(*select(exact=True))
                slot[...] = jnp.full(slot.shape, -jnp.inf, F32)

            return jnp.max(jnp.where(ties, 1.0, 0.0)) > 0.0, redo

        fixes = [column(g) for g in range(2)]
        for tied, redo in fixes:
            pl.when(tied)(redo)
        return carry

    for c in range(s_ref.shape[-1] // (2 * LANES)):
        lane_groups(c, 0)


def peer_prep(x, gain, wq_t, sk, layer, tm=1024):
    t, d = x.shape
    nh = PEER_HEADS
    stat = pl.BlockSpec((1, PEER_N_KEYS, tm), lambda i, h: (h, 0, i))
    stat_f32 = jax.ShapeDtypeStruct((nh, PEER_N_KEYS, t), F32)
    stat_bf16 = jax.ShapeDtypeStruct((nh, PEER_N_KEYS, t), BF16)
    return pl.pallas_call(
        _peer_prep_body,
        grid=(t // tm, nh),
        in_specs=[
            pl.BlockSpec((tm, d), lambda i, h: (i, 0)),
            pl.BlockSpec((1, d), lambda i, h: (0, 0)),
            pl.BlockSpec((None, 2 * PEER_D_HALF, d), lambda i, h: (layer, h, 0)),
            pl.BlockSpec((None, 2, PEER_N_KEYS, PEER_D_HALF), lambda i, h: (layer, 0, 0, 0)),
        ],
        out_specs=[pl.BlockSpec((d, tm), lambda i, h: (0, i)), stat, stat, stat, stat],
        out_shape=[jax.ShapeDtypeStruct((d, t), BF16), stat_bf16, stat_bf16, stat_f32, stat_f32],
        scratch_shapes=[pltpu.VMEM((2, PEER_N_KEYS, tm), F32), pltpu.VMEM((2, _CAND_ROWS, LANES), F32)],
        compiler_params=_cparams(("parallel", "arbitrary")),
        name="peer_prep",
    )(x, gain.reshape(1, d).astype(F32), wq_t, sk)


_SLAB = 16
_GATE_LANES = 256


def _peer_main_body(xn_ref, u_ref, vta_ref, vtb_ref, rk2_ref, e2_ref, n1_ref, c1_ref, res_ref, o_ref,
                    acc_ref, h_ref, a_ref):
    k = pl.program_id(1)
    nk = pl.num_programs(1) - 1
    te, tm = h_ref.shape[1:]
    n_i = te // PEER_N_KEYS
    n_slab = PEER_N_KEYS // _SLAB
    zero = jnp.zeros((_SLAB, _GATE_LANES), BF16)

    def gate_stage(slot, blk):
        for lc in range(tm // _GATE_LANES):
            lanes = slice(lc * _GATE_LANES, (lc + 1) * _GATE_LANES)
            for ii in range(n_i):
                i = blk * n_i + ii
                g = [None] * n_slab
                for h in range(PEER_HEADS):
                    nb = jnp.broadcast_to(n1_ref[h, pl.ds(i, 1), lanes], (_SLAB, _GATE_LANES)).astype(BF16)
                    cb = jnp.broadcast_to(c1_ref[h, pl.ds(i, 1), lanes], (_SLAB, _GATE_LANES)).astype(BF16)
                    for sl in range(n_slab):
                        rows = slice(sl * _SLAB, (sl + 1) * _SLAB)
                        w = jnp.where(rk2_ref[h, rows, lanes] < nb, e2_ref[h, rows, lanes] * cb, zero)
                        g[sl] = w if g[sl] is None else g[sl] + w
                for sl in range(n_slab):
                    rows = slice(ii * PEER_N_KEYS + sl * _SLAB, ii * PEER_N_KEYS + (sl + 1) * _SLAB)
                    hh = h_ref[slot, rows, lanes]
                    act = hh * (1.0 + lax.erf(hh * (1.0 / math.sqrt(2.0))))
                    a_ref[slot, rows, lanes] = act.astype(BF16) * g[sl]

    def pre_activations(slot):
        h_ref[slot] = jnp.dot(u_ref[slot * te:(slot + 1) * te, :].astype(BF16), xn_ref[...], preferred_element_type=F32)

    def block_output(slot, vt_ref):
        return jnp.dot(vt_ref[...], a_ref[slot], preferred_element_type=F32)

    @pl.when(k == 0)
    def _():
        pre_activations(0)
        gate_stage(0, 0)
        pre_activations(1)
        acc_ref[...] = block_output(0, vtb_ref)

    @pl.when((k > 0) & (k < nk))
    def _():
        gate_stage(1, 2 * k - 1)
        pre_activations(0)
        acc_ref[...] += block_output(1, vta_ref)
        gate_stage(0, 2 * k)
        pre_activations(1)
        acc_ref[...] += block_output(0, vtb_ref)

    @pl.when(k == nk)
    def _():
        gate_stage(1, 2 * k - 1)
        o_ref[...] = res_ref[...] + (acc_ref[...] + block_output(1, vta_ref)).T


def peer_main(xn_t, u, vt, rk2, e2, n1, c1, res, layer, tm=512):
    d, t = xn_t.shape
    te = vt.shape[-1]
    nk = u.shape[1] // (2 * te)
    nh = PEER_HEADS
    once = pl.Buffered(1)
    stat = pl.BlockSpec((nh, PEER_N_KEYS, tm), lambda i, k: (0, 0, i), pipeline_mode=once)
    return pl.pallas_call(
        _peer_main_body,
        grid=(t // tm, nk + 1),
        in_specs=[
            pl.BlockSpec((d, tm), lambda i, k: (0, i), pipeline_mode=once),
            pl.BlockSpec((None, 2 * te, d), lambda i, k: (layer, jnp.minimum(k, nk - 1), 0)),
            pl.BlockSpec((None, None, d, te), lambda i, k: (layer, jnp.clip(2 * k - 1, 0, 2 * nk - 1), 0, 0)),
            pl.BlockSpec((None, None, d, te), lambda i, k: (layer, jnp.minimum(2 * k, 2 * nk - 1), 0, 0)),
            stat, stat, stat, stat,
            pl.BlockSpec((tm, d), lambda i, k: (i, 0), pipeline_mode=once),
        ],
        out_specs=pl.BlockSpec((tm, d), lambda i, k: (i, 0)),
        out_shape=jax.ShapeDtypeStruct((t, d), F32),
        scratch_shapes=[pltpu.VMEM((d, tm), F32), pltpu.VMEM((2, te, tm), F32), pltpu.VMEM((2, te, tm), BF16)],
        compiler_params=_cparams(("parallel", "arbitrary")),
        name="peer_main",
    )(xn_t, u, vt, vt, rk2, e2, n1, c1, res)


PEER_BLOCK = 512


def _transpose_block_body(v_ref, o_ref):
    o_ref[...] = v_ref[...].T.astype(BF16)


def peer_v_blocks(v):
    n_layers, n_e, d = v.shape
    te = PEER_BLOCK
    return pl.pallas_call(
        _transpose_block_body,
        grid=(n_layers, n_e // te),
        in_specs=[pl.BlockSpec((None, te, d), lambda l, e: (l, e, 0))],
        out_specs=pl.BlockSpec((None, None, d, te), lambda l, e: (l, e, 0, 0)),
        out_shape=jax.ShapeDtypeStruct((n_layers, n_e // te, d, te), BF16),
        compiler_params=_cparams(("parallel", "parallel")),
        name="peer_v_blocks",
    )(v)


def peer_wq_t(w_q):
    n_layers, d_in, d_out = w_q.shape
    te = PEER_BLOCK
    return pl.pallas_call(
        _transpose_block_body,
        grid=(n_layers, d_in // te),
        in_specs=[pl.BlockSpec((None, te, d_out), lambda l, e: (l, e, 0))],
        out_specs=pl.BlockSpec((None, d_out, te), lambda l, e: (l, 0, e)),
        out_shape=jax.ShapeDtypeStruct((n_layers, d_out, d_in), BF16),
        compiler_params=_cparams(("parallel", "parallel")),
        name="peer_wq_t",
    )(w_q)


def peer_weights(w_q, sub_keys, u, v):
    return peer_wq_t(w_q), sub_keys.astype(BF16), u, peer_v_blocks(v)


def peer_ffn(x, gain, weights, layer):
    wq_t, sk, u, vt = weights
    xn_t, rk2, e2, n1, c1 = peer_prep(x, gain, wq_t, sk, layer)
    return peer_main(xn_t, u, vt, rk2, e2, n1, c1, x, layer)


def _row128(v):
    return v.reshape(1, LANES).astype(F32)


def mla_mixer(x, gain, positions, w_in, g_qa, w_qb, g_kva, w_kvb, g_q, g_k, w_o, b, s):
    t = b * s
    nh = MLA_HEADS
    w_in_p = jnp.pad(w_in, ((0, 0), (0, 64))).astype(BF16)
    z = norm_matmul(x, w_in_p, gain=gain, tn=w_in_p.shape[1], name="mla_in")
    wq = w_qb.reshape(MLA_Q_RANK, nh, MLA_QK)
    wq = jnp.concatenate([wq[:, :, :MLA_NOPE].reshape(MLA_Q_RANK, -1), wq[:, :, MLA_NOPE:].reshape(MLA_Q_RANK, -1)], 1)

    scale = LOG2_E / math.sqrt(MLA_QK)
    inv_freq = ROPE_THETA ** (-jnp.arange(0, MLA_ROPE, 2, dtype=F32) / MLA_ROPE)
    sign = jnp.where((jnp.arange(LANES) % 64) < 32, -1.0, 1.0)
    rows = [
        _row128(jnp.tile(inv_freq, 4)),
        _row128(sign),
        _row128(g_q[:MLA_NOPE] * scale),
        _row128(jnp.tile(g_q[MLA_NOPE:], 2) * scale),
        _row128(g_k[:MLA_NOPE]),
        _row128(jnp.pad(g_k[MLA_NOPE:], (0, 64))),
    ]
    pos_col = positions.astype(F32).reshape(t, 1)
    qf, kf, vf = mla_qkv_prep(z, wq.astype(BF16), w_kvb.astype(BF16), g_qa, g_kva, pos_col, rows, b, s)
    o = attention(qf, kf, vf, fox=False)
    return norm_matmul(o, w_o.astype(BF16), residual=x, tn=w_o.shape[1], name="mla_out")


def fox_mixer(x, gain, w_in, b_f, g_q, g_k, w_o, b, s):
    nh = FOX_HEADS
    d = x.shape[1]
    zmain = norm_matmul(x, w_in.astype(BF16), gain=gain, n_cols=4 * d, tm=1024, tn=1024, out_dtype=BF16,
                        name="fox_in")
    w_f = jnp.pad(w_in[:, 4 * d:], ((0, 0), (0, LANES - nh))).astype(BF16)
    flog = norm_matmul(x, w_f, gain=gain, tm=1024, name="fox_f")
    scale = LOG2_E / math.sqrt(FOX_HEAD_DIM)
    rows = [_row128(jnp.pad(b_f, (0, LANES - nh))), _row128(g_q * scale), _row128(g_k)]
    qf, kf, vf, ct = fox_prep(zmain, flog, rows, b, s)
    crow = ct[:, :nh, :]
    o = attention(qf, kf, vf, fox=True, ccol=crow[..., None], crow=crow, gate=zmain, gate_col0=3 * nh)
    return norm_matmul(o, w_o.astype(BF16), residual=x, tn=w_o.shape[1], name="fox_out")


def kernel(x, positions, norm_mix_g, norm_ffn_g, mla_w_in, mla_g_qa, mla_w_qb, mla_g_kva, mla_w_kvb, mla_g_q, mla_g_k, mla_w_o, fox_w_in, fox_b_f, fox_g_q, fox_g_k, fox_w_o, peer_w_q, peer_sub_keys, peer_u, peer_v):
    b, s, d = x.shape
    depth = norm_mix_g.shape[0]
    xt = x.reshape(b * s, d)
    peer_w = peer_weights(peer_w_q, peer_sub_keys, peer_u, peer_v)
    for i in range(depth):
        j = i // 2
        if i % 2 == 0:
            xt = mla_mixer(xt, norm_mix_g[i], positions, mla_w_in[j], mla_g_qa[j], mla_w_qb[j], mla_g_kva[j],
                           mla_w_kvb[j], mla_g_q[j], mla_g_k[j], mla_w_o[j], b, s)
        else:
            xt = fox_mixer(xt, norm_mix_g[i], fox_w_in[j], fox_b_f[j], fox_g_q[j], fox_g_k[j], fox_w_o[j], b, s)
        xt = peer_ffn(xt, norm_ffn_g[i], peer_w, i)
    return xt.reshape(b, s, d)
```
